```python
import math
import jax, jax.numpy as jnp
from jax import lax
import numpy as np

D_MODEL = 2048
BATCH = 2
SEQ = 8192
DEPTH = 1

DIFF_HEADS = 8
DIFF_HEAD_DIM = 64
MOBA_HEADS = 8
MOBA_HEAD_DIM = 128
MOBA_BLOCK = 256
MOBA_TOPK = 3
MOBA_Q_CHUNK = 64
ATTN_Q_BLOCK = 128
REL_BUCKETS = 32
REL_MAX_DIST = 128
N_EXPERTS = 64
N_GROUPS = 8
TOPK_GROUPS = 4
TOPK_EXPERTS = 8
EXPERT_DIM = 512
SHARED_DIM = 512
ROUTED_SCALE = 2.5
MOE_ROW_BLOCK = 128
RMS_EPS = 1e-6

DIFF_QK_W = DIFF_HEADS * 2 * DIFF_HEAD_DIM
DIFF_V_W = DIFF_HEADS * 2 * DIFF_HEAD_DIM
MOBA_W = MOBA_HEADS * MOBA_HEAD_DIM
IN_COLS = 2 * DIFF_QK_W + DIFF_V_W + 3 * MOBA_W + 2 * D_MODEL

kernel_name = 'hybrid_diffattn_moba_moe_adaln'


def _rmsnorm(x, gain):
    xf = x.astype(jnp.float32)
    y = xf * lax.rsqrt(jnp.mean(xf * xf, axis=-1, keepdims=True) + RMS_EPS)
    return (y * gain.astype(jnp.float32)).astype(x.dtype)


def _t5_bucket(rel):
    n = jnp.maximum(rel, 0)
    max_exact = REL_BUCKETS // 2
    nf = jnp.maximum(n, 1).astype(jnp.float32)
    large = max_exact + (jnp.log(nf / max_exact) / math.log(REL_MAX_DIST / max_exact)
                         * (REL_BUCKETS - max_exact)).astype(jnp.int32)
    large = jnp.minimum(large, REL_BUCKETS - 1)
    return jnp.where(n < max_exact, n, large)


def _diff_attention(q, k, v, lam_vecs, subln_g, table, layer_idx):
    B, S = q.shape[0], q.shape[1]
    H, dh, QB = DIFF_HEADS, DIFF_HEAD_DIM, ATTN_Q_BLOCK
    lam_init = 0.8 - 0.6 * math.exp(-0.3 * layer_idx)
    lv = lam_vecs.astype(jnp.float32)
    lam = jnp.exp(jnp.sum(lv[0] * lv[1])) - jnp.exp(jnp.sum(lv[2] * lv[3])) + lam_init
    qh = q.transpose(0, 2, 3, 1, 4)
    kh = k.transpose(0, 2, 3, 1, 4)
    vh = v.transpose(0, 2, 1, 3)
    nqb = S // QB
    q_blocks = qh.reshape(B, H, 2, nqb, QB, dh).transpose(3, 0, 1, 2, 4, 5)
    k_pos = jnp.arange(S)
    scale = dh ** -0.5

    def one_block(args):
        qb, start = args
        q_pos = start + jnp.arange(QB)
        rel = q_pos[:, None] - k_pos[None, :]
        bias = table[_t5_bucket(rel)].astype(jnp.float32).transpose(2, 0, 1)
        logits = jnp.einsum('bhmqd,bhmkd->bhmqk', qb, kh).astype(jnp.float32) * scale + bias[None, :, None]
        logits = jnp.where(rel >= 0, logits, -jnp.inf)
        p = jax.nn.softmax(logits, axis=-1)
        a = p[:, :, 0] - lam * p[:, :, 1]
        return jnp.einsum('bhqk,bhkd->bhqd', a.astype(vh.dtype), vh)

    o = lax.map(one_block, (q_blocks, jnp.arange(nqb, dtype=jnp.int32) * QB))
    o = _rmsnorm(o, subln_g) * (1.0 - lam_init)
    return o.transpose(1, 0, 3, 2, 4).reshape(B, S, H * 2 * dh)


def _moba_attention(q, k, v, table):
    B, S = q.shape[0], q.shape[1]
    H, dh, BLK, QC = MOBA_HEADS, MOBA_HEAD_DIM, MOBA_BLOCK, MOBA_Q_CHUNK
    qh = q.transpose(0, 2, 1, 3)
    kh = k.transpose(0, 2, 1, 3)
    vh = v.transpose(0, 2, 1, 3)
    nb = -(-S // BLK)
    pad = nb * BLK - S
    kblk = jnp.pad(kh, ((0, 0), (0, 0), (0, pad), (0, 0))).reshape(B, H, nb, BLK, dh)
    vblk = jnp.pad(vh, ((0, 0), (0, 0), (0, pad), (0, 0))).reshape(B, H, nb, BLK, dh)
    kmean = jnp.mean(kblk.astype(jnp.float32), axis=3)
    topk = min(MOBA_TOPK, nb)
    nqc = S // QC
    q_chunks = qh.reshape(B, H, nqc, QC, dh).transpose(2, 0, 1, 3, 4)
    blk_pos = jnp.arange(BLK)
    head_ids = jnp.arange(H)
    scale = dh ** -0.5
    gather = jax.vmap(jax.vmap(lambda t, i: t[i]))

    def one_chunk(args):
        qc, start = args
        cur = start // BLK
        q_pos = start + jnp.arange(QC)
        gate = jnp.einsum('bhqd,bhnd->bhqn', qc.astype(jnp.float32), kmean)
        gate = jnp.where(jnp.arange(nb) < cur, gate, -jnp.inf)
        _, idx = lax.top_k(gate, topk)
        valid = idx < cur
        ksel = gather(kblk, idx)
        vsel = gather(vblk, idx)
        kown = lax.dynamic_index_in_dim(kblk, cur, axis=2, keepdims=False)
        vown = lax.dynamic_index_in_dim(vblk, cur, axis=2, keepdims=False)
        s_sel = jnp.einsum('bhqd,bhqtkd->bhqtk', qc, ksel).astype(jnp.float32) * scale
        rel_sel = q_pos[None, None, :, None, None] - (idx[..., None] * BLK + blk_pos)
        s_sel = s_sel + table[head_ids[None, :, None, None, None], _t5_bucket(rel_sel)].astype(jnp.float32)
        s_sel = jnp.where(valid[..., None], s_sel, -jnp.inf)
        s_own = jnp.einsum('bhqd,bhkd->bhqk', qc, kown).astype(jnp.float32) * scale
        rel_own = q_pos[:, None] - (cur * BLK + blk_pos)[None, :]
        s_own = s_own + table[:, _t5_bucket(rel_own)].astype(jnp.float32)[None]
        s_own = jnp.where(rel_own >= 0, s_own, -jnp.inf)
        logits = jnp.concatenate([s_own, s_sel.reshape(B, H, QC, topk * BLK)], axis=-1)
        p = jax.nn.softmax(logits, axis=-1).astype(vh.dtype)
        p_sel = p[..., BLK:].reshape(B, H, QC, topk, BLK)
        return (jnp.einsum('bhqk,bhkd->bhqd', p[..., :BLK], vown)
                + jnp.einsum('bhqtk,bhqtkd->bhqd', p_sel, vsel))

    o = lax.map(one_chunk, (q_chunks, jnp.arange(nqc, dtype=jnp.int32) * QC))
    return o.transpose(1, 0, 3, 2, 4).reshape(B, S, H * dh)


def _moe(h, w_router, router_bias, w_gate, w_up, w_down, w_sh_gate, w_sh_up, w_sh_down):
    B, S, D = h.shape
    T = B * S
    E, K, M = N_EXPERTS, TOPK_EXPERTS, MOE_ROW_BLOCK
    ht = h.reshape(T, D)
    scores = jax.nn.sigmoid(jnp.matmul(ht, w_router).astype(jnp.float32))
    sel = scores + router_bias.astype(jnp.float32)
    grp = sel.reshape(T, N_GROUPS, E // N_GROUPS)
    grp_score = jnp.sum(lax.top_k(grp, 2)[0], axis=-1)
    _, gidx = lax.top_k(grp_score, TOPK_GROUPS)
    gmask = jnp.sum(jax.nn.one_hot(gidx, N_GROUPS, dtype=jnp.int32), axis=-2) > 0
    sel = jnp.where(jnp.repeat(gmask, E // N_GROUPS, axis=-1), sel, -jnp.inf)
    _, eidx = lax.top_k(sel, K)
    w = jnp.take_along_axis(scores, eidx, axis=-1)
    w = w / jnp.sum(w, axis=-1, keepdims=True) * ROUTED_SCALE
    A = T * K
    flat_e = eidx.reshape(A)
    flat_tok = jnp.repeat(jnp.arange(T, dtype=jnp.int32), K)
    order = jnp.argsort(flat_e)
    e_sorted = flat_e[order]
    tok_sorted = flat_tok[order]
    w_sorted = w.reshape(A)[order]
    counts = jnp.bincount(flat_e, length=E)
    starts = jnp.cumsum(counts) - counts
    padded = ((counts + M - 1) // M) * M
    pad_ends = jnp.cumsum(padded)
    dest = (pad_ends - padded)[e_sorted] + jnp.arange(A) - starts[e_sorted]
    n_blocks = -(-A // M) + E
    P = n_blocks * M
    row_tok = jnp.zeros((P,), jnp.int32).at[dest].set(tok_sorted)
    row_w = jnp.zeros((P,), jnp.float32).at[dest].set(w_sorted)
    block_expert = jnp.minimum(jnp.searchsorted(pad_ends // M, jnp.arange(n_blocks), side='right'), E - 1)
    x_pad = ht[row_tok].reshape(n_blocks, M, D)

    def expert_block(args):
        xb, e = args
        g = jnp.matmul(xb, w_gate[e])
        u = jnp.matmul(xb, w_up[e])
        return jnp.matmul(jax.nn.silu(g) * u, w_down[e])

    y_pad = lax.map(expert_block, (x_pad, block_expert)).reshape(P, D)
    routed = jax.ops.segment_sum(y_pad.astype(jnp.float32) * row_w[:, None], row_tok, num_segments=T)
    shared = jnp.matmul(jax.nn.silu(jnp.matmul(ht, w_sh_gate)) * jnp.matmul(ht, w_sh_up), w_sh_down)
    return (routed.astype(h.dtype) + shared).reshape(B, S, D)


def setup_inputs(seed: int = 0) -> dict:
    key = jax.random.key(seed)
    ks = jax.random.split(key, 24)
    f32 = jnp.float32
    D, L = D_MODEL, DEPTH
    nrm = lambda k, shape, s: jax.random.normal(k, shape, f32) * s
    return {
        'x': nrm(ks[0], (BATCH, SEQ, D), 1.0),
        'c': nrm(ks[1], (BATCH, D), 1.0),
        'w_ada': nrm(ks[2], (L, D, 6 * D), 0.5 * D ** -0.5),
        'b_ada': nrm(ks[3], (L, 6 * D), 0.02),
        'g_mix': 1.0 + nrm(ks[4], (L, D), 0.05),
        'g_ffn': 1.0 + nrm(ks[5], (L, D), 0.05),
        'w_in': nrm(ks[6], (L, D, IN_COLS), D ** -0.5),
        'diff_lambda': nrm(ks[7], (L, 4, DIFF_HEAD_DIM), 0.1),
        'diff_subln_g': 1.0 + nrm(ks[8], (L, 2 * DIFF_HEAD_DIM), 0.05),
        'rel_bias': nrm(ks[9], (REL_BUCKETS, DIFF_HEADS + MOBA_HEADS), 0.5),
        'w_o_diff': nrm(ks[10], (L, DIFF_V_W, D), DIFF_V_W ** -0.5),
        'w_o_moba': nrm(ks[11], (L, MOBA_W, D), MOBA_W ** -0.5),
        'w_out': nrm(ks[12], (L, D, D), D ** -0.5),
        'w_router': nrm(ks[13], (L, D, N_EXPERTS), D ** -0.5),
        'router_bias': nrm(ks[14], (L, N_EXPERTS), 0.01),
        'w_exp_gate': nrm(ks[15], (L, N_EXPERTS, D, EXPERT_DIM), D ** -0.5),
        'w_exp_up': nrm(ks[16], (L, N_EXPERTS, D, EXPERT_DIM), D ** -0.5),
        'w_exp_down': nrm(ks[17], (L, N_EXPERTS, EXPERT_DIM, D), EXPERT_DIM ** -0.5),
        'w_sh_gate': nrm(ks[18], (L, D, SHARED_DIM), D ** -0.5),
        'w_sh_up': nrm(ks[19], (L, D, SHARED_DIM), D ** -0.5),
        'w_sh_down': nrm(ks[20], (L, SHARED_DIM, D), SHARED_DIM ** -0.5),
        'g_final': 1.0 + nrm(ks[21], (D,), 0.05),
    }


def reference(x, c, w_ada, b_ada, g_mix, g_ffn, w_in, diff_lambda, diff_subln_g, rel_bias,
              w_o_diff, w_o_moba, w_out, w_router, router_bias, w_exp_gate, w_exp_up, w_exp_down,
              w_sh_gate, w_sh_up, w_sh_down, g_final):
    B, S, D = x.shape
    table_diff = rel_bias[:, :DIFF_HEADS]
    table_moba = rel_bias[:, DIFF_HEADS:].T
    c_act = jax.nn.silu(c)
    o_qd = 0
    o_kd = o_qd + DIFF_QK_W
    o_vd = o_kd + DIFF_QK_W
    o_qm = o_vd + DIFF_V_W
    o_km = o_qm + MOBA_W
    o_vm = o_km + MOBA_W
    o_gd = o_vm + MOBA_W
    o_gm = o_gd + D
    for l in range(DEPTH):
        mod = jnp.matmul(c_act, w_ada[l]) + b_ada[l]
        shift1, scale1, gate1, shift2, scale2, gate2 = jnp.split(mod[:, None, :], 6, axis=-1)
        h = _rmsnorm(x, g_mix[l]) * (1.0 + scale1) + shift1
        proj = jnp.matmul(h, w_in[l])
        qd = proj[..., o_qd:o_kd].reshape(B, S, DIFF_HEADS, 2, DIFF_HEAD_DIM)
        kd = proj[..., o_kd:o_vd].reshape(B, S, DIFF_HEADS, 2, DIFF_HEAD_DIM)
        vd = proj[..., o_vd:o_qm].reshape(B, S, DIFF_HEADS, 2 * DIFF_HEAD_DIM)
        qm = proj[..., o_qm:o_km].reshape(B, S, MOBA_HEADS, MOBA_HEAD_DIM)
        km = proj[..., o_km:o_vm].reshape(B, S, MOBA_HEADS, MOBA_HEAD_DIM)
        vm = proj[..., o_vm:o_gd].reshape(B, S, MOBA_HEADS, MOBA_HEAD_DIM)
        gd = jax.nn.sigmoid(proj[..., o_gd:o_gm])
        gm = jax.nn.sigmoid(proj[..., o_gm:])
        y_d = jnp.matmul(_diff_attention(qd, kd, vd, diff_lambda[l], diff_subln_g[l], table_diff, l), w_o_diff[l])
        y_m = jnp.matmul(_moba_attention(qm, km, vm, table_moba), w_o_moba[l])
        mixed = jnp.matmul(gd * y_d + gm * y_m, w_out[l])
        x = x + gate1 * mixed
        h2 = _rmsnorm(x, g_ffn[l]) * (1.0 + scale2) + shift2
        x = x + gate2 * _moe(h2, w_router[l], router_bias[l], w_exp_gate[l], w_exp_up[l], w_exp_down[l],
                             w_sh_gate[l], w_sh_up[l], w_sh_down[l])
    return _rmsnorm(x, g_final)
```

```python
import functools
import math

import jax
import jax.numpy as jnp
from jax import lax
from jax.experimental import pallas as pl
from jax.experimental.pallas import tpu as pltpu

F32 = jnp.float32
BF16 = jnp.bfloat16

DIFF_HEADS = 8
DIFF_HEAD_DIM = 64
MOBA_HEADS = 8
MOBA_HEAD_DIM = 128
MOBA_BLOCK = 256
MOBA_TOPK = 3
REL_BUCKETS = 32
REL_MAX_DIST = 128
N_EXPERTS = 64
N_GROUPS = 8
TOPK_GROUPS = 4
TOPK_EXPERTS = 8
ROUTED_SCALE = 2.5
RMS_EPS = 1e-6
LANES = 128
NEG_BIG = -1e30
DIFF_TILE = 512
EXPERT_ROWS = 256
VMEM_LIMIT = 56 * 1024 * 1024


def _cparams(*sem):
    return pltpu.CompilerParams(dimension_semantics=sem, vmem_limit_bytes=VMEM_LIMIT)


def _dot(a, b):
    return jnp.dot(a, b, preferred_element_type=F32)


def _dot_nt(a, b):
    return lax.dot_general(a, b, (((1,), (1,)), ((), ())), preferred_element_type=F32)


def _ada_kernel(c_ref, w_ref, b_ref, o_ref):
    c = c_ref[...]
    ca = c * jax.nn.sigmoid(c)
    o_ref[...] = jnp.dot(ca, w_ref[...], preferred_element_type=F32,
                         precision=lax.Precision.HIGHEST) + b_ref[...]


def _ada(c_pad, w_ada, b_ada, tn=1024):
    rows, d = c_pad.shape
    n = w_ada.shape[1]
    return pl.pallas_call(
        _ada_kernel,
        grid=(n // tn,),
        in_specs=[pl.BlockSpec((rows, d), lambda j: (0, 0)),
                  pl.BlockSpec((d, tn), lambda j: (0, j)),
                  pl.BlockSpec((1, tn), lambda j: (0, j))],
        out_specs=pl.BlockSpec((rows, tn), lambda j: (0, j)),
        out_shape=jax.ShapeDtypeStruct((rows, n), F32),
        compiler_params=_cparams("arbitrary"),
        name="ada",
    )(c_pad, w_ada, b_ada)


def _modulated_norm(x, g, scale, shift):
    ms = jnp.mean(x * x, axis=-1, keepdims=True)
    return (x * lax.rsqrt(ms + RMS_EPS) * g) * (1.0 + scale) + shift


def _inproj_kernel(x_ref, mod_ref, g_ref, w_ref, cs_ref, o_ref, h_ref, *, chunk):
    @pl.when(pl.program_id(1) == 0)
    def _():
        shift = mod_ref[0, 0:1, :]
        scale = mod_ref[0, 1:2, :]
        g = g_ref[...]

        def body(r, carry):
            rows = pl.ds(pl.multiple_of(r * chunk, chunk), chunk)
            h_ref[rows, :] = _modulated_norm(x_ref[rows, :], g, scale, shift).astype(BF16)
            return carry

        lax.fori_loop(0, x_ref.shape[0] // chunk, body, 0)

    o_ref[...] = (_dot(h_ref[...], w_ref[...]) * cs_ref[...]).astype(BF16)


def _inproj(x2, mod3, g_mix, w_in_bf, colscale, seq, tm=1024, tn=1024):
    t, d = x2.shape
    n = w_in_bf.shape[1]
    per_batch = seq // tm
    return pl.pallas_call(
        functools.partial(_inproj_kernel, chunk=128),
        grid=(t // tm, n // tn),
        in_specs=[pl.BlockSpec((tm, d), lambda i, j: (i, 0)),
                  pl.BlockSpec((1, 6, d), lambda i, j: (i // per_batch, 0, 0)),
                  pl.BlockSpec((1, d), lambda i, j: (0, 0)),
                  pl.BlockSpec((d, tn), lambda i, j: (0, j)),
                  pl.BlockSpec((1, tn), lambda i, j: (0, j))],
        out_specs=pl.BlockSpec((tm, tn), lambda i, j: (i, j)),
        out_shape=jax.ShapeDtypeStruct((t, n), BF16),
        scratch_shapes=[pltpu.VMEM((tm, d), BF16)],
        compiler_params=_cparams("arbitrary", "arbitrary"),
        name="inproj",
    )(x2, mod3, g_mix, w_in_bf, colscale)


def _kmean_kernel(k_ref, o_ref):
    o_ref[0] = jnp.mean(k_ref[...].astype(F32), axis=0, keepdims=True)


def _kmean(proj, col_block, width):
    t = proj.shape[0]
    nblk = t // MOBA_BLOCK
    return pl.pallas_call(
        _kmean_kernel,
        grid=(nblk,),
        in_specs=[pl.BlockSpec((MOBA_BLOCK, width), lambda i: (i, col_block))],
        out_specs=pl.BlockSpec((1, 1, width), lambda i: (i, 0, 0)),
        out_shape=jax.ShapeDtypeStruct((nblk, 1, width), F32),
        compiler_params=_cparams("arbitrary"),
        name="kmean",
    )(proj)


def _softmax_first(s, v, m_ref, l_ref, acc_ref):
    m = jnp.max(s, axis=-1, keepdims=True)
    p = jnp.exp(s - m)
    m_ref[...] = m
    l_ref[...] = jnp.sum(p, axis=-1, keepdims=True)
    acc_ref[...] = _dot(p.astype(BF16), v)


def _softmax_next(s, v, m_ref, l_ref, acc_ref):
    m_old = m_ref[...]
    m = jnp.maximum(m_old, jnp.max(s, axis=-1, keepdims=True))
    a = jnp.exp(m_old - m)
    p = jnp.exp(s - m)
    m_ref[...] = m
    l_ref[...] = a * l_ref[...] + jnp.sum(p, axis=-1, keepdims=True)
    acc_ref[...] = a * acc_ref[...] + _dot(p.astype(BF16), v)


def _diff_kernel(lam_ref, q_ref, k_ref, v_ref, bias_ref, g_ref, o_ref,
                 m1, l1, a1, m2, l2, a2, *, t, out_scale):
    qi = pl.program_id(2)
    q = q_ref[...]
    lane = lax.broadcasted_iota(jnp.int32, q.shape, 1)
    zero = jnp.zeros_like(q)
    q1 = jnp.where(lane < DIFF_HEAD_DIM, q, zero)
    q2 = jnp.where(lane >= DIFF_HEAD_DIM, q, zero)

    def kv(kj):
        rows = pl.ds(pl.multiple_of(kj * t, t), t)
        return k_ref[rows, :], v_ref[rows, :]

    k, v = kv(qi)
    b0 = bias_ref[0, 0]
    _softmax_first(_dot_nt(q1, k) + b0, v, m1, l1, a1)
    _softmax_first(_dot_nt(q2, k) + b0, v, m2, l2, a2)

    @pl.when(qi >= 1)
    def _():
        k, v = kv(qi - 1)
        b1 = bias_ref[0, 1]
        _softmax_next(_dot_nt(q1, k) + b1, v, m1, l1, a1)
        _softmax_next(_dot_nt(q2, k) + b1, v, m2, l2, a2)

    def far(kj, carry):
        k, v = kv(kj)
        _softmax_next(_dot_nt(q1, k), v, m1, l1, a1)
        _softmax_next(_dot_nt(q2, k), v, m2, l2, a2)
        return carry

    lax.fori_loop(0, jnp.maximum(qi - 1, 0), far, 0)

    lam = lam_ref[0]
    o = a1[...] / l1[...] - lam * (a2[...] / l2[...])
    ms = jnp.mean(o * o, axis=-1, keepdims=True)
    o_ref[...] = ((o * lax.rsqrt(ms + RMS_EPS) * g_ref[...]) * out_scale).astype(BF16)


def _diff_attention(proj, lam, bias, subln_g, batch, seq, out_scale, t=DIFF_TILE):
    t = min(t, seq)
    nq = seq // t
    hw = 2 * DIFF_HEAD_DIM
    kcol = DIFF_HEADS
    vcol = 2 * DIFF_HEADS
    return pl.pallas_call(
        functools.partial(_diff_kernel, t=t, out_scale=out_scale),
        grid=(batch, DIFF_HEADS, nq),
        in_specs=[pl.BlockSpec(memory_space=pltpu.SMEM),
                  pl.BlockSpec((t, hw), lambda b, h, i: (b * nq + i, h)),
                  pl.BlockSpec((seq, hw), lambda b, h, i: (b, kcol + h)),
                  pl.BlockSpec((seq, hw), lambda b, h, i: (b, vcol + h)),
                  pl.BlockSpec((1, 2, t, t), lambda b, h, i: (h, 0, 0, 0)),
                  pl.BlockSpec((1, hw), lambda b, h, i: (0, 0))],
        out_specs=pl.BlockSpec((t, hw), lambda b, h, i: (b * nq + i, h)),
        out_shape=jax.ShapeDtypeStruct((batch * seq, DIFF_HEADS * hw), BF16),
        scratch_shapes=[pltpu.VMEM((t, 1), F32), pltpu.VMEM((t, 1), F32), pltpu.VMEM((t, hw), F32),
                        pltpu.VMEM((t, 1), F32), pltpu.VMEM((t, 1), F32), pltpu.VMEM((t, hw), F32)],
        compiler_params=_cparams("arbitrary", "arbitrary", "arbitrary"),
        name="diffattn",
    )(lam, proj, proj, proj, bias, subln_g)


def _moba_kernel(q_ref, k_ref, v_ref, km_ref, bias_ref, o_ref, m, l, acc, sel_ref, *, t):
    cur = pl.program_id(2)
    q = q_ref[...]
    nb = km_ref.shape[1]

    gate = lax.dot_general(q.astype(F32), km_ref[0], (((1,), (1,)), ((), ())),
                           preferred_element_type=F32, precision=lax.Precision.HIGHEST)
    blk = lax.broadcasted_iota(jnp.int32, gate.shape, 1)
    gate = jnp.where(blk < cur, gate, -jnp.inf)
    sel = jnp.zeros(gate.shape, F32)
    for _ in range(MOBA_TOPK):
        gmax = jnp.max(gate, axis=-1, keepdims=True)
        first = jnp.min(jnp.where(gate == gmax, blk, nb), axis=-1, keepdims=True)
        pick = (blk == first) & (gmax > -jnp.inf)
        sel = jnp.where(pick, 1.0, sel)
        gate = jnp.where(blk == first, -jnp.inf, gate)
    sel_ref[...] = sel

    def kv(j):
        rows = pl.ds(pl.multiple_of(j * t, t), t)
        return k_ref[rows, :], v_ref[rows, :]

    def selected(j):
        col = jnp.sum(jnp.where(blk == j, sel_ref[...], 0.0), axis=-1, keepdims=True)
        return col > 0.5

    k, v = kv(cur)
    _softmax_first(_dot_nt(q, k) + bias_ref[0, 0], v, m, l, acc)

    @pl.when(cur >= 1)
    def _():
        k, v = kv(cur - 1)
        s = _dot_nt(q, k) + bias_ref[0, 1]
        _softmax_next(jnp.where(selected(cur - 1), s, NEG_BIG), v, m, l, acc)

    def far(j, carry):
        k, v = kv(j)
        _softmax_next(jnp.where(selected(j), _dot_nt(q, k), NEG_BIG), v, m, l, acc)
        return carry

    lax.fori_loop(0, jnp.maximum(cur - 1, 0), far, 0)
    o_ref[...] = (acc[...] / l[...]).astype(BF16)


def _moba_attention(proj, kmean, bias, batch, seq, qcol, kcol, vcol):
    t = MOBA_BLOCK
    nq = seq // t
    dh = MOBA_HEAD_DIM
    nb = kmean.shape[1]
    return pl.pallas_call(
        functools.partial(_moba_kernel, t=t),
        grid=(batch, MOBA_HEADS, nq),
        in_specs=[pl.BlockSpec((t, dh), lambda b, h, i: (b * nq + i, qcol + h)),
                  pl.BlockSpec((seq, dh), lambda b, h, i: (b, kcol + h)),
                  pl.BlockSpec((seq, dh), lambda b, h, i: (b, vcol + h)),
                  pl.BlockSpec((1, nb, dh), lambda b, h, i: (b, 0, h)),
                  pl.BlockSpec((1, 2, t, t), lambda b, h, i: (h, 0, 0, 0))],
        out_specs=pl.BlockSpec((t, dh), lambda b, h, i: (b * nq + i, h)),
        out_shape=jax.ShapeDtypeStruct((batch * seq, MOBA_HEADS * dh), BF16),
        scratch_shapes=[pltpu.VMEM((t, 1), F32), pltpu.VMEM((t, 1), F32), pltpu.VMEM((t, dh), F32),
                        pltpu.VMEM((t, nb), F32)],
        compiler_params=_cparams("arbitrary", "arbitrary", "arbitrary"),
        name="moba",
    )(proj, proj, proj, kmean, bias)


def _first_index_of_max(vals, ids, sentinel):
    vmax = jnp.max(vals, axis=0, keepdims=True)
    first = jnp.min(jnp.where(vals == vmax, ids, sentinel), axis=0, keepdims=True)
    return vmax, first


def _route(scores_t, bias_t):
    e, n = scores_t.shape
    per = e // N_GROUPS
    selv = scores_t + bias_t
    sub = lax.broadcasted_iota(jnp.int32, (per, n), 0)
    gscore = []
    for g in range(N_GROUPS):
        blk = selv[g * per:(g + 1) * per, :]
        top1, first = _first_index_of_max(blk, sub, per)
        top2 = jnp.max(jnp.where(sub == first, -jnp.inf, blk), axis=0, keepdims=True)
        gscore.append(top1 + top2)
    gs = jnp.concatenate(gscore, axis=0)
    gid = lax.broadcasted_iota(jnp.int32, gs.shape, 0)
    gsel = jnp.zeros(gs.shape, F32)
    for _ in range(TOPK_GROUPS):
        _, first = _first_index_of_max(gs, gid, N_GROUPS)
        gsel = jnp.where(gid == first, 1.0, gsel)
        gs = jnp.where(gid == first, -jnp.inf, gs)
    masked = jnp.concatenate(
        [jnp.where(gsel[g:g + 1, :] > 0.5, selv[g * per:(g + 1) * per, :], -jnp.inf)
         for g in range(N_GROUPS)], axis=0)
    eid = lax.broadcasted_iota(jnp.int32, masked.shape, 0)
    picked = jnp.zeros(masked.shape, F32)
    for _ in range(TOPK_EXPERTS):
        _, first = _first_index_of_max(masked, eid, e)
        picked = jnp.where(eid == first, 1.0, picked)
        masked = jnp.where(eid == first, -jnp.inf, masked)
    w = jnp.where(picked > 0.5, scores_t, 0.0)
    return w / jnp.sum(w, axis=0, keepdims=True) * ROUTED_SCALE


def _mix_kernel(od_ref, om_ref, gd_ref, gm_ref, x_ref, mod_ref, wod_ref, wom_ref, wout_ref,
                gffn_ref, wr_ref, rb_ref, x1_ref, h2_ref, wt_ref):
    yd = _dot(od_ref[...], wod_ref[...])
    ym = _dot(om_ref[...], wom_ref[...])
    z = jax.nn.sigmoid(gd_ref[...].astype(F32)) * yd + jax.nn.sigmoid(gm_ref[...].astype(F32)) * ym
    mixed = _dot(z.astype(BF16), wout_ref[...])
    x1 = x_ref[...] + mod_ref[0, 2:3, :] * mixed
    x1_ref[...] = x1
    h2 = _modulated_norm(x1, gffn_ref[...], mod_ref[0, 4:5, :], mod_ref[0, 3:4, :])
    h2_ref[...] = h2.astype(BF16)
    logits_t = lax.dot_general(wr_ref[...], h2, (((1,), (1,)), ((), ())),
                               preferred_element_type=F32, precision=lax.Precision.HIGHEST)
    wt_ref[...] = _route(jax.nn.sigmoid(logits_t), rb_ref[...])


def _mix(od, om, proj, x2, mod3, wod, wom, wout, g_ffn, wr_t, rb_t, seq, gd_col, gm_col, tm=256):
    t, d = x2.shape
    per_batch = seq // tm
    const = lambda i: (0, 0)
    return pl.pallas_call(
        _mix_kernel,
        grid=(t // tm,),
        in_specs=[pl.BlockSpec((tm, od.shape[1]), lambda i: (i, 0)),
                  pl.BlockSpec((tm, om.shape[1]), lambda i: (i, 0)),
                  pl.BlockSpec((tm, d), lambda i: (i, gd_col)),
                  pl.BlockSpec((tm, d), lambda i: (i, gm_col)),
                  pl.BlockSpec((tm, d), lambda i: (i, 0)),
                  pl.BlockSpec((1, 6, d), lambda i: (i // per_batch, 0, 0)),
                  pl.BlockSpec(wod.shape, const),
                  pl.BlockSpec(wom.shape, const),
                  pl.BlockSpec(wout.shape, const),
                  pl.BlockSpec((1, d), const),
                  pl.BlockSpec(wr_t.shape, const),
                  pl.BlockSpec(rb_t.shape, const)],
        out_specs=[pl.BlockSpec((tm, d), lambda i: (i, 0)),
                   pl.BlockSpec((tm, d), lambda i: (i, 0)),
                   pl.BlockSpec((N_EXPERTS, tm), lambda i: (0, i))],
        out_shape=[jax.ShapeDtypeStruct((t, d), F32),
                   jax.ShapeDtypeStruct((t, d), BF16),
                   jax.ShapeDtypeStruct((N_EXPERTS, t), F32)],
        compiler_params=_cparams("arbitrary"),
        name="mix",
    )(od, om, proj, proj, x2, mod3, wod, wom, wout, g_ffn, wr_t, rb_t)


def _expert_kernel(be_ref, na_ref, x_ref, wg_ref, wu_ref, wd_ref, o_ref, wg_s, wu_s, wd_s):
    i = pl.program_id(0)

    @pl.when(i < na_ref[0])
    def _():
        prev = be_ref[jnp.maximum(i - 1, 0)]

        @pl.when((i == 0) | (be_ref[i] != prev))
        def _():
            wg_s[...] = wg_ref[0].astype(BF16)
            wu_s[...] = wu_ref[0].astype(BF16)
            wd_s[...] = wd_ref[0].astype(BF16)

        x = x_ref[...]
        g = _dot(x, wg_s[...])
        u = _dot(x, wu_s[...])
        a = (g * jax.nn.sigmoid(g) * u).astype(BF16)
        o_ref[...] = _dot(a, wd_s[...]).astype(BF16)


def _experts(block_expert, n_active, x_pad, w_gate, w_up, w_down, rows=EXPERT_ROWS):
    p, d = x_pad.shape
    f = w_gate.shape[2]
    n_blocks = p // rows
    def xmap(i, be, na):
        return (jnp.minimum(i, na[0] - 1), 0)

    def wmap(i, be, na):
        return (be[jnp.minimum(i, na[0] - 1)], 0, 0)

    grid_spec = pltpu.PrefetchScalarGridSpec(
        num_scalar_prefetch=2,
        grid=(n_blocks,),
        in_specs=[pl.BlockSpec((rows, d), xmap),
                  pl.BlockSpec((1, d, f), wmap),
                  pl.BlockSpec((1, d, f), wmap),
                  pl.BlockSpec((1, f, d), wmap)],
        out_specs=pl.BlockSpec((rows, d), xmap),
        scratch_shapes=[pltpu.VMEM((d, f), BF16), pltpu.VMEM((d, f), BF16), pltpu.VMEM((f, d), BF16)],
    )
    return pl.pallas_call(
        _expert_kernel,
        grid_spec=grid_spec,
        out_shape=jax.ShapeDtypeStruct((p, d), BF16),
        compiler_params=_cparams("arbitrary"),
        name="experts",
    )(block_expert, n_active, x_pad, w_gate, w_up, w_down)


def _final_kernel(x1_ref, h2_ref, r_ref, mod_ref, wsg_ref, wsu_ref, wsd_ref, gf_ref, o_ref):
    h2 = h2_ref[...]
    g = _dot(h2, wsg_ref[...])
    u = _dot(h2, wsu_ref[...])
    shared = _dot((g * jax.nn.sigmoid(g) * u).astype(BF16), wsd_ref[...])
    x2 = x1_ref[...] + mod_ref[0, 5:6, :] * (r_ref[...] + shared)
    ms = jnp.mean(x2 * x2, axis=-1, keepdims=True)
    o_ref[...] = x2 * lax.rsqrt(ms + RMS_EPS) * gf_ref[...]


def _final(x1, h2, routed, mod3, wsg, wsu, wsd, g_final, seq, tm=256):
    t, d = x1.shape
    per_batch = seq // tm
    const = lambda i: (0, 0)
    row = pl.BlockSpec((tm, d), lambda i: (i, 0))
    return pl.pallas_call(
        _final_kernel,
        grid=(t // tm,),
        in_specs=[row, row, row,
                  pl.BlockSpec((1, 6, d), lambda i: (i // per_batch, 0, 0)),
                  pl.BlockSpec(wsg.shape, const),
                  pl.BlockSpec(wsu.shape, const),
                  pl.BlockSpec(wsd.shape, const),
                  pl.BlockSpec((1, d), const)],
        out_specs=row,
        out_shape=jax.ShapeDtypeStruct((t, d), F32),
        compiler_params=_cparams("arbitrary"),
        name="final",
    )(x1, h2, routed, mod3, wsg, wsu, wsd, g_final)


def _bucket(rel):
    n = jnp.maximum(rel, 0)
    max_exact = REL_BUCKETS // 2
    nf = jnp.maximum(n, 1).astype(F32)
    large = max_exact + (jnp.log(nf / max_exact) / math.log(REL_MAX_DIST / max_exact)
                         * (REL_BUCKETS - max_exact)).astype(jnp.int32)
    large = jnp.minimum(large, REL_BUCKETS - 1)
    return jnp.where(n < max_exact, n, large)


def _near_bias(table, t):
    assert t >= REL_MAX_DIST
    r = jnp.arange(t)[:, None]
    c = jnp.arange(t)[None, :]
    far = table[:, REL_BUCKETS - 1][:, None, None]
    b0 = jnp.where((r - c) >= 0, table[:, _bucket(r - c)] - far, NEG_BIG)
    b1 = table[:, _bucket(t + r - c)] - far
    return jnp.stack([b0, b1], axis=1).astype(F32)


def _dispatch(wt, rows):
    e, t = wt.shape
    k = TOPK_EXPERTS
    a = t * k
    w_tok, eidx = lax.top_k(wt.T, k)
    sel = (wt > 0).astype(jnp.int32)
    counts = jnp.sum(sel, axis=1)
    rank = jnp.cumsum(sel, axis=1) - sel
    padded = ((counts + rows - 1) // rows) * rows
    pad_ends = jnp.cumsum(padded)
    pad_starts = pad_ends - padded
    starts = jnp.cumsum(counts) - counts
    pos = pad_starts[eidx] + jnp.take_along_axis(rank.T, eidx, axis=1)
    n_blocks = a // rows + e
    n_active = (pad_ends[-1] // rows).astype(jnp.int32)
    block_expert = jnp.minimum(
        jnp.searchsorted(pad_ends // rows, jnp.arange(n_blocks), side='right'), e - 1).astype(jnp.int32)
    order = jnp.argsort(eidx.reshape(a))
    tok_sorted = (order // k).astype(jnp.int32)
    p = jnp.arange(n_blocks * rows)
    ep = jnp.repeat(block_expert, rows)
    j = p - pad_starts[ep]
    row_tok = jnp.where(j < counts[ep], tok_sorted[jnp.clip(starts[ep] + j, 0, a - 1)], 0)
    return w_tok, pos, row_tok, block_expert, n_active.reshape(1)


def kernel(x, c, w_ada, b_ada, g_mix, g_ffn, w_in, diff_lambda, diff_subln_g, rel_bias, w_o_diff, w_o_moba,
           w_out, w_router, router_bias, w_exp_gate, w_exp_up, w_exp_down, w_sh_gate, w_sh_up, w_sh_down,
           g_final):
    batch, seq, d = x.shape
    t = batch * seq
    depth = w_ada.shape[0]
    assert seq % MOBA_BLOCK == 0 and seq % min(DIFF_TILE, seq) == 0
    hw = 2 * DIFF_HEAD_DIM
    qk_w = DIFF_HEADS * hw
    moba_w = MOBA_HEADS * MOBA_HEAD_DIM
    o_qm = 3 * qk_w
    o_km = o_qm + moba_w
    o_vm = o_km + moba_w
    o_gd = o_vm + moba_w
    o_gm = o_gd + d
    table_diff = rel_bias[:, :DIFF_HEADS].T
    table_moba = rel_bias[:, DIFF_HEADS:].T
    bias_diff = _near_bias(table_diff, min(DIFF_TILE, seq))
    bias_moba = _near_bias(table_moba, MOBA_BLOCK)
    colscale = jnp.ones((1, w_in.shape[2]), F32)
    colscale = colscale.at[:, :qk_w].set(DIFF_HEAD_DIM ** -0.5)
    colscale = colscale.at[:, o_qm:o_km].set(MOBA_HEAD_DIM ** -0.5)
    c_pad = jnp.zeros((8, d), F32).at[:batch].set(c)

    assert depth == 1, "single-layer block: the final norm is fused into the last kernel"
    l = 0
    xc = x.reshape(t, d)
    mod3 = _ada(c_pad, w_ada[l], b_ada[l][None, :])[:batch].reshape(batch, 6, d)
    proj = _inproj(xc, mod3, g_mix[l][None, :], w_in[l].astype(BF16), colscale, seq, tm=min(1024, seq))
    lam_init = 0.8 - 0.6 * math.exp(-0.3 * l)
    lv = diff_lambda[l].astype(F32)
    lam = (jnp.exp(jnp.sum(lv[0] * lv[1])) - jnp.exp(jnp.sum(lv[2] * lv[3])) + lam_init).reshape(1)
    od = _diff_attention(proj, lam, bias_diff, diff_subln_g[l][None, :], batch, seq, 1.0 - lam_init)
    kmean = _kmean(proj, o_km // moba_w, moba_w).reshape(batch, seq // MOBA_BLOCK, moba_w)
    om = _moba_attention(proj, kmean, bias_moba, batch, seq,
                         o_qm // MOBA_HEAD_DIM, o_km // MOBA_HEAD_DIM, o_vm // MOBA_HEAD_DIM)
    x1, h2, wt = _mix(od, om, proj, xc, mod3, w_o_diff[l].astype(BF16), w_o_moba[l].astype(BF16),
                      w_out[l].astype(BF16), g_ffn[l][None, :], w_router[l].T,
                      router_bias[l][:, None], seq, o_gd // d, o_gm // d)
    w_tok, pos, row_tok, block_expert, n_active = _dispatch(wt, EXPERT_ROWS)
    x_pad = h2[row_tok]
    y_pad = _experts(block_expert, n_active, x_pad, w_exp_gate[l], w_exp_up[l], w_exp_down[l])
    routed = jnp.sum(y_pad[pos].astype(F32) * w_tok[:, :, None], axis=1)
    out = _final(x1, h2, routed, mod3, w_sh_gate[l].astype(BF16), w_sh_up[l].astype(BF16),
                 w_sh_down[l].astype(BF16), g_final[None, :], seq)
    return out.reshape(batch, seq, d)
```

```python
import functools
import math

import jax
import jax.numpy as jnp
from jax import lax
from jax.experimental import pallas as pl
from jax.experimental.pallas import tpu as pltpu

F32 = jnp.float32
BF16 = jnp.bfloat16

DIFF_HEADS = 8
DIFF_HEAD_DIM = 64
MOBA_HEADS = 8
MOBA_HEAD_DIM = 128
MOBA_BLOCK = 256
MOBA_TOPK = 3
REL_BUCKETS = 32
REL_MAX_DIST = 128
N_EXPERTS = 64
N_GROUPS = 8
TOPK_GROUPS = 4
TOPK_EXPERTS = 8
ROUTED_SCALE = 2.5
RMS_EPS = 1e-6
LANES = 128
NEG_BIG = -1e30
LOG2E = 1.4426950408889634
DIFF_TILE = 512
EXPERT_ROWS = 256
VMEM_LIMIT = 56 * 1024 * 1024


def _cparams(*sem):
    return pltpu.CompilerParams(dimension_semantics=sem, vmem_limit_bytes=VMEM_LIMIT)


def _dot(a, b):
    return jnp.dot(a, b, preferred_element_type=F32)


def _dot_nt(a, b):
    return lax.dot_general(a, b, (((1,), (1,)), ((), ())), preferred_element_type=F32)


def _ada_kernel(c_ref, w_ref, b_ref, o_ref):
    c = c_ref[...]
    ca = c * jax.nn.sigmoid(c)
    o_ref[...] = jnp.dot(ca, w_ref[...], preferred_element_type=F32,
                         precision=lax.Precision.HIGHEST) + b_ref[...]


def _ada(c_pad, w_ada, b_ada, tn=1024):
    rows, d = c_pad.shape
    n = w_ada.shape[1]
    return pl.pallas_call(
        _ada_kernel,
        grid=(n // tn,),
        in_specs=[pl.BlockSpec((rows, d), lambda j: (0, 0)),
                  pl.BlockSpec((d, tn), lambda j: (0, j)),
                  pl.BlockSpec((1, tn), lambda j: (0, j))],
        out_specs=pl.BlockSpec((rows, tn), lambda j: (0, j)),
        out_shape=jax.ShapeDtypeStruct((rows, n), F32),
        compiler_params=_cparams("arbitrary"),
        name="ada",
    )(c_pad, w_ada, b_ada)


def _modulated_norm(x, g, scale, shift):
    ms = jnp.mean(x * x, axis=-1, keepdims=True)
    return (x * lax.rsqrt(ms + RMS_EPS) * g) * (1.0 + scale) + shift


def _inproj_kernel(x_ref, mod_ref, g_ref, w_ref, cs_ref, o_ref, h_ref, *, chunk):
    @pl.when(pl.program_id(1) == 0)
    def _():
        shift = mod_ref[0, 0:1, :]
        scale = mod_ref[0, 1:2, :]
        g = g_ref[...]

        def body(r, carry):
            rows = pl.ds(pl.multiple_of(r * chunk, chunk), chunk)
            h_ref[rows, :] = _modulated_norm(x_ref[rows, :], g, scale, shift).astype(BF16)
            return carry

        lax.fori_loop(0, x_ref.shape[0] // chunk, body, 0)

    o_ref[...] = (_dot(h_ref[...], w_ref[...]) * cs_ref[...]).astype(BF16)


def _inproj(x2, mod3, g_mix, w_in_bf, colscale, seq, tm=1024, tn=1024):
    t, d = x2.shape
    n = w_in_bf.shape[1]
    per_batch = seq // tm
    return pl.pallas_call(
        functools.partial(_inproj_kernel, chunk=128),
        grid=(t // tm, n // tn),
        in_specs=[pl.BlockSpec((tm, d), lambda i, j: (i, 0)),
                  pl.BlockSpec((1, 6, d), lambda i, j: (i // per_batch, 0, 0)),
                  pl.BlockSpec((1, d), lambda i, j: (0, 0)),
                  pl.BlockSpec((d, tn), lambda i, j: (0, j)),
                  pl.BlockSpec((1, tn), lambda i, j: (0, j))],
        out_specs=pl.BlockSpec((tm, tn), lambda i, j: (i, j)),
        out_shape=jax.ShapeDtypeStruct((t, n), BF16),
        scratch_shapes=[pltpu.VMEM((tm, d), BF16)],
        compiler_params=_cparams("arbitrary", "arbitrary"),
        name="inproj",
    )(x2, mod3, g_mix, w_in_bf, colscale)


def _kmean_kernel(k_ref, o_ref):
    o_ref[0] = jnp.mean(k_ref[...].astype(F32), axis=0, keepdims=True)


def _kmean(proj, col_block, width):
    t = proj.shape[0]
    nblk = t // MOBA_BLOCK
    return pl.pallas_call(
        _kmean_kernel,
        grid=(nblk,),
        in_specs=[pl.BlockSpec((MOBA_BLOCK, width), lambda i: (i, col_block))],
        out_specs=pl.BlockSpec((1, 1, width), lambda i: (i, 0, 0)),
        out_shape=jax.ShapeDtypeStruct((nblk, 1, width), F32),
        compiler_params=_cparams("arbitrary"),
        name="kmean",
    )(proj)


def _softmax_first(s, vt, m_ref, l_ref, acc_ref):
    m = jnp.max(s, axis=0, keepdims=True)
    p = jnp.exp2(s - m)
    m_ref[...] = m
    l_ref[...] = jnp.sum(p, axis=0, keepdims=True)
    acc_ref[...] = _dot(vt, p.astype(BF16))


def _softmax_next(s, vt, m_ref, l_ref, acc_ref):
    m_old = m_ref[...]
    m = jnp.maximum(m_old, jnp.max(s, axis=0, keepdims=True))
    a = jnp.exp2(m_old - m)
    p = jnp.exp2(s - m)
    m_ref[...] = m
    l_ref[...] = a * l_ref[...] + jnp.sum(p, axis=0, keepdims=True)
    acc_ref[...] = a * acc_ref[...] + _dot(vt, p.astype(BF16))


def _diff_kernel(lam_ref, qt_ref, k_ref, vt_ref, bias_ref, g_ref, o_ref, m, l, acc, *, t, out_scale):
    qi = pl.program_id(2)
    qt = qt_ref[...]
    zeros = jnp.zeros((DIFF_HEAD_DIM, t), qt.dtype)
    w12 = jnp.concatenate([jnp.concatenate([qt[:DIFF_HEAD_DIM], zeros], axis=0),
                           jnp.concatenate([zeros, qt[DIFF_HEAD_DIM:]], axis=0)], axis=1)

    def logits(kj):
        rows = pl.ds(pl.multiple_of(kj * t, t), t)
        return _dot(k_ref[rows, :], w12)

    def both(b):
        return jnp.concatenate([b, b], axis=1)

    _softmax_first(logits(qi) + both(bias_ref[0, 0]), vt_ref[qi], m, l, acc)

    @pl.when(qi >= 1)
    def _():
        _softmax_next(logits(qi - 1) + both(bias_ref[0, 1]), vt_ref[qi - 1], m, l, acc)

    def far(kj, carry):
        _softmax_next(logits(kj), vt_ref[kj], m, l, acc)
        return carry

    lax.fori_loop(0, jnp.maximum(qi - 1, 0), far, 0)

    o12 = acc[...] / l[...]
    o = o12[:, :t] - lam_ref[0] * o12[:, t:]
    ms = jnp.mean(o * o, axis=0, keepdims=True)
    o = (o * lax.rsqrt(ms + RMS_EPS) * g_ref[...]) * out_scale
    o_ref[...] = o.T.astype(BF16)


def _diff_attention(proj, qt, vt, lam, bias, subln_g, batch, seq, out_scale, t):
    nq = seq // t
    hw = 2 * DIFF_HEAD_DIM
    kcol = DIFF_HEADS
    return pl.pallas_call(
        functools.partial(_diff_kernel, t=t, out_scale=out_scale),
        grid=(batch, DIFF_HEADS, nq),
        in_specs=[pl.BlockSpec(memory_space=pltpu.SMEM),
                  pl.BlockSpec((hw, t), lambda b, h, i: (h, b * nq + i)),
                  pl.BlockSpec((seq, hw), lambda b, h, i: (b, kcol + h)),
                  pl.BlockSpec((nq, hw, t), lambda b, h, i: (b, h, 0)),
                  pl.BlockSpec((1, 2, t, t), lambda b, h, i: (h, 0, 0, 0)),
                  pl.BlockSpec((hw, 1), lambda b, h, i: (0, 0))],
        out_specs=pl.BlockSpec((t, hw), lambda b, h, i: (b * nq + i, h)),
        out_shape=jax.ShapeDtypeStruct((batch * seq, DIFF_HEADS * hw), BF16),
        scratch_shapes=[pltpu.VMEM((1, 2 * t), F32), pltpu.VMEM((1, 2 * t), F32),
                        pltpu.VMEM((hw, 2 * t), F32)],
        compiler_params=_cparams("arbitrary", "arbitrary", "arbitrary"),
        name="diffattn",
    )(lam, qt, proj, vt, bias, subln_g)


def _moba_kernel(qt_ref, k_ref, vt_ref, km_ref, bias_ref, o_ref, m, l, acc, sel_ref, *, t):
    cur = pl.program_id(2)
    qt = qt_ref[...]
    nb = km_ref.shape[1]

    gate = jnp.dot(km_ref[0], qt.astype(F32), preferred_element_type=F32,
                   precision=lax.Precision.HIGHEST)
    blk = lax.broadcasted_iota(jnp.int32, gate.shape, 0)
    gate = jnp.where(blk < cur, gate, -jnp.inf)
    sel = jnp.zeros(gate.shape, F32)
    for _ in range(MOBA_TOPK):
        gmax = jnp.max(gate, axis=0, keepdims=True)
        first = jnp.min(jnp.where(gate == gmax, blk, nb), axis=0, keepdims=True)
        pick = (blk == first) & (gmax > -jnp.inf)
        sel = jnp.where(pick, 1.0, sel)
        gate = jnp.where(blk == first, -jnp.inf, gate)
    sel_ref[...] = sel

    def logits(j0, nblk):
        rows = pl.ds(pl.multiple_of(j0 * t, t), nblk * t)
        return _dot(k_ref[rows, :], qt)

    def keep(j, valid):
        return jnp.where(valid, sel_ref[pl.ds(j, 1), :], 0.0) > 0.5

    def pair_vt(j0):
        return jnp.concatenate([vt_ref[j0], vt_ref[j0 + 1]], axis=1)

    @pl.when(cur == 0)
    def _():
        _softmax_first(logits(0, 1) + bias_ref[0, 0], vt_ref[0], m, l, acc)

    @pl.when(cur >= 1)
    def _():
        s = logits(cur - 1, 2)
        prev = jnp.where(keep(cur - 1, True), s[:t] + bias_ref[0, 1], NEG_BIG)
        own = s[t:] + bias_ref[0, 0]
        _softmax_first(jnp.concatenate([prev, own], axis=0), pair_vt(cur - 1), m, l, acc)

    nfar = jnp.maximum(cur - 1, 0)

    def far(c, carry):
        j0 = 2 * c
        s = logits(j0, 2)
        s0 = jnp.where(keep(j0, True), s[:t], NEG_BIG)
        s1 = jnp.where(keep(j0 + 1, j0 + 1 < nfar), s[t:], NEG_BIG)
        _softmax_next(jnp.concatenate([s0, s1], axis=0), pair_vt(j0), m, l, acc)
        return carry

    lax.fori_loop(0, (nfar + 1) // 2, far, 0)
    o_ref[...] = (acc[...] / l[...]).T.astype(BF16)


def _moba_attention(proj, qt, vt, kmean, bias, batch, seq, kcol):
    t = MOBA_BLOCK
    nq = seq // t
    dh = MOBA_HEAD_DIM
    nb = kmean.shape[1]
    return pl.pallas_call(
        functools.partial(_moba_kernel, t=t),
        grid=(batch, MOBA_HEADS, nq),
        in_specs=[pl.BlockSpec((dh, t), lambda b, h, i: (h, b * nq + i)),
                  pl.BlockSpec((seq, dh), lambda b, h, i: (b, kcol + h)),
                  pl.BlockSpec((nq, dh, t), lambda b, h, i: (b, h, 0)),
                  pl.BlockSpec((1, nb, dh), lambda b, h, i: (b, 0, h)),
                  pl.BlockSpec((1, 2, t, t), lambda b, h, i: (h, 0, 0, 0))],
        out_specs=pl.BlockSpec((t, dh), lambda b, h, i: (b * nq + i, h)),
        out_shape=jax.ShapeDtypeStruct((batch * seq, MOBA_HEADS * dh), BF16),
        scratch_shapes=[pltpu.VMEM((1, t), F32), pltpu.VMEM((1, t), F32), pltpu.VMEM((dh, t), F32),
                        pltpu.VMEM((nb, t), F32)],
        compiler_params=_cparams("arbitrary", "arbitrary", "arbitrary"),
        name="moba",
    )(qt, proj, vt, kmean, bias)


def _first_index_of_max(vals, ids, sentinel):
    vmax = jnp.max(vals, axis=0, keepdims=True)
    first = jnp.min(jnp.where(vals == vmax, ids, sentinel), axis=0, keepdims=True)
    return vmax, first


def _route(scores_t, bias_t):
    e, n = scores_t.shape
    per = e // N_GROUPS
    selv = scores_t + bias_t
    sub = lax.broadcasted_iota(jnp.int32, (per, n), 0)
    gscore = []
    for g in range(N_GROUPS):
        blk = selv[g * per:(g + 1) * per, :]
        top1, first = _first_index_of_max(blk, sub, per)
        top2 = jnp.max(jnp.where(sub == first, -jnp.inf, blk), axis=0, keepdims=True)
        gscore.append(top1 + top2)
    gs = jnp.concatenate(gscore, axis=0)
    gid = lax.broadcasted_iota(jnp.int32, gs.shape, 0)
    gsel = jnp.zeros(gs.shape, F32)
    for _ in range(TOPK_GROUPS):
        _, first = _first_index_of_max(gs, gid, N_GROUPS)
        gsel = jnp.where(gid == first, 1.0, gsel)
        gs = jnp.where(gid == first, -jnp.inf, gs)
    masked = jnp.concatenate(
        [jnp.where(gsel[g:g + 1, :] > 0.5, selv[g * per:(g + 1) * per, :], -jnp.inf)
         for g in range(N_GROUPS)], axis=0)
    eid = lax.broadcasted_iota(jnp.int32, masked.shape, 0)
    picked = jnp.zeros(masked.shape, F32)
    for _ in range(TOPK_EXPERTS):
        _, first = _first_index_of_max(masked, eid, e)
        picked = jnp.where(eid == first, 1.0, picked)
        masked = jnp.where(eid == first, -jnp.inf, masked)
    w = jnp.where(picked > 0.5, scores_t, 0.0)
    return w / jnp.sum(w, axis=0, keepdims=True) * ROUTED_SCALE


def _mix_kernel(od_ref, om_ref, gd_ref, gm_ref, x_ref, mod_ref, wod_ref, wom_ref, wout_ref,
                gffn_ref, wr_ref, rb_ref, x1_ref, h2_ref, wt_ref):
    yd = _dot(od_ref[...], wod_ref[...])
    ym = _dot(om_ref[...], wom_ref[...])
    z = jax.nn.sigmoid(gd_ref[...].astype(F32)) * yd + jax.nn.sigmoid(gm_ref[...].astype(F32)) * ym
    mixed = _dot(z.astype(BF16), wout_ref[...])
    x1 = x_ref[...] + mod_ref[0, 2:3, :] * mixed
    x1_ref[...] = x1
    h2 = _modulated_norm(x1, gffn_ref[...], mod_ref[0, 4:5, :], mod_ref[0, 3:4, :])
    h2_ref[...] = h2.astype(BF16)
    logits_t = lax.dot_general(wr_ref[...], h2, (((1,), (1,)), ((), ())),
                               preferred_element_type=F32, precision=lax.Precision.HIGHEST)
    wt_ref[...] = _route(jax.nn.sigmoid(logits_t), rb_ref[...])


def _mix(od, om, proj, x2, mod3, wod, wom, wout, g_ffn, wr_t, rb_t, seq, gd_col, gm_col, tm=256):
    t, d = x2.shape
    per_batch = seq // tm
    const = lambda i: (0, 0)
    return pl.pallas_call(
        _mix_kernel,
        grid=(t // tm,),
        in_specs=[pl.BlockSpec((tm, od.shape[1]), lambda i: (i, 0)),
                  pl.BlockSpec((tm, om.shape[1]), lambda i: (i, 0)),
                  pl.BlockSpec((tm, d), lambda i: (i, gd_col)),
                  pl.BlockSpec((tm, d), lambda i: (i, gm_col)),
                  pl.BlockSpec((tm, d), lambda i: (i, 0)),
                  pl.BlockSpec((1, 6, d), lambda i: (i // per_batch, 0, 0)),
                  pl.BlockSpec(wod.shape, const),
                  pl.BlockSpec(wom.shape, const),
                  pl.BlockSpec(wout.shape, const),
                  pl.BlockSpec((1, d), const),
                  pl.BlockSpec(wr_t.shape, const),
                  pl.BlockSpec(rb_t.shape, const)],
        out_specs=[pl.BlockSpec((tm, d), lambda i: (i, 0)),
                   pl.BlockSpec((tm, d), lambda i: (i, 0)),
                   pl.BlockSpec((N_EXPERTS, tm), lambda i: (0, i))],
        out_shape=[jax.ShapeDtypeStruct((t, d), F32),
                   jax.ShapeDtypeStruct((t, d), BF16),
                   jax.ShapeDtypeStruct((N_EXPERTS, t), F32)],
        compiler_params=_cparams("arbitrary"),
        name="mix",
    )(od, om, proj, proj, x2, mod3, wod, wom, wout, g_ffn, wr_t, rb_t)


def _expert_kernel(be_ref, na_ref, x_ref, wg_ref, wu_ref, wd_ref, o_ref, wg_s, wu_s, wd_s):
    i = pl.program_id(0)

    @pl.when(i < na_ref[0])
    def _():
        prev = be_ref[jnp.maximum(i - 1, 0)]

        @pl.when((i == 0) | (be_ref[i] != prev))
        def _():
            wg_s[...] = wg_ref[0].astype(BF16)
            wu_s[...] = wu_ref[0].astype(BF16)
            wd_s[...] = wd_ref[0].astype(BF16)

        x = x_ref[...]
        g = _dot(x, wg_s[...])
        u = _dot(x, wu_s[...])
        a = (g * jax.nn.sigmoid(g) * u).astype(BF16)
        o_ref[...] = _dot(a, wd_s[...]).astype(BF16)


def _experts(block_expert, n_active, x_pad, w_gate, w_up, w_down, rows=EXPERT_ROWS):
    p, d = x_pad.shape
    f = w_gate.shape[2]
    n_blocks = p // rows
    def xmap(i, be, na):
        return (jnp.minimum(i, na[0] - 1), 0)

    def wmap(i, be, na):
        return (be[jnp.minimum(i, na[0] - 1)], 0, 0)

    grid_spec = pltpu.PrefetchScalarGridSpec(
        num_scalar_prefetch=2,
        grid=(n_blocks,),
        in_specs=[pl.BlockSpec((rows, d), xmap),
                  pl.BlockSpec((1, d, f), wmap),
                  pl.BlockSpec((1, d, f), wmap),
                  pl.BlockSpec((1, f, d), wmap)],
        out_specs=pl.BlockSpec((rows, d), xmap),
        scratch_shapes=[pltpu.VMEM((d, f), BF16), pltpu.VMEM((d, f), BF16), pltpu.VMEM((f, d), BF16)],
    )
    return pl.pallas_call(
        _expert_kernel,
        grid_spec=grid_spec,
        out_shape=jax.ShapeDtypeStruct((p, d), BF16),
        compiler_params=_cparams("arbitrary"),
        name="experts",
    )(block_expert, n_active, x_pad, w_gate, w_up, w_down)


def _final_kernel(x1_ref, h2_ref, r_ref, mod_ref, wsg_ref, wsu_ref, wsd_ref, gf_ref, o_ref):
    h2 = h2_ref[...]
    g = _dot(h2, wsg_ref[...])
    u = _dot(h2, wsu_ref[...])
    shared = _dot((g * jax.nn.sigmoid(g) * u).astype(BF16), wsd_ref[...])
    x2 = x1_ref[...] + mod_ref[0, 5:6, :] * (r_ref[...] + shared)
    ms = jnp.mean(x2 * x2, axis=-1, keepdims=True)
    o_ref[...] = x2 * lax.rsqrt(ms + RMS_EPS) * gf_ref[...]


def _final(x1, h2, routed, mod3, wsg, wsu, wsd, g_final, seq, tm=256):
    t, d = x1.shape
    per_batch = seq // tm
    const = lambda i: (0, 0)
    row = pl.BlockSpec((tm, d), lambda i: (i, 0))
    return pl.pallas_call(
        _final_kernel,
        grid=(t // tm,),
        in_specs=[row, row, row,
                  pl.BlockSpec((1, 6, d), lambda i: (i // per_batch, 0, 0)),
                  pl.BlockSpec(wsg.shape, const),
                  pl.BlockSpec(wsu.shape, const),
                  pl.BlockSpec(wsd.shape, const),
                  pl.BlockSpec((1, d), const)],
        out_specs=row,
        out_shape=jax.ShapeDtypeStruct((t, d), F32),
        compiler_params=_cparams("arbitrary"),
        name="final",
    )(x1, h2, routed, mod3, wsg, wsu, wsd, g_final)


def _bucket(rel):
    n = jnp.maximum(rel, 0)
    max_exact = REL_BUCKETS // 2
    nf = jnp.maximum(n, 1).astype(F32)
    large = max_exact + (jnp.log(nf / max_exact) / math.log(REL_MAX_DIST / max_exact)
                         * (REL_BUCKETS - max_exact)).astype(jnp.int32)
    large = jnp.minimum(large, REL_BUCKETS - 1)
    return jnp.where(n < max_exact, n, large)


def _near_bias(table, t):
    assert t >= REL_MAX_DIST
    key = jnp.arange(t)[:, None]
    qry = jnp.arange(t)[None, :]
    far = table[:, REL_BUCKETS - 1][:, None, None]
    b0 = jnp.where((qry - key) >= 0, (table[:, _bucket(qry - key)] - far) * LOG2E, NEG_BIG)
    b1 = (table[:, _bucket(t + qry - key)] - far) * LOG2E
    return jnp.stack([b0, b1], axis=1).astype(F32)


def _dispatch(wt, rows):
    e, t = wt.shape
    k = TOPK_EXPERTS
    a = t * k
    w_tok, eidx = lax.top_k(wt.T, k)
    sel = (wt > 0).astype(jnp.int32)
    counts = jnp.sum(sel, axis=1)
    rank = jnp.cumsum(sel, axis=1) - sel
    padded = ((counts + rows - 1) // rows) * rows
    pad_ends = jnp.cumsum(padded)
    pad_starts = pad_ends - padded
    starts = jnp.cumsum(counts) - counts
    pos = pad_starts[eidx] + jnp.take_along_axis(rank.T, eidx, axis=1)
    n_blocks = a // rows + e
    n_active = (pad_ends[-1] // rows).astype(jnp.int32)
    block_expert = jnp.minimum(
        jnp.searchsorted(pad_ends // rows, jnp.arange(n_blocks), side='right'), e - 1).astype(jnp.int32)
    order = jnp.argsort(eidx.reshape(a))
    tok_sorted = (order // k).astype(jnp.int32)
    p = jnp.arange(n_blocks * rows)
    ep = jnp.repeat(block_expert, rows)
    j = p - pad_starts[ep]
    row_tok = jnp.where(j < counts[ep], tok_sorted[jnp.clip(starts[ep] + j, 0, a - 1)], 0)
    return w_tok, pos, row_tok, block_expert, n_active.reshape(1)


def kernel(x, c, w_ada, b_ada, g_mix, g_ffn, w_in, diff_lambda, diff_subln_g, rel_bias, w_o_diff, w_o_moba,
           w_out, w_router, router_bias, w_exp_gate, w_exp_up, w_exp_down, w_sh_gate, w_sh_up, w_sh_down,
           g_final):
    batch, seq, d = x.shape
    t = batch * seq
    depth = w_ada.shape[0]
    assert seq % MOBA_BLOCK == 0 and seq % min(DIFF_TILE, seq) == 0
    hw = 2 * DIFF_HEAD_DIM
    qk_w = DIFF_HEADS * hw
    moba_w = MOBA_HEADS * MOBA_HEAD_DIM
    o_qm = 3 * qk_w
    o_km = o_qm + moba_w
    o_vm = o_km + moba_w
    o_gd = o_vm + moba_w
    o_gm = o_gd + d
    table_diff = rel_bias[:, :DIFF_HEADS].T
    table_moba = rel_bias[:, DIFF_HEADS:].T
    bias_diff = _near_bias(table_diff, min(DIFF_TILE, seq))
    bias_moba = _near_bias(table_moba, MOBA_BLOCK)
    colscale = jnp.ones((1, w_in.shape[2]), F32)
    colscale = colscale.at[:, :qk_w].set(DIFF_HEAD_DIM ** -0.5 * LOG2E)
    colscale = colscale.at[:, o_qm:o_km].set(MOBA_HEAD_DIM ** -0.5 * LOG2E)
    c_pad = jnp.zeros((8, d), F32).at[:batch].set(c)

    assert depth == 1, "single-layer block: the final norm is fused into the last kernel"
    l = 0
    xc = x.reshape(t, d)
    mod3 = _ada(c_pad, w_ada[l], b_ada[l][None, :])[:batch].reshape(batch, 6, d)
    proj = _inproj(xc, mod3, g_mix[l][None, :], w_in[l].astype(BF16), colscale, seq, tm=min(1024, seq))
    lam_init = 0.8 - 0.6 * math.exp(-0.3 * l)
    lv = diff_lambda[l].astype(F32)
    lam = (jnp.exp(jnp.sum(lv[0] * lv[1])) - jnp.exp(jnp.sum(lv[2] * lv[3])) + lam_init).reshape(1)
    td = min(DIFF_TILE, seq)
    qd_t = proj[:, :qk_w].T
    vd_t = proj[:, 2 * qk_w:o_qm].reshape(t // td, td, qk_w).transpose(0, 2, 1)
    od = _diff_attention(proj, qd_t, vd_t, lam, bias_diff, diff_subln_g[l][:, None], batch, seq,
                         1.0 - lam_init, td)
    kmean = _kmean(proj, o_km // moba_w, moba_w).reshape(batch, seq // MOBA_BLOCK, moba_w)
    qm_t = proj[:, o_qm:o_km].T
    vm_t = proj[:, o_vm:o_gd].reshape(t // MOBA_BLOCK, MOBA_BLOCK, moba_w).transpose(0, 2, 1)
    om = _moba_attention(proj, qm_t, vm_t, kmean, bias_moba, batch, seq, o_km // MOBA_HEAD_DIM)
    x1, h2, wt = _mix(od, om, proj, xc, mod3, w_o_diff[l].astype(BF16), w_o_moba[l].astype(BF16),
                      w_out[l].astype(BF16), g_ffn[l][None, :], w_router[l].T,
                      router_bias[l][:, None], seq, o_gd // d, o_gm // d)
    w_tok, pos, row_tok, block_expert, n_active = _dispatch(wt, EXPERT_ROWS)
    x_pad = h2[row_tok]
    y_pad = _experts(block_expert, n_active, x_pad, w_exp_gate[l], w_exp_up[l], w_exp_down[l])
    routed = jnp.sum(y_pad[pos].astype(F32) * w_tok[:, :, None], axis=1)
    out = _final(x1, h2, routed, mod3, w_sh_gate[l].astype(BF16), w_sh_up[l].astype(BF16),
                 w_sh_down[l].astype(BF16), g_final[None, :], seq)
    return out.reshape(batch, seq, d)
```

```python
import functools
import math

import jax
import jax.numpy as jnp
from jax import lax
from jax.experimental import pallas as pl
from jax.experimental.pallas import tpu as pltpu

F32 = jnp.float32
BF16 = jnp.bfloat16

DIFF_HEADS = 8
DIFF_HEAD_DIM = 64
MOBA_HEADS = 8
MOBA_HEAD_DIM = 128
MOBA_BLOCK = 256
MOBA_TOPK = 3
REL_BUCKETS = 32
REL_MAX_DIST = 128
N_EXPERTS = 64
N_GROUPS = 8
TOPK_GROUPS = 4
TOPK_EXPERTS = 8
ROUTED_SCALE = 2.5
RMS_EPS = 1e-6
LANES = 128
NEG_BIG = -1e30
LOG2E = 1.4426950408889634
DIFF_TILE = 512
EXPERT_ROWS = 256
VMEM_LIMIT = 56 * 1024 * 1024


def _cparams(*sem):
    return pltpu.CompilerParams(dimension_semantics=sem, vmem_limit_bytes=VMEM_LIMIT)


def _dot(a, b):
    return jnp.dot(a, b, preferred_element_type=F32)


def _dot_nt(a, b):
    return lax.dot_general(a, b, (((1,), (1,)), ((), ())), preferred_element_type=F32)


def _ada_kernel(c_ref, w_ref, b_ref, o_ref):
    c = c_ref[...]
    ca = c * jax.nn.sigmoid(c)
    o_ref[...] = jnp.dot(ca, w_ref[...], preferred_element_type=F32,
                         precision=lax.Precision.HIGHEST) + b_ref[...]


def _ada(c_pad, w_ada, b_ada, tn=1024):
    rows, d = c_pad.shape
    n = w_ada.shape[1]
    return pl.pallas_call(
        _ada_kernel,
        grid=(n // tn,),
        in_specs=[pl.BlockSpec((rows, d), lambda j: (0, 0)),
                  pl.BlockSpec((d, tn), lambda j: (0, j)),
                  pl.BlockSpec((1, tn), lambda j: (0, j))],
        out_specs=pl.BlockSpec((rows, tn), lambda j: (0, j)),
        out_shape=jax.ShapeDtypeStruct((rows, n), F32),
        compiler_params=_cparams("arbitrary"),
        name="ada",
    )(c_pad, w_ada, b_ada)


def _modulated_norm(x, g, scale, shift):
    ms = jnp.mean(x * x, axis=-1, keepdims=True)
    return (x * lax.rsqrt(ms + RMS_EPS) * g) * (1.0 + scale) + shift


def _inproj_kernel(x_ref, mod_ref, g_ref, w_ref, cs_ref, o_ref, h_ref, *, chunk):
    @pl.when(pl.program_id(1) == 0)
    def _():
        shift = mod_ref[0, 0:1, :]
        scale = mod_ref[0, 1:2, :]
        g = g_ref[...]

        def body(r, carry):
            rows = pl.ds(pl.multiple_of(r * chunk, chunk), chunk)
            h_ref[rows, :] = _modulated_norm(x_ref[rows, :], g, scale, shift).astype(BF16)
            return carry

        lax.fori_loop(0, x_ref.shape[0] // chunk, body, 0)

    o_ref[...] = (_dot(h_ref[...], w_ref[...]) * cs_ref[...]).astype(BF16)


def _inproj(x2, mod3, g_mix, w_in_bf, colscale, seq, tm=1024, tn=1024):
    t, d = x2.shape
    n = w_in_bf.shape[1]
    per_batch = seq // tm
    return pl.pallas_call(
        functools.partial(_inproj_kernel, chunk=128),
        grid=(t // tm, n // tn),
        in_specs=[pl.BlockSpec((tm, d), lambda i, j: (i, 0)),
                  pl.BlockSpec((1, 6, d), lambda i, j: (i // per_batch, 0, 0)),
                  pl.BlockSpec((1, d), lambda i, j: (0, 0)),
                  pl.BlockSpec((d, tn), lambda i, j: (0, j)),
                  pl.BlockSpec((1, tn), lambda i, j: (0, j))],
        out_specs=pl.BlockSpec((tm, tn), lambda i, j: (i, j)),
        out_shape=jax.ShapeDtypeStruct((t, n), BF16),
        scratch_shapes=[pltpu.VMEM((tm, d), BF16)],
        compiler_params=_cparams("arbitrary", "arbitrary"),
        name="inproj",
    )(x2, mod3, g_mix, w_in_bf, colscale)


def _kmean_kernel(k_ref, o_ref):
    o_ref[0] = jnp.mean(k_ref[...].astype(F32), axis=0, keepdims=True)


def _kmean(proj, col_block, width):
    t = proj.shape[0]
    nblk = t // MOBA_BLOCK
    return pl.pallas_call(
        _kmean_kernel,
        grid=(nblk,),
        in_specs=[pl.BlockSpec((MOBA_BLOCK, width), lambda i: (i, col_block))],
        out_specs=pl.BlockSpec((1, 1, width), lambda i: (i, 0, 0)),
        out_shape=jax.ShapeDtypeStruct((nblk, 1, width), F32),
        compiler_params=_cparams("arbitrary"),
        name="kmean",
    )(proj)


def _softmax_first(s, vt, m_ref, l_ref, acc_ref):
    m = jnp.max(s, axis=0, keepdims=True)
    p = jnp.exp2(s - m)
    m_ref[...] = m
    l_ref[...] = jnp.sum(p, axis=0, keepdims=True)
    acc_ref[...] = _dot(vt, p.astype(BF16))


def _softmax_next(s, vt, m_ref, l_ref, acc_ref):
    m_old = m_ref[...]
    m = jnp.maximum(m_old, jnp.max(s, axis=0, keepdims=True))
    a = jnp.exp2(m_old - m)
    p = jnp.exp2(s - m)
    m_ref[...] = m
    l_ref[...] = a * l_ref[...] + jnp.sum(p, axis=0, keepdims=True)
    acc_ref[...] = a * acc_ref[...] + _dot(vt, p.astype(BF16))


def _diff_kernel(lam_ref, qt_ref, k_ref, vt_ref, bias_ref, g_ref, o_ref, m, l, acc, *, t, out_scale):
    qi = pl.program_id(2)
    qt = qt_ref[...]
    zeros = jnp.zeros((DIFF_HEAD_DIM, t), qt.dtype)
    w12 = jnp.concatenate([jnp.concatenate([qt[:DIFF_HEAD_DIM], zeros], axis=0),
                           jnp.concatenate([zeros, qt[DIFF_HEAD_DIM:]], axis=0)], axis=1)

    def logits(kj):
        rows = pl.ds(pl.multiple_of(kj * t, t), t)
        return _dot(k_ref[rows, :], w12)

    def both(b):
        return jnp.concatenate([b, b], axis=1)

    _softmax_first(logits(qi) + both(bias_ref[0, 0]), vt_ref[qi], m, l, acc)

    @pl.when(qi >= 1)
    def _():
        _softmax_next(logits(qi - 1) + both(bias_ref[0, 1]), vt_ref[qi - 1], m, l, acc)

    def far(kj, carry):
        _softmax_next(logits(kj), vt_ref[kj], m, l, acc)
        return carry

    lax.fori_loop(0, jnp.maximum(qi - 1, 0), far, 0)

    o12 = acc[...] / l[...]
    o = o12[:, :t] - lam_ref[0] * o12[:, t:]
    ms = jnp.mean(o * o, axis=0, keepdims=True)
    o = (o * lax.rsqrt(ms + RMS_EPS) * g_ref[...]) * out_scale
    o_ref[...] = o.T.astype(BF16)


def _diff_attention(proj, qt, vt, lam, bias, subln_g, batch, seq, out_scale, t):
    nq = seq // t
    hw = 2 * DIFF_HEAD_DIM
    kcol = DIFF_HEADS
    return pl.pallas_call(
        functools.partial(_diff_kernel, t=t, out_scale=out_scale),
        grid=(batch, DIFF_HEADS, nq),
        in_specs=[pl.BlockSpec(memory_space=pltpu.SMEM),
                  pl.BlockSpec((hw, t), lambda b, h, i: (h, b * nq + i)),
                  pl.BlockSpec((seq, hw), lambda b, h, i: (b, kcol + h)),
                  pl.BlockSpec((nq, hw, t), lambda b, h, i: (b, h, 0)),
                  pl.BlockSpec((1, 2, t, t), lambda b, h, i: (h, 0, 0, 0)),
                  pl.BlockSpec((hw, 1), lambda b, h, i: (0, 0))],
        out_specs=pl.BlockSpec((t, hw), lambda b, h, i: (b * nq + i, h)),
        out_shape=jax.ShapeDtypeStruct((batch * seq, DIFF_HEADS * hw), BF16),
        scratch_shapes=[pltpu.VMEM((1, 2 * t), F32), pltpu.VMEM((1, 2 * t), F32),
                        pltpu.VMEM((hw, 2 * t), F32)],
        compiler_params=_cparams("arbitrary", "arbitrary", "arbitrary"),
        name="diffattn",
    )(lam, qt, proj, vt, bias, subln_g)


def _moba_kernel(qt_ref, k_ref, vt_ref, km_ref, bias_ref, o_ref, m, l, acc, sel_ref, *, t):
    cur = pl.program_id(2)
    qt = qt_ref[...]
    nb = km_ref.shape[1]

    gate = jnp.dot(km_ref[0], qt.astype(F32), preferred_element_type=F32,
                   precision=lax.Precision.HIGHEST)
    blk = lax.broadcasted_iota(jnp.int32, gate.shape, 0)
    gate = jnp.where(blk < cur, gate, -jnp.inf)
    sel = jnp.zeros(gate.shape, F32)
    for _ in range(MOBA_TOPK):
        gmax = jnp.max(gate, axis=0, keepdims=True)
        first = jnp.min(jnp.where(gate == gmax, blk, nb), axis=0, keepdims=True)
        pick = (blk == first) & (gmax > -jnp.inf)
        sel = jnp.where(pick, 1.0, sel)
        gate = jnp.where(blk == first, -jnp.inf, gate)
    sel_ref[...] = sel

    def logits(j0, nblk):
        rows = pl.ds(pl.multiple_of(j0 * t, t), nblk * t)
        return _dot(k_ref[rows, :], qt)

    def keep(j, valid):
        return jnp.where(valid, sel_ref[pl.ds(j, 1), :], 0.0) > 0.5

    def pair_vt(j0):
        return jnp.concatenate([vt_ref[j0], vt_ref[j0 + 1]], axis=1)

    @pl.when(cur == 0)
    def _():
        _softmax_first(logits(0, 1) + bias_ref[0, 0], vt_ref[0], m, l, acc)

    @pl.when(cur >= 1)
    def _():
        s = logits(cur - 1, 2)
        prev = jnp.where(keep(cur - 1, True), s[:t] + bias_ref[0, 1], NEG_BIG)
        own = s[t:] + bias_ref[0, 0]
        _softmax_first(jnp.concatenate([prev, own], axis=0), pair_vt(cur - 1), m, l, acc)

    nfar = jnp.maximum(cur - 1, 0)

    def far(c, carry):
        j0 = 2 * c
        s = logits(j0, 2)
        s0 = jnp.where(keep(j0, True), s[:t], NEG_BIG)
        s1 = jnp.where(keep(j0 + 1, j0 + 1 < nfar), s[t:], NEG_BIG)
        _softmax_next(jnp.concatenate([s0, s1], axis=0), pair_vt(j0), m, l, acc)
        return carry

    lax.fori_loop(0, (nfar + 1) // 2, far, 0)
    o_ref[...] = (acc[...] / l[...]).T.astype(BF16)


def _moba_attention(proj, qt, vt, kmean, bias, batch, seq, kcol):
    t = MOBA_BLOCK
    nq = seq // t
    dh = MOBA_HEAD_DIM
    nb = kmean.shape[1]
    return pl.pallas_call(
        functools.partial(_moba_kernel, t=t),
        grid=(batch, MOBA_HEADS, nq),
        in_specs=[pl.BlockSpec((dh, t), lambda b, h, i: (h, b * nq + i)),
                  pl.BlockSpec((seq, dh), lambda b, h, i: (b, kcol + h)),
                  pl.BlockSpec((nq, dh, t), lambda b, h, i: (b, h, 0)),
                  pl.BlockSpec((1, nb, dh), lambda b, h, i: (b, 0, h)),
                  pl.BlockSpec((1, 2, t, t), lambda b, h, i: (h, 0, 0, 0))],
        out_specs=pl.BlockSpec((t, dh), lambda b, h, i: (b * nq + i, h)),
        out_shape=jax.ShapeDtypeStruct((batch * seq, MOBA_HEADS * dh), BF16),
        scratch_shapes=[pltpu.VMEM((1, t), F32), pltpu.VMEM((1, t), F32), pltpu.VMEM((dh, t), F32),
                        pltpu.VMEM((nb, t), F32)],
        compiler_params=_cparams("arbitrary", "arbitrary", "arbitrary"),
        name="moba",
    )(qt, proj, vt, kmean, bias)


def _first_index_of_max(vals, ids, sentinel):
    vmax = jnp.max(vals, axis=0, keepdims=True)
    first = jnp.min(jnp.where(vals == vmax, ids, sentinel), axis=0, keepdims=True)
    return vmax, first


def _route(scores_t, bias_t):
    e, n = scores_t.shape
    per = e // N_GROUPS
    selv = scores_t + bias_t
    sub = lax.broadcasted_iota(jnp.int32, (per, n), 0)
    gscore = []
    for g in range(N_GROUPS):
        blk = selv[g * per:(g + 1) * per, :]
        top1, first = _first_index_of_max(blk, sub, per)
        top2 = jnp.max(jnp.where(sub == first, -jnp.inf, blk), axis=0, keepdims=True)
        gscore.append(top1 + top2)
    gs = jnp.concatenate(gscore, axis=0)
    gid = lax.broadcasted_iota(jnp.int32, gs.shape, 0)
    gsel = jnp.zeros(gs.shape, F32)
    for _ in range(TOPK_GROUPS):
        _, first = _first_index_of_max(gs, gid, N_GROUPS)
        gsel = jnp.where(gid == first, 1.0, gsel)
        gs = jnp.where(gid == first, -jnp.inf, gs)
    masked = jnp.concatenate(
        [jnp.where(gsel[g:g + 1, :] > 0.5, selv[g * per:(g + 1) * per, :], -jnp.inf)
         for g in range(N_GROUPS)], axis=0)
    eid = lax.broadcasted_iota(jnp.int32, masked.shape, 0)
    picked = jnp.zeros(masked.shape, F32)
    for _ in range(TOPK_EXPERTS):
        _, first = _first_index_of_max(masked, eid, e)
        picked = jnp.where(eid == first, 1.0, picked)
        masked = jnp.where(eid == first, -jnp.inf, masked)
    w = jnp.where(picked > 0.5, scores_t, 0.0)
    return w / jnp.sum(w, axis=0, keepdims=True) * ROUTED_SCALE


def _mix_kernel(od_ref, om_ref, gd_ref, gm_ref, x_ref, mod_ref, wod_ref, wom_ref, wout_ref,
                gffn_ref, wr_ref, rb_ref, x1_ref, h2_ref, wt_ref):
    yd = _dot(od_ref[...], wod_ref[...])
    ym = _dot(om_ref[...], wom_ref[...])
    z = jax.nn.sigmoid(gd_ref[...].astype(F32)) * yd + jax.nn.sigmoid(gm_ref[...].astype(F32)) * ym
    mixed = _dot(z.astype(BF16), wout_ref[...])
    x1 = x_ref[...] + mod_ref[0, 2:3, :] * mixed
    x1_ref[...] = x1
    h2 = _modulated_norm(x1, gffn_ref[...], mod_ref[0, 4:5, :], mod_ref[0, 3:4, :])
    h2_ref[...] = h2.astype(BF16)
    logits_t = lax.dot_general(wr_ref[...], h2, (((1,), (1,)), ((), ())),
                               preferred_element_type=F32, precision=lax.Precision.HIGHEST)
    wt_ref[...] = _route(jax.nn.sigmoid(logits_t), rb_ref[...])


def _mix(od, om, proj, x2, mod3, wod, wom, wout, g_ffn, wr_t, rb_t, seq, gd_col, gm_col, tm=256):
    t, d = x2.shape
    per_batch = seq // tm
    const = lambda i: (0, 0)
    return pl.pallas_call(
        _mix_kernel,
        grid=(t // tm,),
        in_specs=[pl.BlockSpec((tm, od.shape[1]), lambda i: (i, 0)),
                  pl.BlockSpec((tm, om.shape[1]), lambda i: (i, 0)),
                  pl.BlockSpec((tm, d), lambda i: (i, gd_col)),
                  pl.BlockSpec((tm, d), lambda i: (i, gm_col)),
                  pl.BlockSpec((tm, d), lambda i: (i, 0)),
                  pl.BlockSpec((1, 6, d), lambda i: (i // per_batch, 0, 0)),
                  pl.BlockSpec(wod.shape, const),
                  pl.BlockSpec(wom.shape, const),
                  pl.BlockSpec(wout.shape, const),
                  pl.BlockSpec((1, d), const),
                  pl.BlockSpec(wr_t.shape, const),
                  pl.BlockSpec(rb_t.shape, const)],
        out_specs=[pl.BlockSpec((tm, d), lambda i: (i, 0)),
                   pl.BlockSpec((tm, d), lambda i: (i, 0)),
                   pl.BlockSpec((N_EXPERTS, tm), lambda i: (0, i))],
        out_shape=[jax.ShapeDtypeStruct((t, d), F32),
                   jax.ShapeDtypeStruct((t, d), BF16),
                   jax.ShapeDtypeStruct((N_EXPERTS, t), F32)],
        compiler_params=_cparams("arbitrary"),
        name="mix",
    )(od, om, proj, proj, x2, mod3, wod, wom, wout, g_ffn, wr_t, rb_t)


def _expert_kernel(be_ref, na_ref, x_ref, wg_ref, wu_ref, wd_ref, o_ref, wg_s, wu_s, wd_s):
    i = pl.program_id(0)

    @pl.when(i < na_ref[0])
    def _():
        prev = be_ref[jnp.maximum(i - 1, 0)]

        @pl.when((i == 0) | (be_ref[i] != prev))
        def _():
            wg_s[...] = wg_ref[0].astype(BF16)
            wu_s[...] = wu_ref[0].astype(BF16)
            wd_s[...] = wd_ref[0].astype(BF16)

        x = x_ref[...]
        g = _dot(x, wg_s[...])
        u = _dot(x, wu_s[...])
        a = (g * jax.nn.sigmoid(g) * u).astype(BF16)
        o_ref[...] = _dot(a, wd_s[...]).astype(BF16)


def _experts(block_expert, n_active, x_pad, w_gate, w_up, w_down, rows=EXPERT_ROWS):
    p, d = x_pad.shape
    f = w_gate.shape[2]
    n_blocks = p // rows
    def last_active(i, na):
        return jnp.minimum(i, jnp.maximum(na[0] - 1, 0))

    def xmap(i, be, na):
        return (last_active(i, na), 0)

    def wmap(i, be, na):
        return (be[last_active(i, na)], 0, 0)

    grid_spec = pltpu.PrefetchScalarGridSpec(
        num_scalar_prefetch=2,
        grid=(n_blocks,),
        in_specs=[pl.BlockSpec((rows, d), xmap),
                  pl.BlockSpec((1, d, f), wmap),
                  pl.BlockSpec((1, d, f), wmap),
                  pl.BlockSpec((1, f, d), wmap)],
        out_specs=pl.BlockSpec((rows, d), xmap),
        scratch_shapes=[pltpu.VMEM((d, f), BF16), pltpu.VMEM((d, f), BF16), pltpu.VMEM((f, d), BF16)],
    )
    return pl.pallas_call(
        _expert_kernel,
        grid_spec=grid_spec,
        out_shape=jax.ShapeDtypeStruct((p, d), BF16),
        compiler_params=_cparams("arbitrary"),
        name="experts",
    )(block_expert, n_active, x_pad, w_gate, w_up, w_down)


def _final_kernel(x1_ref, h2_ref, r_ref, mod_ref, wsg_ref, wsu_ref, wsd_ref, gf_ref, o_ref):
    h2 = h2_ref[...]
    g = _dot(h2, wsg_ref[...])
    u = _dot(h2, wsu_ref[...])
    shared = _dot((g * jax.nn.sigmoid(g) * u).astype(BF16), wsd_ref[...])
    x2 = x1_ref[...] + mod_ref[0, 5:6, :] * (r_ref[...] + shared)
    ms = jnp.mean(x2 * x2, axis=-1, keepdims=True)
    o_ref[...] = x2 * lax.rsqrt(ms + RMS_EPS) * gf_ref[...]


def _final(x1, h2, routed, mod3, wsg, wsu, wsd, g_final, seq, tm=256):
    t, d = x1.shape
    per_batch = seq // tm
    const = lambda i: (0, 0)
    row = pl.BlockSpec((tm, d), lambda i: (i, 0))
    return pl.pallas_call(
        _final_kernel,
        grid=(t // tm,),
        in_specs=[row, row, row,
                  pl.BlockSpec((1, 6, d), lambda i: (i // per_batch, 0, 0)),
                  pl.BlockSpec(wsg.shape, const),
                  pl.BlockSpec(wsu.shape, const),
                  pl.BlockSpec(wsd.shape, const),
                  pl.BlockSpec((1, d), const)],
        out_specs=row,
        out_shape=jax.ShapeDtypeStruct((t, d), F32),
        compiler_params=_cparams("arbitrary"),
        name="final",
    )(x1, h2, routed, mod3, wsg, wsu, wsd, g_final)


def _bucket(rel):
    n = jnp.maximum(rel, 0)
    max_exact = REL_BUCKETS // 2
    nf = jnp.maximum(n, 1).astype(F32)
    large = max_exact + (jnp.log(nf / max_exact) / math.log(REL_MAX_DIST / max_exact)
                         * (REL_BUCKETS - max_exact)).astype(jnp.int32)
    large = jnp.minimum(large, REL_BUCKETS - 1)
    return jnp.where(n < max_exact, n, large)


def _bias_kernel(tab_ref, o_ref, *, t):
    h = pl.program_id(0)
    sub = pl.program_id(1)
    key = lax.broadcasted_iota(jnp.int32, (t, t), 0)
    qry = lax.broadcasted_iota(jnp.int32, (t, t), 1)
    rel = qry - key + sub * t
    bucket = _bucket(rel)
    far = tab_ref[h, REL_BUCKETS - 1]
    bias = jnp.zeros((t, t), F32)
    for b in range(REL_BUCKETS - 1):
        bias = jnp.where(bucket == b, tab_ref[h, b] - far, bias)
    o_ref[0, 0] = jnp.where(rel >= 0, bias * LOG2E, NEG_BIG)


def _near_bias(table, t):
    assert t >= REL_MAX_DIST
    heads = table.shape[0]
    return pl.pallas_call(
        functools.partial(_bias_kernel, t=t),
        grid=(heads, 2),
        in_specs=[pl.BlockSpec(memory_space=pltpu.SMEM)],
        out_specs=pl.BlockSpec((1, 1, t, t), lambda h, s: (h, s, 0, 0)),
        out_shape=jax.ShapeDtypeStruct((heads, 2, t, t), F32),
        compiler_params=_cparams("arbitrary", "arbitrary"),
        name="relbias",
    )(table)


def _dispatch(wt, rows):
    e, t = wt.shape
    k = TOPK_EXPERTS
    a = t * k
    w_tok, eidx = lax.top_k(wt.T, k)
    sel = (wt > 0).astype(jnp.int32)
    counts = jnp.sum(sel, axis=1)
    rank = jnp.cumsum(sel, axis=1) - sel
    padded = ((counts + rows - 1) // rows) * rows
    pad_ends = jnp.cumsum(padded)
    pad_starts = pad_ends - padded
    starts = jnp.cumsum(counts) - counts
    pos = pad_starts[eidx] + jnp.take_along_axis(rank.T, eidx, axis=1)
    n_blocks = a // rows + e
    n_active = (pad_ends[-1] // rows).astype(jnp.int32)
    end_blocks = pad_ends // rows
    block_expert = jnp.sum(jnp.arange(n_blocks)[:, None] >= end_blocks[None, :], axis=1)
    block_expert = jnp.minimum(block_expert, e - 1).astype(jnp.int32)
    order = jnp.argsort(eidx.reshape(a))
    tok_sorted = (order // k).astype(jnp.int32)
    p = jnp.arange(n_blocks * rows)
    ep = jnp.repeat(block_expert, rows)
    j = p - pad_starts[ep]
    row_tok = jnp.where(j < counts[ep], tok_sorted[jnp.clip(starts[ep] + j, 0, a - 1)], p % t)
    return w_tok, pos, row_tok, block_expert, n_active.reshape(1)


def kernel(x, c, w_ada, b_ada, g_mix, g_ffn, w_in, diff_lambda, diff_subln_g, rel_bias, w_o_diff, w_o_moba,
           w_out, w_router, router_bias, w_exp_gate, w_exp_up, w_exp_down, w_sh_gate, w_sh_up, w_sh_down,
           g_final):
    batch, seq, d = x.shape
    t = batch * seq
    depth = w_ada.shape[0]
    assert seq % MOBA_BLOCK == 0 and seq % min(DIFF_TILE, seq) == 0
    hw = 2 * DIFF_HEAD_DIM
    qk_w = DIFF_HEADS * hw
    moba_w = MOBA_HEADS * MOBA_HEAD_DIM
    o_qm = 3 * qk_w
    o_km = o_qm + moba_w
    o_vm = o_km + moba_w
    o_gd = o_vm + moba_w
    o_gm = o_gd + d
    table_diff = rel_bias[:, :DIFF_HEADS].T
    table_moba = rel_bias[:, DIFF_HEADS:].T
    bias_diff = _near_bias(table_diff, min(DIFF_TILE, seq))
    bias_moba = _near_bias(table_moba, MOBA_BLOCK)
    colscale = jnp.ones((1, w_in.shape[2]), F32)
    colscale = colscale.at[:, :qk_w].set(DIFF_HEAD_DIM ** -0.5 * LOG2E)
    colscale = colscale.at[:, o_qm:o_km].set(MOBA_HEAD_DIM ** -0.5 * LOG2E)
    c_pad = jnp.zeros((8, d), F32).at[:batch].set(c)

    assert depth == 1, "single-layer block: the final norm is fused into the last kernel"
    l = 0
    xc = x.reshape(t, d)
    mod3 = _ada(c_pad, w_ada[l], b_ada[l][None, :])[:batch].reshape(batch, 6, d)
    proj = _inproj(xc, mod3, g_mix[l][None, :], w_in[l].astype(BF16), colscale, seq, tm=min(1024, seq))
    lam_init = 0.8 - 0.6 * math.exp(-0.3 * l)
    lv = diff_lambda[l].astype(F32)
    lam = (jnp.exp(jnp.sum(lv[0] * lv[1])) - jnp.exp(jnp.sum(lv[2] * lv[3])) + lam_init).reshape(1)
    td = min(DIFF_TILE, seq)
    qd_t = proj[:, :qk_w].T
    vd_t = proj[:, 2 * qk_w:o_qm].reshape(t // td, td, qk_w).transpose(0, 2, 1)
    od = _diff_attention(proj, qd_t, vd_t, lam, bias_diff, diff_subln_g[l][:, None], batch, seq,
                         1.0 - lam_init, td)
    kmean = _kmean(proj, o_km // moba_w, moba_w).reshape(batch, seq // MOBA_BLOCK, moba_w)
    qm_t = proj[:, o_qm:o_km].T
    vm_t = proj[:, o_vm:o_gd].reshape(t // MOBA_BLOCK, MOBA_BLOCK, moba_w).transpose(0, 2, 1)
    om = _moba_attention(proj, qm_t, vm_t, kmean, bias_moba, batch, seq, o_km // MOBA_HEAD_DIM)
    x1, h2, wt = _mix(od, om, proj, xc, mod3, w_o_diff[l].astype(BF16), w_o_moba[l].astype(BF16),
                      w_out[l].astype(BF16), g_ffn[l][None, :], w_router[l].T,
                      router_bias[l][:, None], seq, o_gd // d, o_gm // d)
    w_tok, pos, row_tok, block_expert, n_active = _dispatch(wt, EXPERT_ROWS)
    x_pad = h2[row_tok]
    y_pad = _experts(block_expert, n_active, x_pad, w_exp_gate[l], w_exp_up[l], w_exp_down[l])
    routed = jnp.sum(y_pad[pos].astype(F32) * w_tok[:, :, None], axis=1)
    out = _final(x1, h2, routed, mod3, w_sh_gate[l].astype(BF16), w_sh_up[l].astype(BF16),
                 w_sh_down[l].astype(BF16), g_final[None, :], seq)
    return out.reshape(batch, seq, d)
```

```python
import functools
import math

import jax
import jax.numpy as jnp
from jax import lax
from jax.experimental import pallas as pl
from jax.experimental.pallas import tpu as pltpu

F32 = jnp.float32
BF16 = jnp.bfloat16

DIFF_HEADS = 8
DIFF_HEAD_DIM = 64
MOBA_HEADS = 8
MOBA_HEAD_DIM = 128
MOBA_BLOCK = 256
MOBA_TOPK = 3
REL_BUCKETS = 32
REL_MAX_DIST = 128
N_EXPERTS = 64
N_GROUPS = 8
TOPK_GROUPS = 4
TOPK_EXPERTS = 8
ROUTED_SCALE = 2.5
RMS_EPS = 1e-6
LANES = 128
NEG_BIG = -1e30
LOG2E = 1.4426950408889634
DIFF_TILE = 512
DIFF_HEADS_PER_STEP = 4
MOBA_HEADS_PER_STEP = 4
EXPERT_ROWS = 256
VMEM_LIMIT = 56 * 1024 * 1024


def _cparams(*sem):
    return pltpu.CompilerParams(dimension_semantics=sem, vmem_limit_bytes=VMEM_LIMIT)


def _dot(a, b):
    return jnp.dot(a, b, preferred_element_type=F32)


def _dot_nt(a, b):
    return lax.dot_general(a, b, (((1,), (1,)), ((), ())), preferred_element_type=F32)


def _ada_kernel(c_ref, w_ref, b_ref, o_ref):
    c = c_ref[...]
    ca = c * jax.nn.sigmoid(c)
    o_ref[...] = jnp.dot(ca, w_ref[...], preferred_element_type=F32,
                         precision=lax.Precision.HIGHEST) + b_ref[...]


def _ada(c_pad, w_ada, b_ada, tn=1024):
    rows, d = c_pad.shape
    n = w_ada.shape[1]
    return pl.pallas_call(
        _ada_kernel,
        grid=(n // tn,),
        in_specs=[pl.BlockSpec((rows, d), lambda j: (0, 0)),
                  pl.BlockSpec((d, tn), lambda j: (0, j)),
                  pl.BlockSpec((1, tn), lambda j: (0, j))],
        out_specs=pl.BlockSpec((rows, tn), lambda j: (0, j)),
        out_shape=jax.ShapeDtypeStruct((rows, n), F32),
        compiler_params=_cparams("arbitrary"),
        name="ada",
    )(c_pad, w_ada, b_ada)


def _modulated_norm(x, g, scale, shift):
    ms = jnp.mean(x * x, axis=-1, keepdims=True)
    return (x * lax.rsqrt(ms + RMS_EPS) * g) * (1.0 + scale) + shift


def _inproj_kernel(x_ref, mod_ref, g_ref, w_ref, cs_ref, o_ref, h_ref, *, chunk):
    @pl.when(pl.program_id(1) == 0)
    def _():
        shift = mod_ref[0, 0:1, :]
        scale = mod_ref[0, 1:2, :]
        g = g_ref[...]

        def body(r, carry):
            rows = pl.ds(pl.multiple_of(r * chunk, chunk), chunk)
            h_ref[rows, :] = _modulated_norm(x_ref[rows, :], g, scale, shift).astype(BF16)
            return carry

        lax.fori_loop(0, x_ref.shape[0] // chunk, body, 0)

    o_ref[...] = (_dot(h_ref[...], w_ref[...]) * cs_ref[...]).astype(BF16)


def _inproj(x2, mod3, g_mix, w_in_bf, colscale, seq, tm=1024, tn=1024):
    t, d = x2.shape
    n = w_in_bf.shape[1]
    per_batch = seq // tm
    return pl.pallas_call(
        functools.partial(_inproj_kernel, chunk=128),
        grid=(t // tm, n // tn),
        in_specs=[pl.BlockSpec((tm, d), lambda i, j: (i, 0)),
                  pl.BlockSpec((1, 6, d), lambda i, j: (i // per_batch, 0, 0)),
                  pl.BlockSpec((1, d), lambda i, j: (0, 0)),
                  pl.BlockSpec((d, tn), lambda i, j: (0, j)),
                  pl.BlockSpec((1, tn), lambda i, j: (0, j))],
        out_specs=pl.BlockSpec((tm, tn), lambda i, j: (i, j)),
        out_shape=jax.ShapeDtypeStruct((t, n), BF16),
        scratch_shapes=[pltpu.VMEM((tm, d), BF16)],
        compiler_params=_cparams("arbitrary", "arbitrary"),
        name="inproj",
    )(x2, mod3, g_mix, w_in_bf, colscale)


def _kmean_kernel(k_ref, o_ref):
    o_ref[0] = jnp.mean(k_ref[...].astype(F32), axis=0, keepdims=True)


def _kmean(proj, col_block, width):
    t = proj.shape[0]
    nblk = t // MOBA_BLOCK
    return pl.pallas_call(
        _kmean_kernel,
        grid=(nblk,),
        in_specs=[pl.BlockSpec((MOBA_BLOCK, width), lambda i: (i, col_block))],
        out_specs=pl.BlockSpec((1, 1, width), lambda i: (i, 0, 0)),
        out_shape=jax.ShapeDtypeStruct((nblk, 1, width), F32),
        compiler_params=_cparams("arbitrary"),
        name="kmean",
    )(proj)


def _online_softmax(logits_fn, vt_fn, m, l, acc, heads, first):
    s_all = [logits_fn(g) for g in range(heads)]
    p_all, a_all = [], []
    for g, s in enumerate(s_all):
        smax = jnp.max(s, axis=0, keepdims=True)
        if first:
            mg = smax
            p = jnp.exp2(s - mg)
            l[g] = jnp.sum(p, axis=0, keepdims=True)
        else:
            m_old = m[g]
            mg = jnp.maximum(m_old, smax)
            a = jnp.exp2(m_old - mg)
            p = jnp.exp2(s - mg)
            l[g] = a * l[g] + jnp.sum(p, axis=0, keepdims=True)
            a_all.append(a)
        m[g] = mg
        p_all.append(p.astype(BF16))
    for g in range(heads):
        pv = _dot(vt_fn(g), p_all[g])
        acc[g] = pv if first else a_all[g] * acc[g] + pv


def _diff_kernel(lam_ref, qt_ref, k_ref, vt_ref, bias_ref, g_ref, o_ref, w12, m, l, acc,
                 *, t, heads, out_scale):
    qi = pl.program_id(2)
    hw = 2 * DIFF_HEAD_DIM
    zeros = jnp.zeros((DIFF_HEAD_DIM, t), BF16)
    for g in range(heads):
        w12[g, :, :t] = jnp.concatenate([qt_ref[g * hw:g * hw + DIFF_HEAD_DIM, :], zeros], axis=0)
        w12[g, :, t:] = jnp.concatenate([zeros, qt_ref[g * hw + DIFF_HEAD_DIM:(g + 1) * hw, :]], axis=0)

    def logits(g, kj):
        rows = pl.ds(pl.multiple_of(kj * t, t), t)
        return _dot(k_ref[rows, g * hw:(g + 1) * hw], w12[g])

    def vt(g, kj):
        return vt_ref[kj, g * hw:(g + 1) * hw, :]

    def both(b):
        return jnp.concatenate([b, b], axis=1)

    def chunk(kj, bias_idx, first):
        def biased(g):
            s = logits(g, kj)
            return s if bias_idx is None else s + both(bias_ref[g, bias_idx])
        _online_softmax(biased, lambda g: vt(g, kj), m, l, acc, heads, first)

    chunk(qi, 0, True)

    @pl.when(qi >= 1)
    def _():
        chunk(qi - 1, 1, False)

    def far(kj, carry):
        chunk(kj, None, False)
        return carry

    lax.fori_loop(0, jnp.maximum(qi - 1, 0), far, 0)

    for g in range(heads):
        o12 = acc[g] / l[g]
        o = o12[:, :t] - lam_ref[0] * o12[:, t:]
        ms = jnp.mean(o * o, axis=0, keepdims=True)
        o = (o * lax.rsqrt(ms + RMS_EPS) * g_ref[...]) * out_scale
        o_ref[:, g * hw:(g + 1) * hw] = o.T.astype(BF16)


def _diff_attention(proj, qt, vt, lam, bias, subln_g, batch, seq, out_scale, t, heads=DIFF_HEADS_PER_STEP):
    nq = seq // t
    hw = 2 * DIFF_HEAD_DIM
    gw = heads * hw
    ng = DIFF_HEADS // heads
    kcol = DIFF_HEADS // heads
    once = pl.Buffered(1)
    return pl.pallas_call(
        functools.partial(_diff_kernel, t=t, heads=heads, out_scale=out_scale),
        grid=(batch, ng, nq),
        in_specs=[pl.BlockSpec(memory_space=pltpu.SMEM),
                  pl.BlockSpec((gw, t), lambda b, h, i: (h, b * nq + i)),
                  pl.BlockSpec((seq, gw), lambda b, h, i: (b, kcol + h), pipeline_mode=once),
                  pl.BlockSpec((nq, gw, t), lambda b, h, i: (b, h, 0), pipeline_mode=once),
                  pl.BlockSpec((heads, 2, t, t), lambda b, h, i: (h, 0, 0, 0), pipeline_mode=once),
                  pl.BlockSpec((hw, 1), lambda b, h, i: (0, 0))],
        out_specs=pl.BlockSpec((t, gw), lambda b, h, i: (b * nq + i, h)),
        out_shape=jax.ShapeDtypeStruct((batch * seq, DIFF_HEADS * hw), BF16),
        scratch_shapes=[pltpu.VMEM((heads, hw, 2 * t), BF16),
                        pltpu.VMEM((heads, 1, 2 * t), F32), pltpu.VMEM((heads, 1, 2 * t), F32),
                        pltpu.VMEM((heads, hw, 2 * t), F32)],
        compiler_params=_cparams("arbitrary", "arbitrary", "arbitrary"),
        name="diffattn",
    )(lam, qt, proj, vt, bias, subln_g)


def _moba_kernel(qt_ref, k_ref, vt_ref, km_ref, bias_ref, o_ref, m, l, acc, sel_ref, *, t, heads):
    cur = pl.program_id(2)
    dh = MOBA_HEAD_DIM
    nb = km_ref.shape[1]

    def cols(g):
        return slice(g * dh, (g + 1) * dh)

    for g in range(heads):
        gate = jnp.dot(km_ref[0, :, cols(g)], qt_ref[cols(g), :].astype(F32), preferred_element_type=F32,
                       precision=lax.Precision.HIGHEST)
        blk = lax.broadcasted_iota(jnp.int32, gate.shape, 0)
        gate = jnp.where(blk < cur, gate, -jnp.inf)
        sel = jnp.zeros(gate.shape, F32)
        for _ in range(MOBA_TOPK):
            gmax = jnp.max(gate, axis=0, keepdims=True)
            first = jnp.min(jnp.where(gate == gmax, blk, nb), axis=0, keepdims=True)
            pick = (blk == first) & (gmax > -jnp.inf)
            sel = jnp.where(pick, 1.0, sel)
            gate = jnp.where(blk == first, -jnp.inf, gate)
        sel_ref[g] = sel

    def logits(g, j0, nblk):
        rows = pl.ds(pl.multiple_of(j0 * t, t), nblk * t)
        return _dot(k_ref[rows, cols(g)], qt_ref[cols(g), :])

    def keep(g, j, valid):
        return jnp.where(valid, sel_ref[g, pl.ds(j, 1), :], 0.0) > 0.5

    def pair_vt(g, j0):
        return jnp.concatenate([vt_ref[j0, cols(g), :], vt_ref[j0 + 1, cols(g), :]], axis=1)

    @pl.when(cur == 0)
    def _():
        _online_softmax(lambda g: logits(g, 0, 1) + bias_ref[g, 0], lambda g: vt_ref[0, cols(g), :],
                        m, l, acc, heads, True)

    @pl.when(cur >= 1)
    def _():
        def near(g):
            s = logits(g, cur - 1, 2)
            prev = jnp.where(keep(g, cur - 1, True), s[:t] + bias_ref[g, 1], NEG_BIG)
            return jnp.concatenate([prev, s[t:] + bias_ref[g, 0]], axis=0)
        _online_softmax(near, lambda g: pair_vt(g, cur - 1), m, l, acc, heads, True)

    nfar = jnp.maximum(cur - 1, 0)

    def far(c, carry):
        j0 = 2 * c

        def masked(g):
            s = logits(g, j0, 2)
            s0 = jnp.where(keep(g, j0, True), s[:t], NEG_BIG)
            s1 = jnp.where(keep(g, j0 + 1, j0 + 1 < nfar), s[t:], NEG_BIG)
            return jnp.concatenate([s0, s1], axis=0)
        _online_softmax(masked, lambda g: pair_vt(g, j0), m, l, acc, heads, False)
        return carry

    lax.fori_loop(0, (nfar + 1) // 2, far, 0)
    for g in range(heads):
        o_ref[:, cols(g)] = (acc[g] / l[g]).T.astype(BF16)


def _moba_attention(proj, qt, vt, kmean, bias, batch, seq, kcol_w, heads=MOBA_HEADS_PER_STEP):
    t = MOBA_BLOCK
    nq = seq // t
    dh = MOBA_HEAD_DIM
    gw = heads * dh
    ng = MOBA_HEADS // heads
    kcol = kcol_w // gw
    nb = kmean.shape[1]
    once = pl.Buffered(1)
    return pl.pallas_call(
        functools.partial(_moba_kernel, t=t, heads=heads),
        grid=(batch, ng, nq),
        in_specs=[pl.BlockSpec((gw, t), lambda b, h, i: (h, b * nq + i)),
                  pl.BlockSpec((seq, gw), lambda b, h, i: (b, kcol + h), pipeline_mode=once),
                  pl.BlockSpec((nq, gw, t), lambda b, h, i: (b, h, 0), pipeline_mode=once),
                  pl.BlockSpec((1, nb, gw), lambda b, h, i: (b, 0, h)),
                  pl.BlockSpec((heads, 2, t, t), lambda b, h, i: (h, 0, 0, 0), pipeline_mode=once)],
        out_specs=pl.BlockSpec((t, gw), lambda b, h, i: (b * nq + i, h)),
        out_shape=jax.ShapeDtypeStruct((batch * seq, MOBA_HEADS * dh), BF16),
        scratch_shapes=[pltpu.VMEM((heads, 1, t), F32), pltpu.VMEM((heads, 1, t), F32),
                        pltpu.VMEM((heads, dh, t), F32), pltpu.VMEM((heads, nb, t), F32)],
        compiler_params=_cparams("arbitrary", "arbitrary", "arbitrary"),
        name="moba",
    )(qt, proj, vt, kmean, bias)


def _first_index_of_max(vals, ids, sentinel):
    vmax = jnp.max(vals, axis=0, keepdims=True)
    first = jnp.min(jnp.where(vals == vmax, ids, sentinel), axis=0, keepdims=True)
    return vmax, first


def _route(scores_t, bias_t):
    e, n = scores_t.shape
    per = e // N_GROUPS
    selv = scores_t + bias_t
    sub = lax.broadcasted_iota(jnp.int32, (per, n), 0)
    gscore = []
    for g in range(N_GROUPS):
        blk = selv[g * per:(g + 1) * per, :]
        top1, first = _first_index_of_max(blk, sub, per)
        top2 = jnp.max(jnp.where(sub == first, -jnp.inf, blk), axis=0, keepdims=True)
        gscore.append(top1 + top2)
    gs = jnp.concatenate(gscore, axis=0)
    gid = lax.broadcasted_iota(jnp.int32, gs.shape, 0)
    gsel = jnp.zeros(gs.shape, F32)
    for _ in range(TOPK_GROUPS):
        _, first = _first_index_of_max(gs, gid, N_GROUPS)
        gsel = jnp.where(gid == first, 1.0, gsel)
        gs = jnp.where(gid == first, -jnp.inf, gs)
    masked = jnp.concatenate(
        [jnp.where(gsel[g:g + 1, :] > 0.5, selv[g * per:(g + 1) * per, :], -jnp.inf)
         for g in range(N_GROUPS)], axis=0)
    eid = lax.broadcasted_iota(jnp.int32, masked.shape, 0)
    picked = jnp.zeros(masked.shape, F32)
    for _ in range(TOPK_EXPERTS):
        _, first = _first_index_of_max(masked, eid, e)
        picked = jnp.where(eid == first, 1.0, picked)
        masked = jnp.where(eid == first, -jnp.inf, masked)
    w = jnp.where(picked > 0.5, scores_t, 0.0)
    return w / jnp.sum(w, axis=0, keepdims=True) * ROUTED_SCALE


def _mix_kernel(od_ref, om_ref, gd_ref, gm_ref, x_ref, mod_ref, wod_ref, wom_ref, wout_ref,
                gffn_ref, wr_ref, rb_ref, x1_ref, h2_ref, wt_ref):
    yd = _dot(od_ref[...], wod_ref[...])
    ym = _dot(om_ref[...], wom_ref[...])
    z = jax.nn.sigmoid(gd_ref[...].astype(F32)) * yd + jax.nn.sigmoid(gm_ref[...].astype(F32)) * ym
    mixed = _dot(z.astype(BF16), wout_ref[...])
    x1 = x_ref[...] + mod_ref[0, 2:3, :] * mixed
    x1_ref[...] = x1
    h2 = _modulated_norm(x1, gffn_ref[...], mod_ref[0, 4:5, :], mod_ref[0, 3:4, :])
    h2_ref[...] = h2.astype(BF16)
    logits_t = lax.dot_general(wr_ref[...], h2, (((1,), (1,)), ((), ())),
                               preferred_element_type=F32, precision=lax.Precision.HIGHEST)
    wt_ref[...] = _route(jax.nn.sigmoid(logits_t), rb_ref[...])


def _mix(od, om, proj, x2, mod3, wod, wom, wout, g_ffn, wr_t, rb_t, seq, gd_col, gm_col, tm=256):
    t, d = x2.shape
    per_batch = seq // tm
    const = lambda i: (0, 0)
    return pl.pallas_call(
        _mix_kernel,
        grid=(t // tm,),
        in_specs=[pl.BlockSpec((tm, od.shape[1]), lambda i: (i, 0)),
                  pl.BlockSpec((tm, om.shape[1]), lambda i: (i, 0)),
                  pl.BlockSpec((tm, d), lambda i: (i, gd_col)),
                  pl.BlockSpec((tm, d), lambda i: (i, gm_col)),
                  pl.BlockSpec((tm, d), lambda i: (i, 0)),
                  pl.BlockSpec((1, 6, d), lambda i: (i // per_batch, 0, 0)),
                  pl.BlockSpec(wod.shape, const),
                  pl.BlockSpec(wom.shape, const),
                  pl.BlockSpec(wout.shape, const),
                  pl.BlockSpec((1, d), const),
                  pl.BlockSpec(wr_t.shape, const),
                  pl.BlockSpec(rb_t.shape, const)],
        out_specs=[pl.BlockSpec((tm, d), lambda i: (i, 0)),
                   pl.BlockSpec((tm, d), lambda i: (i, 0)),
                   pl.BlockSpec((N_EXPERTS, tm), lambda i: (0, i))],
        out_shape=[jax.ShapeDtypeStruct((t, d), F32),
                   jax.ShapeDtypeStruct((t, d), BF16),
                   jax.ShapeDtypeStruct((N_EXPERTS, t), F32)],
        compiler_params=_cparams("arbitrary"),
        name="mix",
    )(od, om, proj, proj, x2, mod3, wod, wom, wout, g_ffn, wr_t, rb_t)


def _expert_kernel(be_ref, na_ref, x_ref, wg_ref, wu_ref, wd_ref, o_ref, wg_s, wu_s, wd_s):
    i = pl.program_id(0)

    @pl.when(i < na_ref[0])
    def _():
        prev = be_ref[jnp.maximum(i - 1, 0)]

        @pl.when((i == 0) | (be_ref[i] != prev))
        def _():
            wg_s[...] = wg_ref[0].astype(BF16)
            wu_s[...] = wu_ref[0].astype(BF16)
            wd_s[...] = wd_ref[0].astype(BF16)

        x = x_ref[...]
        g = _dot(x, wg_s[...])
        u = _dot(x, wu_s[...])
        a = (g * jax.nn.sigmoid(g) * u).astype(BF16)
        o_ref[...] = _dot(a, wd_s[...]).astype(BF16)


def _experts(block_expert, n_active, x_pad, w_gate, w_up, w_down, rows=EXPERT_ROWS):
    p, d = x_pad.shape
    f = w_gate.shape[2]
    n_blocks = p // rows
    def last_active(i, na):
        return jnp.minimum(i, jnp.maximum(na[0] - 1, 0))

    def xmap(i, be, na):
        return (last_active(i, na), 0)

    def wmap(i, be, na):
        return (be[last_active(i, na)], 0, 0)

    grid_spec = pltpu.PrefetchScalarGridSpec(
        num_scalar_prefetch=2,
        grid=(n_blocks,),
        in_specs=[pl.BlockSpec((rows, d), xmap),
                  pl.BlockSpec((1, d, f), wmap),
                  pl.BlockSpec((1, d, f), wmap),
                  pl.BlockSpec((1, f, d), wmap)],
        out_specs=pl.BlockSpec((rows, d), xmap),
        scratch_shapes=[pltpu.VMEM((d, f), BF16), pltpu.VMEM((d, f), BF16), pltpu.VMEM((f, d), BF16)],
    )
    return pl.pallas_call(
        _expert_kernel,
        grid_spec=grid_spec,
        out_shape=jax.ShapeDtypeStruct((p, d), BF16),
        compiler_params=_cparams("arbitrary"),
        name="experts",
    )(block_expert, n_active, x_pad, w_gate, w_up, w_down)


def _final_kernel(x1_ref, h2_ref, r_ref, mod_ref, wsg_ref, wsu_ref, wsd_ref, gf_ref, o_ref):
    h2 = h2_ref[...]
    g = _dot(h2, wsg_ref[...])
    u = _dot(h2, wsu_ref[...])
    shared = _dot((g * jax.nn.sigmoid(g) * u).astype(BF16), wsd_ref[...])
    x2 = x1_ref[...] + mod_ref[0, 5:6, :] * (r_ref[...] + shared)
    ms = jnp.mean(x2 * x2, axis=-1, keepdims=True)
    o_ref[...] = x2 * lax.rsqrt(ms + RMS_EPS) * gf_ref[...]


def _final(x1, h2, routed, mod3, wsg, wsu, wsd, g_final, seq, tm=256):
    t, d = x1.shape
    per_batch = seq // tm
    const = lambda i: (0, 0)
    row = pl.BlockSpec((tm, d), lambda i: (i, 0))
    return pl.pallas_call(
        _final_kernel,
        grid=(t // tm,),
        in_specs=[row, row, row,
                  pl.BlockSpec((1, 6, d), lambda i: (i // per_batch, 0, 0)),
                  pl.BlockSpec(wsg.shape, const),
                  pl.BlockSpec(wsu.shape, const),
                  pl.BlockSpec(wsd.shape, const),
                  pl.BlockSpec((1, d), const)],
        out_specs=row,
        out_shape=jax.ShapeDtypeStruct((t, d), F32),
        compiler_params=_cparams("arbitrary"),
        name="final",
    )(x1, h2, routed, mod3, wsg, wsu, wsd, g_final)


def _bucket(rel):
    n = jnp.maximum(rel, 0)
    max_exact = REL_BUCKETS // 2
    nf = jnp.maximum(n, 1).astype(F32)
    large = max_exact + (jnp.log(nf / max_exact) / math.log(REL_MAX_DIST / max_exact)
                         * (REL_BUCKETS - max_exact)).astype(jnp.int32)
    large = jnp.minimum(large, REL_BUCKETS - 1)
    return jnp.where(n < max_exact, n, large)


def _bias_kernel(tab_ref, o_ref, *, t):
    h = pl.program_id(0)
    sub = pl.program_id(1)
    key = lax.broadcasted_iota(jnp.int32, (t, t), 0)
    qry = lax.broadcasted_iota(jnp.int32, (t, t), 1)
    rel = qry - key + sub * t
    bucket = _bucket(rel)
    far = tab_ref[h, REL_BUCKETS - 1]
    bias = jnp.zeros((t, t), F32)
    for b in range(REL_BUCKETS - 1):
        bias = jnp.where(bucket == b, tab_ref[h, b] - far, bias)
    o_ref[0, 0] = jnp.where(rel >= 0, bias * LOG2E, NEG_BIG)


def _near_bias(table, t):
    assert t >= REL_MAX_DIST
    heads = table.shape[0]
    return pl.pallas_call(
        functools.partial(_bias_kernel, t=t),
        grid=(heads, 2),
        in_specs=[pl.BlockSpec(memory_space=pltpu.SMEM)],
        out_specs=pl.BlockSpec((1, 1, t, t), lambda h, s: (h, s, 0, 0)),
        out_shape=jax.ShapeDtypeStruct((heads, 2, t, t), F32),
        compiler_params=_cparams("arbitrary", "arbitrary"),
        name="relbias",
    )(table)


def _dispatch(wt, rows):
    e, t = wt.shape
    k = TOPK_EXPERTS
    a = t * k
    w_tok, eidx = lax.top_k(wt.T, k)
    sel = (wt > 0).astype(jnp.int32)
    counts = jnp.sum(sel, axis=1)
    rank = jnp.cumsum(sel, axis=1) - sel
    padded = ((counts + rows - 1) // rows) * rows
    pad_ends = jnp.cumsum(padded)
    pad_starts = pad_ends - padded
    starts = jnp.cumsum(counts) - counts
    pos = pad_starts[eidx] + jnp.take_along_axis(rank.T, eidx, axis=1)
    n_blocks = a // rows + e
    n_active = (pad_ends[-1] // rows).astype(jnp.int32)
    end_blocks = pad_ends // rows
    block_expert = jnp.sum(jnp.arange(n_blocks)[:, None] >= end_blocks[None, :], axis=1)
    block_expert = jnp.minimum(block_expert, e - 1).astype(jnp.int32)
    order = jnp.argsort(eidx.reshape(a))
    tok_sorted = (order // k).astype(jnp.int32)
    p = jnp.arange(n_blocks * rows)
    ep = jnp.repeat(block_expert, rows)
    j = p - pad_starts[ep]
    row_tok = jnp.where(j < counts[ep], tok_sorted[jnp.clip(starts[ep] + j, 0, a - 1)], p % t)
    return w_tok, pos, row_tok, block_expert, n_active.reshape(1)


def kernel(x, c, w_ada, b_ada, g_mix, g_ffn, w_in, diff_lambda, diff_subln_g, rel_bias, w_o_diff, w_o_moba,
           w_out, w_router, router_bias, w_exp_gate, w_exp_up, w_exp_down, w_sh_gate, w_sh_up, w_sh_down,
           g_final):
    batch, seq, d = x.shape
    t = batch * seq
    depth = w_ada.shape[0]
    assert seq % MOBA_BLOCK == 0 and seq % min(DIFF_TILE, seq) == 0
    hw = 2 * DIFF_HEAD_DIM
    qk_w = DIFF_HEADS * hw
    moba_w = MOBA_HEADS * MOBA_HEAD_DIM
    o_qm = 3 * qk_w
    o_km = o_qm + moba_w
    o_vm = o_km + moba_w
    o_gd = o_vm + moba_w
    o_gm = o_gd + d
    table_diff = rel_bias[:, :DIFF_HEADS].T
    table_moba = rel_bias[:, DIFF_HEADS:].T
    bias_diff = _near_bias(table_diff, min(DIFF_TILE, seq))
    bias_moba = _near_bias(table_moba, MOBA_BLOCK)
    colscale = jnp.ones((1, w_in.shape[2]), F32)
    colscale = colscale.at[:, :qk_w].set(DIFF_HEAD_DIM ** -0.5 * LOG2E)
    colscale = colscale.at[:, o_qm:o_km].set(MOBA_HEAD_DIM ** -0.5 * LOG2E)
    c_pad = jnp.zeros((8, d), F32).at[:batch].set(c)

    assert depth == 1, "single-layer block: the final norm is fused into the last kernel"
    l = 0
    xc = x.reshape(t, d)
    mod3 = _ada(c_pad, w_ada[l], b_ada[l][None, :])[:batch].reshape(batch, 6, d)
    proj = _inproj(xc, mod3, g_mix[l][None, :], w_in[l].astype(BF16), colscale, seq, tm=min(1024, seq))
    lam_init = 0.8 - 0.6 * math.exp(-0.3 * l)
    lv = diff_lambda[l].astype(F32)
    lam = (jnp.exp(jnp.sum(lv[0] * lv[1])) - jnp.exp(jnp.sum(lv[2] * lv[3])) + lam_init).reshape(1)
    td = min(DIFF_TILE, seq)
    qd_t = proj[:, :qk_w].T
    vd_t = proj[:, 2 * qk_w:o_qm].reshape(t // td, td, qk_w).transpose(0, 2, 1)
    od = _diff_attention(proj, qd_t, vd_t, lam, bias_diff, diff_subln_g[l][:, None], batch, seq,
                         1.0 - lam_init, td)
    kmean = _kmean(proj, o_km // moba_w, moba_w).reshape(batch, seq // MOBA_BLOCK, moba_w)
    qm_t = proj[:, o_qm:o_km].T
    vm_t = proj[:, o_vm:o_gd].reshape(t // MOBA_BLOCK, MOBA_BLOCK, moba_w).transpose(0, 2, 1)
    om = _moba_attention(proj, qm_t, vm_t, kmean, bias_moba, batch, seq, o_km)
    x1, h2, wt = _mix(od, om, proj, xc, mod3, w_o_diff[l].astype(BF16), w_o_moba[l].astype(BF16),
                      w_out[l].astype(BF16), g_ffn[l][None, :], w_router[l].T,
                      router_bias[l][:, None], seq, o_gd // d, o_gm // d)
    w_tok, pos, row_tok, block_expert, n_active = _dispatch(wt, EXPERT_ROWS)
    x_pad = h2[row_tok]
    y_pad = _experts(block_expert, n_active, x_pad, w_exp_gate[l], w_exp_up[l], w_exp_down[l])
    routed = jnp.sum(y_pad[pos].astype(F32) * w_tok[:, :, None], axis=1)
    out = _final(x1, h2, routed, mod3, w_sh_gate[l].astype(BF16), w_sh_up[l].astype(BF16),
                 w_sh_down[l].astype(BF16), g_final[None, :], seq)
    return out.reshape(batch, seq, d)
```

```python
import functools
import math

import jax
import jax.numpy as jnp
from jax import lax
from jax.experimental import pallas as pl
from jax.experimental.pallas import tpu as pltpu

F32 = jnp.float32
BF16 = jnp.bfloat16

DIFF_HEADS = 8
DIFF_HEAD_DIM = 64
MOBA_HEADS = 8
MOBA_HEAD_DIM = 128
MOBA_BLOCK = 256
MOBA_TOPK = 3
REL_BUCKETS = 32
REL_MAX_DIST = 128
N_EXPERTS = 64
N_GROUPS = 8
TOPK_GROUPS = 4
TOPK_EXPERTS = 8
ROUTED_SCALE = 2.5
RMS_EPS = 1e-6
LANES = 128
NEG_BIG = -1e30
LOG2E = 1.4426950408889634
DIFF_TILE = 512
DIFF_HEADS_PER_STEP = 4
MOBA_HEADS_PER_STEP = 4
EXPERT_ROWS = 256
VMEM_LIMIT = 56 * 1024 * 1024


def _cparams(*sem):
    return pltpu.CompilerParams(dimension_semantics=sem, vmem_limit_bytes=VMEM_LIMIT)


def _dot(a, b):
    return jnp.dot(a, b, preferred_element_type=F32)


def _dot_nt(a, b):
    return lax.dot_general(a, b, (((1,), (1,)), ((), ())), preferred_element_type=F32)


def _ada_kernel(c_ref, w_ref, b_ref, o_ref):
    c = c_ref[...]
    ca = c * jax.nn.sigmoid(c)
    o_ref[...] = jnp.dot(ca, w_ref[...], preferred_element_type=F32,
                         precision=lax.Precision.HIGHEST) + b_ref[...]


def _ada(c_pad, w_ada, b_ada, tn=1024):
    rows, d = c_pad.shape
    n = w_ada.shape[1]
    return pl.pallas_call(
        _ada_kernel,
        grid=(n // tn,),
        in_specs=[pl.BlockSpec((rows, d), lambda j: (0, 0)),
                  pl.BlockSpec((d, tn), lambda j: (0, j)),
                  pl.BlockSpec((1, tn), lambda j: (0, j))],
        out_specs=pl.BlockSpec((rows, tn), lambda j: (0, j)),
        out_shape=jax.ShapeDtypeStruct((rows, n), F32),
        compiler_params=_cparams("arbitrary"),
        name="ada",
    )(c_pad, w_ada, b_ada)


def _modulated_norm(x, g, scale, shift):
    ms = jnp.mean(x * x, axis=-1, keepdims=True)
    return (x * lax.rsqrt(ms + RMS_EPS) * g) * (1.0 + scale) + shift


def _inproj_kernel(x_ref, mod_ref, g_ref, w_ref, cs_ref, o_ref, h_ref, *, chunk):
    @pl.when(pl.program_id(1) == 0)
    def _():
        shift = mod_ref[0, 0:1, :]
        scale = mod_ref[0, 1:2, :]
        g = g_ref[...]

        def body(r, carry):
            rows = pl.ds(pl.multiple_of(r * chunk, chunk), chunk)
            h_ref[rows, :] = _modulated_norm(x_ref[rows, :], g, scale, shift).astype(BF16)
            return carry

        lax.fori_loop(0, x_ref.shape[0] // chunk, body, 0)

    o_ref[...] = (_dot(h_ref[...], w_ref[...]) * cs_ref[...]).astype(BF16)


def _inproj(x2, mod3, g_mix, w_in_bf, colscale, seq, tm=1024, tn=1024):
    t, d = x2.shape
    n = w_in_bf.shape[1]
    per_batch = seq // tm
    return pl.pallas_call(
        functools.partial(_inproj_kernel, chunk=128),
        grid=(t // tm, n // tn),
        in_specs=[pl.BlockSpec((tm, d), lambda i, j: (i, 0)),
                  pl.BlockSpec((1, 6, d), lambda i, j: (i // per_batch, 0, 0)),
                  pl.BlockSpec((1, d), lambda i, j: (0, 0)),
                  pl.BlockSpec((d, tn), lambda i, j: (0, j)),
                  pl.BlockSpec((1, tn), lambda i, j: (0, j))],
        out_specs=pl.BlockSpec((tm, tn), lambda i, j: (i, j)),
        out_shape=jax.ShapeDtypeStruct((t, n), BF16),
        scratch_shapes=[pltpu.VMEM((tm, d), BF16)],
        compiler_params=_cparams("arbitrary", "arbitrary"),
        name="inproj",
    )(x2, mod3, g_mix, w_in_bf, colscale)


def _kmean_kernel(k_ref, o_ref):
    o_ref[0] = jnp.mean(k_ref[...].astype(F32), axis=0, keepdims=True)


def _kmean(proj, col_block, width):
    t = proj.shape[0]
    nblk = t // MOBA_BLOCK
    return pl.pallas_call(
        _kmean_kernel,
        grid=(nblk,),
        in_specs=[pl.BlockSpec((MOBA_BLOCK, width), lambda i: (i, col_block))],
        out_specs=pl.BlockSpec((1, 1, width), lambda i: (i, 0, 0)),
        out_shape=jax.ShapeDtypeStruct((nblk, 1, width), F32),
        compiler_params=_cparams("arbitrary"),
        name="kmean",
    )(proj)


def _online_softmax(logits_fn, vt_fn, m, l, acc, heads, first):
    s_all = [logits_fn(g) for g in range(heads)]
    p_all, a_all = [], []
    for g, s in enumerate(s_all):
        smax = jnp.max(s, axis=0, keepdims=True)
        if first:
            mg = smax
            p = jnp.exp2(s - mg)
            l[g] = jnp.sum(p, axis=0, keepdims=True)
        else:
            m_old = m[g]
            mg = jnp.maximum(m_old, smax)
            a = jnp.exp2(m_old - mg)
            p = jnp.exp2(s - mg)
            l[g] = a * l[g] + jnp.sum(p, axis=0, keepdims=True)
            a_all.append(a)
        m[g] = mg
        p_all.append(p.astype(BF16))
    for g in range(heads):
        pv = _dot(vt_fn(g), p_all[g])
        acc[g] = pv if first else a_all[g] * acc[g] + pv


def _diff_kernel(lam_ref, qt_ref, k_ref, vt_ref, bias_ref, g_ref, o_ref, w12, m, l, acc,
                 *, t, heads, out_scale):
    qi = pl.program_id(2)
    hw = 2 * DIFF_HEAD_DIM
    zeros = jnp.zeros((DIFF_HEAD_DIM, t), BF16)
    for g in range(heads):
        w12[g, :, :t] = jnp.concatenate([qt_ref[g * hw:g * hw + DIFF_HEAD_DIM, :], zeros], axis=0)
        w12[g, :, t:] = jnp.concatenate([zeros, qt_ref[g * hw + DIFF_HEAD_DIM:(g + 1) * hw, :]], axis=0)

    def logits(g, kj):
        rows = pl.ds(pl.multiple_of(kj * t, t), t)
        return _dot(k_ref[rows, g * hw:(g + 1) * hw], w12[g])

    def vt(g, kj):
        return vt_ref[kj, g * hw:(g + 1) * hw, :]

    def both(b):
        return jnp.concatenate([b, b], axis=1)

    def chunk(kj, bias_idx, first):
        def biased(g):
            s = logits(g, kj)
            return s if bias_idx is None else s + both(bias_ref[g, bias_idx])
        _online_softmax(biased, lambda g: vt(g, kj), m, l, acc, heads, first)

    chunk(qi, 0, True)

    @pl.when(qi >= 1)
    def _():
        chunk(qi - 1, 1, False)

    def far(kj, carry):
        chunk(kj, None, False)
        return carry

    lax.fori_loop(0, jnp.maximum(qi - 1, 0), far, 0)

    for g in range(heads):
        o12 = acc[g] / l[g]
        o = o12[:, :t] - lam_ref[0] * o12[:, t:]
        ms = jnp.mean(o * o, axis=0, keepdims=True)
        o = (o * lax.rsqrt(ms + RMS_EPS) * g_ref[...]) * out_scale
        o_ref[:, g * hw:(g + 1) * hw] = o.T.astype(BF16)


def _diff_attention(proj, qt, vt, lam, bias, subln_g, batch, seq, out_scale, t, heads=DIFF_HEADS_PER_STEP):
    nq = seq // t
    hw = 2 * DIFF_HEAD_DIM
    gw = heads * hw
    ng = DIFF_HEADS // heads
    kcol = DIFF_HEADS // heads
    once = pl.Buffered(1)
    return pl.pallas_call(
        functools.partial(_diff_kernel, t=t, heads=heads, out_scale=out_scale),
        grid=(batch, ng, nq),
        in_specs=[pl.BlockSpec(memory_space=pltpu.SMEM),
                  pl.BlockSpec((gw, t), lambda b, h, i: (h, b * nq + i)),
                  pl.BlockSpec((seq, gw), lambda b, h, i: (b, kcol + h), pipeline_mode=once),
                  pl.BlockSpec((nq, gw, t), lambda b, h, i: (b, h, 0), pipeline_mode=once),
                  pl.BlockSpec((heads, 2, t, t), lambda b, h, i: (h, 0, 0, 0), pipeline_mode=once),
                  pl.BlockSpec((hw, 1), lambda b, h, i: (0, 0))],
        out_specs=pl.BlockSpec((t, gw), lambda b, h, i: (b * nq + i, h)),
        out_shape=jax.ShapeDtypeStruct((batch * seq, DIFF_HEADS * hw), BF16),
        scratch_shapes=[pltpu.VMEM((heads, hw, 2 * t), BF16),
                        pltpu.VMEM((heads, 1, 2 * t), F32), pltpu.VMEM((heads, 1, 2 * t), F32),
                        pltpu.VMEM((heads, hw, 2 * t), F32)],
        compiler_params=_cparams("arbitrary", "arbitrary", "arbitrary"),
        name="diffattn",
    )(lam, qt, proj, vt, bias, subln_g)


def _moba_kernel(qt_ref, k_ref, vt_ref, km_ref, bias_ref, o_ref, m, l, acc, sel_ref, *, t, heads):
    cur = pl.program_id(2)
    dh = MOBA_HEAD_DIM
    nb = km_ref.shape[1]

    def cols(g):
        return slice(g * dh, (g + 1) * dh)

    for g in range(heads):
        gate = jnp.dot(km_ref[0, :, cols(g)], qt_ref[cols(g), :].astype(F32), preferred_element_type=F32,
                       precision=lax.Precision.HIGHEST)
        blk = lax.broadcasted_iota(jnp.int32, gate.shape, 0)
        gate = jnp.where(blk < cur, gate, -jnp.inf)
        sel = jnp.zeros(gate.shape, F32)
        for _ in range(MOBA_TOPK):
            gmax = jnp.max(gate, axis=0, keepdims=True)
            first = jnp.min(jnp.where(gate == gmax, blk, nb), axis=0, keepdims=True)
            pick = (blk == first) & (gmax > -jnp.inf)
            sel = jnp.where(pick, 1.0, sel)
            gate = jnp.where(blk == first, -jnp.inf, gate)
        sel_ref[g] = sel

    def logits(g, j0, nblk):
        rows = pl.ds(pl.multiple_of(j0 * t, t), nblk * t)
        return _dot(k_ref[rows, cols(g)], qt_ref[cols(g), :])

    def keep(g, j, valid):
        return jnp.where(valid, sel_ref[g, pl.ds(j, 1), :], 0.0) > 0.5

    def pair_vt(g, j0):
        return jnp.concatenate([vt_ref[j0, cols(g), :], vt_ref[j0 + 1, cols(g), :]], axis=1)

    @pl.when(cur == 0)
    def _():
        _online_softmax(lambda g: logits(g, 0, 1) + bias_ref[g, 0], lambda g: vt_ref[0, cols(g), :],
                        m, l, acc, heads, True)

    @pl.when(cur >= 1)
    def _():
        def near(g):
            s = logits(g, cur - 1, 2)
            prev = jnp.where(keep(g, cur - 1, True), s[:t] + bias_ref[g, 1], NEG_BIG)
            return jnp.concatenate([prev, s[t:] + bias_ref[g, 0]], axis=0)
        _online_softmax(near, lambda g: pair_vt(g, cur - 1), m, l, acc, heads, True)

    nfar = jnp.maximum(cur - 1, 0)

    def far(c, carry):
        j0 = 2 * c

        def masked(g):
            s = logits(g, j0, 2)
            s0 = jnp.where(keep(g, j0, True), s[:t], NEG_BIG)
            s1 = jnp.where(keep(g, j0 + 1, j0 + 1 < nfar), s[t:], NEG_BIG)
            return jnp.concatenate([s0, s1], axis=0)
        _online_softmax(masked, lambda g: pair_vt(g, j0), m, l, acc, heads, False)
        return carry

    lax.fori_loop(0, (nfar + 1) // 2, far, 0)
    for g in range(heads):
        o_ref[:, cols(g)] = (acc[g] / l[g]).T.astype(BF16)


def _moba_attention(proj, qt, vt, kmean, bias, batch, seq, kcol_w, heads=MOBA_HEADS_PER_STEP):
    t = MOBA_BLOCK
    nq = seq // t
    dh = MOBA_HEAD_DIM
    gw = heads * dh
    ng = MOBA_HEADS // heads
    kcol = kcol_w // gw
    nb = kmean.shape[1]
    once = pl.Buffered(1)
    return pl.pallas_call(
        functools.partial(_moba_kernel, t=t, heads=heads),
        grid=(batch, ng, nq),
        in_specs=[pl.BlockSpec((gw, t), lambda b, h, i: (h, b * nq + i)),
                  pl.BlockSpec((seq, gw), lambda b, h, i: (b, kcol + h), pipeline_mode=once),
                  pl.BlockSpec((nq, gw, t), lambda b, h, i: (b, h, 0), pipeline_mode=once),
                  pl.BlockSpec((1, nb, gw), lambda b, h, i: (b, 0, h)),
                  pl.BlockSpec((heads, 2, t, t), lambda b, h, i: (h, 0, 0, 0), pipeline_mode=once)],
        out_specs=pl.BlockSpec((t, gw), lambda b, h, i: (b * nq + i, h)),
        out_shape=jax.ShapeDtypeStruct((batch * seq, MOBA_HEADS * dh), BF16),
        scratch_shapes=[pltpu.VMEM((heads, 1, t), F32), pltpu.VMEM((heads, 1, t), F32),
                        pltpu.VMEM((heads, dh, t), F32), pltpu.VMEM((heads, nb, t), F32)],
        compiler_params=_cparams("arbitrary", "arbitrary", "arbitrary"),
        name="moba",
    )(qt, proj, vt, kmean, bias)


def _first_index_of_max(vals, ids, sentinel):
    vmax = jnp.max(vals, axis=0, keepdims=True)
    first = jnp.min(jnp.where(vals == vmax, ids, sentinel), axis=0, keepdims=True)
    return vmax, first


def _route(scores_t, bias_t):
    e, n = scores_t.shape
    per = e // N_GROUPS
    selv = scores_t + bias_t
    sub = lax.broadcasted_iota(jnp.int32, (per, n), 0)
    gscore = []
    for g in range(N_GROUPS):
        blk = selv[g * per:(g + 1) * per, :]
        top1, first = _first_index_of_max(blk, sub, per)
        top2 = jnp.max(jnp.where(sub == first, -jnp.inf, blk), axis=0, keepdims=True)
        gscore.append(top1 + top2)
    gs = jnp.concatenate(gscore, axis=0)
    gid = lax.broadcasted_iota(jnp.int32, gs.shape, 0)
    gsel = jnp.zeros(gs.shape, F32)
    for _ in range(TOPK_GROUPS):
        _, first = _first_index_of_max(gs, gid, N_GROUPS)
        gsel = jnp.where(gid == first, 1.0, gsel)
        gs = jnp.where(gid == first, -jnp.inf, gs)
    masked = jnp.concatenate(
        [jnp.where(gsel[g:g + 1, :] > 0.5, selv[g * per:(g + 1) * per, :], -jnp.inf)
         for g in range(N_GROUPS)], axis=0)
    eid = lax.broadcasted_iota(jnp.int32, masked.shape, 0)
    picked = jnp.zeros(masked.shape, F32)
    for _ in range(TOPK_EXPERTS):
        _, first = _first_index_of_max(masked, eid, e)
        picked = jnp.where(eid == first, 1.0, picked)
        masked = jnp.where(eid == first, -jnp.inf, masked)
    w = jnp.where(picked > 0.5, scores_t, 0.0)
    return w / jnp.sum(w, axis=0, keepdims=True) * ROUTED_SCALE


def _route_kernel(logits_ref, rb_ref, wt_ref):
    wt_ref[...] = _route(jax.nn.sigmoid(logits_ref[...]), rb_ref[...])


def _route_weights(logits_t, rb_t, tn=2048):
    e, t = logits_t.shape
    tn = min(tn, t)
    return pl.pallas_call(
        _route_kernel,
        grid=(t // tn,),
        in_specs=[pl.BlockSpec((e, tn), lambda i: (0, i)),
                  pl.BlockSpec((e, 1), lambda i: (0, 0))],
        out_specs=pl.BlockSpec((e, tn), lambda i: (0, i)),
        out_shape=jax.ShapeDtypeStruct((e, t), F32),
        compiler_params=_cparams("arbitrary"),
        name="route",
    )(logits_t, rb_t)


def _mix_kernel(od_ref, om_ref, gd_ref, gm_ref, x_ref, mod_ref, wod_ref, wom_ref, wout_ref,
                gffn_ref, wrh_ref, wrl_ref, x1_ref, h2_ref, lg_ref, *, parts):
    n = x_ref.shape[0] // parts
    rows = [slice(p * n, (p + 1) * n) for p in range(parts)]
    y = [(_dot(od_ref[r, :], wod_ref[...]), _dot(om_ref[r, :], wom_ref[...])) for r in rows]
    mixed = []
    for r, (yd, ym) in zip(rows, y):
        z = (jax.nn.sigmoid(gd_ref[r, :].astype(F32)) * yd
             + jax.nn.sigmoid(gm_ref[r, :].astype(F32)) * ym)
        mixed.append(_dot(z.astype(BF16), wout_ref[...]))
    for r, mx in zip(rows, mixed):
        x1 = x_ref[r, :] + mod_ref[0, 2:3, :] * mx
        x1_ref[r, :] = x1
        h2 = _modulated_norm(x1, gffn_ref[...], mod_ref[0, 4:5, :], mod_ref[0, 3:4, :])
        h_hi = h2.astype(BF16)
        h2_ref[r, :] = h_hi
        h_lo = (h2 - h_hi.astype(F32)).astype(BF16)
        lg_ref[r, :] = (_dot(h_hi, wrh_ref[...]) + _dot(h_lo, wrh_ref[...])) + _dot(h_hi, wrl_ref[...])


def _mix(od, om, proj, x2, mod3, wod, wom, wout, g_ffn, w_router, seq, gd_col, gm_col, tm=256):
    t, d = x2.shape
    wr_hi = w_router.astype(BF16)
    wr_lo = (w_router - wr_hi.astype(F32)).astype(BF16)
    per_batch = seq // tm
    const = lambda i: (0, 0)
    once = pl.Buffered(1)
    return pl.pallas_call(
        functools.partial(_mix_kernel, parts=2),
        grid=(t // tm,),
        in_specs=[pl.BlockSpec((tm, od.shape[1]), lambda i: (i, 0)),
                  pl.BlockSpec((tm, om.shape[1]), lambda i: (i, 0)),
                  pl.BlockSpec((tm, d), lambda i: (i, gd_col)),
                  pl.BlockSpec((tm, d), lambda i: (i, gm_col)),
                  pl.BlockSpec((tm, d), lambda i: (i, 0)),
                  pl.BlockSpec((1, 6, d), lambda i: (i // per_batch, 0, 0)),
                  pl.BlockSpec(wod.shape, const, pipeline_mode=once),
                  pl.BlockSpec(wom.shape, const, pipeline_mode=once),
                  pl.BlockSpec(wout.shape, const, pipeline_mode=once),
                  pl.BlockSpec((1, d), const),
                  pl.BlockSpec(wr_hi.shape, const),
                  pl.BlockSpec(wr_lo.shape, const)],
        out_specs=[pl.BlockSpec((tm, d), lambda i: (i, 0)),
                   pl.BlockSpec((tm, d), lambda i: (i, 0)),
                   pl.BlockSpec((tm, N_EXPERTS), lambda i: (i, 0))],
        out_shape=[jax.ShapeDtypeStruct((t, d), F32),
                   jax.ShapeDtypeStruct((t, d), BF16),
                   jax.ShapeDtypeStruct((t, N_EXPERTS), F32)],
        compiler_params=_cparams("arbitrary"),
        name="mix",
    )(od, om, proj, proj, x2, mod3, wod, wom, wout, g_ffn, wr_hi, wr_lo)


def _expert_kernel(be_ref, na_ref, x_ref, rw_ref, wg_ref, wu_ref, wd_ref, o_ref, wg_s, wu_s, wd_s):
    i = pl.program_id(0)

    @pl.when(i < na_ref[0])
    def _():
        prev = be_ref[jnp.maximum(i - 1, 0)]

        @pl.when((i == 0) | (be_ref[i] != prev))
        def _():
            wg_s[...] = wg_ref[0].astype(BF16)
            wu_s[...] = wu_ref[0].astype(BF16)
            wd_s[...] = wd_ref[0].astype(BF16)

        x = x_ref[...]
        g = _dot(x, wg_s[...])
        u = _dot(x, wu_s[...])
        a = (g * jax.nn.sigmoid(g) * u).astype(BF16)
        y = _dot(a, wd_s[...])
        rows = x.shape[0]
        w_row = rw_ref[0]
        diag = (lax.broadcasted_iota(jnp.int32, (rows, rows), 0)
                == lax.broadcasted_iota(jnp.int32, (rows, rows), 1))
        w_col = jnp.sum(jnp.where(diag, w_row, 0.0), axis=1, keepdims=True)
        o_ref[...] = (y * w_col).astype(BF16)

    @pl.when(i >= na_ref[0])
    def _():
        o_ref[...] = jnp.zeros(o_ref.shape, o_ref.dtype)


def _experts(block_expert, n_active, x_pad, row_w, w_gate, w_up, w_down, rows=EXPERT_ROWS):
    p, d = x_pad.shape
    f = w_gate.shape[2]
    n_blocks = p // rows
    def last_active(i, na):
        return jnp.minimum(i, jnp.maximum(na[0] - 1, 0))

    def xmap(i, be, na):
        return (last_active(i, na), 0)

    def rmap(i, be, na):
        return (last_active(i, na), 0, 0)

    def wmap(i, be, na):
        return (be[last_active(i, na)], 0, 0)

    grid_spec = pltpu.PrefetchScalarGridSpec(
        num_scalar_prefetch=2,
        grid=(n_blocks,),
        in_specs=[pl.BlockSpec((rows, d), xmap),
                  pl.BlockSpec((1, 1, rows), rmap),
                  pl.BlockSpec((1, d, f), wmap),
                  pl.BlockSpec((1, d, f), wmap),
                  pl.BlockSpec((1, f, d), wmap)],
        out_specs=pl.BlockSpec((rows, d), lambda i, be, na: (i, 0)),
        scratch_shapes=[pltpu.VMEM((d, f), BF16), pltpu.VMEM((d, f), BF16), pltpu.VMEM((f, d), BF16)],
    )
    return pl.pallas_call(
        _expert_kernel,
        grid_spec=grid_spec,
        out_shape=jax.ShapeDtypeStruct((p, d), BF16),
        compiler_params=_cparams("arbitrary"),
        name="experts",
    )(block_expert, n_active, x_pad, row_w.reshape(n_blocks, 1, rows), w_gate, w_up, w_down)


def _final_kernel(x1_ref, h2_ref, y_ref, mod_ref, wsg_ref, wsu_ref, wsd_ref, gf_ref, o_ref):
    h2 = h2_ref[...]
    g = _dot(h2, wsg_ref[...])
    u = _dot(h2, wsu_ref[...])
    shared = _dot((g * jax.nn.sigmoid(g) * u).astype(BF16), wsd_ref[...])
    routed = y_ref[0].astype(F32)
    for k in range(1, y_ref.shape[0]):
        routed = routed + y_ref[k].astype(F32)
    x2 = x1_ref[...] + mod_ref[0, 5:6, :] * (routed + shared)
    ms = jnp.mean(x2 * x2, axis=-1, keepdims=True)
    o_ref[...] = x2 * lax.rsqrt(ms + RMS_EPS) * gf_ref[...]


def _final(x1, h2, y_tok, mod3, wsg, wsu, wsd, g_final, seq, tm=256):
    t, d = x1.shape
    per_batch = seq // tm
    const = lambda i: (0, 0)
    row = pl.BlockSpec((tm, d), lambda i: (i, 0))
    return pl.pallas_call(
        _final_kernel,
        grid=(t // tm,),
        in_specs=[row, row,
                  pl.BlockSpec((y_tok.shape[0], tm, d), lambda i: (0, i, 0)),
                  pl.BlockSpec((1, 6, d), lambda i: (i // per_batch, 0, 0)),
                  pl.BlockSpec(wsg.shape, const),
                  pl.BlockSpec(wsu.shape, const),
                  pl.BlockSpec(wsd.shape, const),
                  pl.BlockSpec((1, d), const)],
        out_specs=row,
        out_shape=jax.ShapeDtypeStruct((t, d), F32),
        compiler_params=_cparams("arbitrary"),
        name="final",
    )(x1, h2, y_tok, mod3, wsg, wsu, wsd, g_final)


def _bucket(rel):
    n = jnp.maximum(rel, 0)
    max_exact = REL_BUCKETS // 2
    nf = jnp.maximum(n, 1).astype(F32)
    large = max_exact + (jnp.log(nf / max_exact) / math.log(REL_MAX_DIST / max_exact)
                         * (REL_BUCKETS - max_exact)).astype(jnp.int32)
    large = jnp.minimum(large, REL_BUCKETS - 1)
    return jnp.where(n < max_exact, n, large)


def _bias_kernel(tab_ref, o_ref, *, t):
    h = pl.program_id(0)
    sub = pl.program_id(1)
    key = lax.broadcasted_iota(jnp.int32, (t, t), 0)
    qry = lax.broadcasted_iota(jnp.int32, (t, t), 1)
    rel = qry - key + sub * t
    bucket = _bucket(rel)
    far = tab_ref[h, REL_BUCKETS - 1]
    bias = jnp.zeros((t, t), F32)
    for b in range(REL_BUCKETS - 1):
        bias = jnp.where(bucket == b, tab_ref[h, b] - far, bias)
    o_ref[0, 0] = jnp.where(rel >= 0, bias * LOG2E, NEG_BIG)


def _near_bias(table, t):
    assert t >= REL_MAX_DIST
    heads = table.shape[0]
    return pl.pallas_call(
        functools.partial(_bias_kernel, t=t),
        grid=(heads, 2),
        in_specs=[pl.BlockSpec(memory_space=pltpu.SMEM)],
        out_specs=pl.BlockSpec((1, 1, t, t), lambda h, s: (h, s, 0, 0)),
        out_shape=jax.ShapeDtypeStruct((heads, 2, t, t), F32),
        compiler_params=_cparams("arbitrary", "arbitrary"),
        name="relbias",
    )(table)


def _dispatch(wt, rows):
    e, t = wt.shape
    k = TOPK_EXPERTS
    a = t * k
    w_tok, eidx = lax.top_k(wt.T, k)
    sel = (wt > 0).astype(jnp.int32)
    counts = jnp.sum(sel, axis=1)
    rank = jnp.cumsum(sel, axis=1) - sel
    padded = ((counts + rows - 1) // rows) * rows
    pad_ends = jnp.cumsum(padded)
    pad_starts = pad_ends - padded
    starts = jnp.cumsum(counts) - counts
    pos = pad_starts[eidx] + jnp.take_along_axis(rank.T, eidx, axis=1)
    n_blocks = a // rows + e
    n_active = (pad_ends[-1] // rows).astype(jnp.int32)
    end_blocks = pad_ends // rows
    block_expert = jnp.sum(jnp.arange(n_blocks)[:, None] >= end_blocks[None, :], axis=1)
    block_expert = jnp.minimum(block_expert, e - 1).astype(jnp.int32)
    order = jnp.argsort(eidx.reshape(a))
    tok_sorted = (order // k).astype(jnp.int32)
    p = jnp.arange(n_blocks * rows)
    ep = jnp.repeat(block_expert, rows)
    j = p - pad_starts[ep]
    valid = j < counts[ep]
    src = jnp.clip(starts[ep] + j, 0, a - 1)
    row_tok = jnp.where(valid, tok_sorted[src], p % t)
    row_w = jnp.where(valid, w_tok.reshape(a)[order][src], 0.0)
    return pos, row_tok, row_w, block_expert, n_active.reshape(1)


def kernel(x, c, w_ada, b_ada, g_mix, g_ffn, w_in, diff_lambda, diff_subln_g, rel_bias, w_o_diff, w_o_moba,
           w_out, w_router, router_bias, w_exp_gate, w_exp_up, w_exp_down, w_sh_gate, w_sh_up, w_sh_down,
           g_final):
    batch, seq, d = x.shape
    t = batch * seq
    depth = w_ada.shape[0]
    assert seq % MOBA_BLOCK == 0 and seq % min(DIFF_TILE, seq) == 0
    hw = 2 * DIFF_HEAD_DIM
    qk_w = DIFF_HEADS * hw
    moba_w = MOBA_HEADS * MOBA_HEAD_DIM
    o_qm = 3 * qk_w
    o_km = o_qm + moba_w
    o_vm = o_km + moba_w
    o_gd = o_vm + moba_w
    o_gm = o_gd + d
    table_diff = rel_bias[:, :DIFF_HEADS].T
    table_moba = rel_bias[:, DIFF_HEADS:].T
    bias_diff = _near_bias(table_diff, min(DIFF_TILE, seq))
    bias_moba = _near_bias(table_moba, MOBA_BLOCK)
    colscale = jnp.ones((1, w_in.shape[2]), F32)
    colscale = colscale.at[:, :qk_w].set(DIFF_HEAD_DIM ** -0.5 * LOG2E)
    colscale = colscale.at[:, o_qm:o_km].set(MOBA_HEAD_DIM ** -0.5 * LOG2E)
    c_pad = jnp.zeros((8, d), F32).at[:batch].set(c)

    assert depth == 1, "single-layer block: the final norm is fused into the last kernel"
    l = 0
    xc = x.reshape(t, d)
    mod3 = _ada(c_pad, w_ada[l], b_ada[l][None, :])[:batch].reshape(batch, 6, d)
    proj = _inproj(xc, mod3, g_mix[l][None, :], w_in[l].astype(BF16), colscale, seq, tm=min(1024, seq))
    lam_init = 0.8 - 0.6 * math.exp(-0.3 * l)
    lv = diff_lambda[l].astype(F32)
    lam = (jnp.exp(jnp.sum(lv[0] * lv[1])) - jnp.exp(jnp.sum(lv[2] * lv[3])) + lam_init).reshape(1)
    td = min(DIFF_TILE, seq)
    qd_t = proj[:, :qk_w].T
    vd_t = proj[:, 2 * qk_w:o_qm].reshape(t // td, td, qk_w).transpose(0, 2, 1)
    od = _diff_attention(proj, qd_t, vd_t, lam, bias_diff, diff_subln_g[l][:, None], batch, seq,
                         1.0 - lam_init, td)
    kmean = _kmean(proj, o_km // moba_w, moba_w).reshape(batch, seq // MOBA_BLOCK, moba_w)
    qm_t = proj[:, o_qm:o_km].T
    vm_t = proj[:, o_vm:o_gd].reshape(t // MOBA_BLOCK, MOBA_BLOCK, moba_w).transpose(0, 2, 1)
    om = _moba_attention(proj, qm_t, vm_t, kmean, bias_moba, batch, seq, o_km)
    x1, h2, logits = _mix(od, om, proj, xc, mod3, w_o_diff[l].astype(BF16), w_o_moba[l].astype(BF16),
                          w_out[l].astype(BF16), g_ffn[l][None, :], w_router[l], seq,
                          o_gd // d, o_gm // d)
    wt = _route_weights(logits.T, router_bias[l][:, None])
    pos, row_tok, row_w, block_expert, n_active = _dispatch(wt, EXPERT_ROWS)
    x_pad = h2[row_tok]
    y_pad = _experts(block_expert, n_active, x_pad, row_w, w_exp_gate[l], w_exp_up[l], w_exp_down[l])
    y_tok = y_pad[pos.T.reshape(-1)].reshape(TOPK_EXPERTS, t, d)
    out = _final(x1, h2, y_tok, mod3, w_sh_gate[l].astype(BF16), w_sh_up[l].astype(BF16),
                 w_sh_down[l].astype(BF16), g_final[None, :], seq)
    return out.reshape(batch, seq, d)
```

```python
import functools
import math

import jax
import jax.numpy as jnp
from jax import lax
from jax.experimental import pallas as pl
from jax.experimental.pallas import tpu as pltpu

F32 = jnp.float32
BF16 = jnp.bfloat16

DIFF_HEADS = 8
DIFF_HEAD_DIM = 64
MOBA_HEADS = 8
MOBA_HEAD_DIM = 128
MOBA_BLOCK = 256
MOBA_TOPK = 3
REL_BUCKETS = 32
REL_MAX_DIST = 128
N_EXPERTS = 64
N_GROUPS = 8
TOPK_GROUPS = 4
TOPK_EXPERTS = 8
ROUTED_SCALE = 2.5
RMS_EPS = 1e-6
LANES = 128
NEG_BIG = -1e30
LOG2E = 1.4426950408889634
DIFF_TILE = 512
DIFF_HEADS_PER_STEP = 4
MOBA_HEADS_PER_STEP = 4
EXPERT_ROWS = 512
MOE_CHUNKS = 4
VMEM_LIMIT = 56 * 1024 * 1024


def _cparams(*sem):
    return pltpu.CompilerParams(dimension_semantics=sem, vmem_limit_bytes=VMEM_LIMIT)


def _dot(a, b):
    return jnp.dot(a, b, preferred_element_type=F32)


def _dot_nt(a, b):
    return lax.dot_general(a, b, (((1,), (1,)), ((), ())), preferred_element_type=F32)


def _ada_kernel(c_ref, w_ref, b_ref, o_ref):
    c = c_ref[...]
    ca = c * jax.nn.sigmoid(c)
    o_ref[...] = jnp.dot(ca, w_ref[...], preferred_element_type=F32,
                         precision=lax.Precision.HIGHEST) + b_ref[...]


def _ada(c_pad, w_ada, b_ada, tn=1024):
    rows, d = c_pad.shape
    n = w_ada.shape[1]
    return pl.pallas_call(
        _ada_kernel,
        grid=(n // tn,),
        in_specs=[pl.BlockSpec((rows, d), lambda j: (0, 0)),
                  pl.BlockSpec((d, tn), lambda j: (0, j)),
                  pl.BlockSpec((1, tn), lambda j: (0, j))],
        out_specs=pl.BlockSpec((rows, tn), lambda j: (0, j)),
        out_shape=jax.ShapeDtypeStruct((rows, n), F32),
        compiler_params=_cparams("arbitrary"),
        name="ada",
    )(c_pad, w_ada, b_ada)


def _modulated_norm(x, g, scale, shift):
    ms = jnp.mean(x * x, axis=-1, keepdims=True)
    return (x * lax.rsqrt(ms + RMS_EPS) * g) * (1.0 + scale) + shift


def _inproj_kernel(x_ref, mod_ref, g_ref, w_ref, cs_ref, o_ref, h_ref, *, chunk):
    @pl.when(pl.program_id(1) == 0)
    def _():
        shift = mod_ref[0, 0:1, :]
        scale = mod_ref[0, 1:2, :]
        g = g_ref[...]

        def body(r, carry):
            rows = pl.ds(pl.multiple_of(r * chunk, chunk), chunk)
            h_ref[rows, :] = _modulated_norm(x_ref[rows, :], g, scale, shift).astype(BF16)
            return carry

        lax.fori_loop(0, x_ref.shape[0] // chunk, body, 0)

    o_ref[...] = (_dot(h_ref[...], w_ref[...]) * cs_ref[...]).astype(BF16)


def _inproj(x2, mod3, g_mix, w_in_bf, colscale, seq, tm=1024, tn=1024):
    t, d = x2.shape
    n = w_in_bf.shape[1]
    per_batch = seq // tm
    return pl.pallas_call(
        functools.partial(_inproj_kernel, chunk=128),
        grid=(t // tm, n // tn),
        in_specs=[pl.BlockSpec((tm, d), lambda i, j: (i, 0)),
                  pl.BlockSpec((1, 6, d), lambda i, j: (i // per_batch, 0, 0)),
                  pl.BlockSpec((1, d), lambda i, j: (0, 0)),
                  pl.BlockSpec((d, tn), lambda i, j: (0, j)),
                  pl.BlockSpec((1, tn), lambda i, j: (0, j))],
        out_specs=pl.BlockSpec((tm, tn), lambda i, j: (i, j)),
        out_shape=jax.ShapeDtypeStruct((t, n), BF16),
        scratch_shapes=[pltpu.VMEM((tm, d), BF16)],
        compiler_params=_cparams("arbitrary", "arbitrary"),
        name="inproj",
    )(x2, mod3, g_mix, w_in_bf, colscale)


def _kmean_kernel(k_ref, o_ref):
    o_ref[0] = jnp.mean(k_ref[...].astype(F32), axis=0, keepdims=True)


def _kmean(proj, col_block, width):
    t = proj.shape[0]
    nblk = t // MOBA_BLOCK
    return pl.pallas_call(
        _kmean_kernel,
        grid=(nblk,),
        in_specs=[pl.BlockSpec((MOBA_BLOCK, width), lambda i: (i, col_block))],
        out_specs=pl.BlockSpec((1, 1, width), lambda i: (i, 0, 0)),
        out_shape=jax.ShapeDtypeStruct((nblk, 1, width), F32),
        compiler_params=_cparams("arbitrary"),
        name="kmean",
    )(proj)


def _online_softmax(logits_fn, vt_fn, m, l, acc, heads, first):
    s_all = [logits_fn(g) for g in range(heads)]
    p_all, a_all = [], []
    for g, s in enumerate(s_all):
        smax = jnp.max(s, axis=0, keepdims=True)
        if first:
            mg = smax
            p = jnp.exp2(s - mg)
            l[g] = jnp.sum(p, axis=0, keepdims=True)
        else:
            m_old = m[g]
            mg = jnp.maximum(m_old, smax)
            a = jnp.exp2(m_old - mg)
            p = jnp.exp2(s - mg)
            l[g] = a * l[g] + jnp.sum(p, axis=0, keepdims=True)
            a_all.append(a)
        m[g] = mg
        p_all.append(p.astype(BF16))
    for g in range(heads):
        pv = _dot(vt_fn(g), p_all[g])
        acc[g] = pv if first else a_all[g] * acc[g] + pv


def _diff_kernel(lam_ref, qt_ref, k_ref, vt_ref, bias_ref, g_ref, o_ref, w12, m, l, acc,
                 *, t, heads, out_scale):
    qi = pl.program_id(2)
    hw = 2 * DIFF_HEAD_DIM
    zeros = jnp.zeros((DIFF_HEAD_DIM, t), BF16)
    for g in range(heads):
        w12[g, :, :t] = jnp.concatenate([qt_ref[g * hw:g * hw + DIFF_HEAD_DIM, :], zeros], axis=0)
        w12[g, :, t:] = jnp.concatenate([zeros, qt_ref[g * hw + DIFF_HEAD_DIM:(g + 1) * hw, :]], axis=0)

    def logits(g, kj):
        rows = pl.ds(pl.multiple_of(kj * t, t), t)
        return _dot(k_ref[rows, g * hw:(g + 1) * hw], w12[g])

    def vt(g, kj):
        return vt_ref[kj, g * hw:(g + 1) * hw, :]

    def both(b):
        return jnp.concatenate([b, b], axis=1)

    def chunk(kj, bias_idx, first):
        def biased(g):
            s = logits(g, kj)
            return s if bias_idx is None else s + both(bias_ref[g, bias_idx])
        _online_softmax(biased, lambda g: vt(g, kj), m, l, acc, heads, first)

    chunk(qi, 0, True)

    @pl.when(qi >= 1)
    def _():
        chunk(qi - 1, 1, False)

    def far(kj, carry):
        chunk(kj, None, False)
        return carry

    lax.fori_loop(0, jnp.maximum(qi - 1, 0), far, 0)

    for g in range(heads):
        o12 = acc[g] / l[g]
        o = o12[:, :t] - lam_ref[0] * o12[:, t:]
        ms = jnp.mean(o * o, axis=0, keepdims=True)
        o = (o * lax.rsqrt(ms + RMS_EPS) * g_ref[...]) * out_scale
        o_ref[:, g * hw:(g + 1) * hw] = o.T.astype(BF16)


def _diff_attention(proj, qt, vt, lam, bias, subln_g, batch, seq, out_scale, t, heads=DIFF_HEADS_PER_STEP):
    nq = seq // t
    hw = 2 * DIFF_HEAD_DIM
    gw = heads * hw
    ng = DIFF_HEADS // heads
    kcol = DIFF_HEADS // heads
    once = pl.Buffered(1)
    return pl.pallas_call(
        functools.partial(_diff_kernel, t=t, heads=heads, out_scale=out_scale),
        grid=(batch, ng, nq),
        in_specs=[pl.BlockSpec(memory_space=pltpu.SMEM),
                  pl.BlockSpec((gw, t), lambda b, h, i: (h, b * nq + i)),
                  pl.BlockSpec((seq, gw), lambda b, h, i: (b, kcol + h), pipeline_mode=once),
                  pl.BlockSpec((nq, gw, t), lambda b, h, i: (b, h, 0), pipeline_mode=once),
                  pl.BlockSpec((heads, 2, t, t), lambda b, h, i: (h, 0, 0, 0), pipeline_mode=once),
                  pl.BlockSpec((hw, 1), lambda b, h, i: (0, 0))],
        out_specs=pl.BlockSpec((t, gw), lambda b, h, i: (b * nq + i, h)),
        out_shape=jax.ShapeDtypeStruct((batch * seq, DIFF_HEADS * hw), BF16),
        scratch_shapes=[pltpu.VMEM((heads, hw, 2 * t), BF16),
                        pltpu.VMEM((heads, 1, 2 * t), F32), pltpu.VMEM((heads, 1, 2 * t), F32),
                        pltpu.VMEM((heads, hw, 2 * t), F32)],
        compiler_params=_cparams("arbitrary", "arbitrary", "arbitrary"),
        name="diffattn",
    )(lam, qt, proj, vt, bias, subln_g)


def _moba_kernel(qt_ref, k_ref, vt_ref, km_ref, bias_ref, o_ref, m, l, acc, sel_ref, *, t, heads):
    cur = pl.program_id(2)
    dh = MOBA_HEAD_DIM
    nb = km_ref.shape[1]

    def cols(g):
        return slice(g * dh, (g + 1) * dh)

    for g in range(heads):
        gate = jnp.dot(km_ref[0, :, cols(g)], qt_ref[cols(g), :].astype(F32), preferred_element_type=F32,
                       precision=lax.Precision.HIGHEST)
        blk = lax.broadcasted_iota(jnp.int32, gate.shape, 0)
        gate = jnp.where(blk < cur, gate, -jnp.inf)
        sel = jnp.zeros(gate.shape, F32)
        for _ in range(MOBA_TOPK):
            gmax = jnp.max(gate, axis=0, keepdims=True)
            first = jnp.min(jnp.where(gate == gmax, blk, nb), axis=0, keepdims=True)
            pick = (blk == first) & (gmax > -jnp.inf)
            sel = jnp.where(pick, 1.0, sel)
            gate = jnp.where(blk == first, -jnp.inf, gate)
        sel_ref[g] = sel

    def logits(g, j0, nblk):
        rows = pl.ds(pl.multiple_of(j0 * t, t), nblk * t)
        return _dot(k_ref[rows, cols(g)], qt_ref[cols(g), :])

    def keep(g, j, valid):
        return jnp.where(valid, sel_ref[g, pl.ds(j, 1), :], 0.0) > 0.5

    def pair_vt(g, j0):
        return jnp.concatenate([vt_ref[j0, cols(g), :], vt_ref[j0 + 1, cols(g), :]], axis=1)

    @pl.when(cur == 0)
    def _():
        _online_softmax(lambda g: logits(g, 0, 1) + bias_ref[g, 0], lambda g: vt_ref[0, cols(g), :],
                        m, l, acc, heads, True)

    @pl.when(cur >= 1)
    def _():
        def near(g):
            s = logits(g, cur - 1, 2)
            prev = jnp.where(keep(g, cur - 1, True), s[:t] + bias_ref[g, 1], NEG_BIG)
            return jnp.concatenate([prev, s[t:] + bias_ref[g, 0]], axis=0)
        _online_softmax(near, lambda g: pair_vt(g, cur - 1), m, l, acc, heads, True)

    nfar = jnp.maximum(cur - 1, 0)

    def far(c, carry):
        j0 = 2 * c

        def masked(g):
            s = logits(g, j0, 2)
            s0 = jnp.where(keep(g, j0, True), s[:t], NEG_BIG)
            s1 = jnp.where(keep(g, j0 + 1, j0 + 1 < nfar), s[t:], NEG_BIG)
            return jnp.concatenate([s0, s1], axis=0)
        _online_softmax(masked, lambda g: pair_vt(g, j0), m, l, acc, heads, False)
        return carry

    lax.fori_loop(0, (nfar + 1) // 2, far, 0)
    for g in range(heads):
        o_ref[:, cols(g)] = (acc[g] / l[g]).T.astype(BF16)


def _moba_attention(proj, qt, vt, kmean, bias, batch, seq, kcol_w, heads=MOBA_HEADS_PER_STEP):
    t = MOBA_BLOCK
    nq = seq // t
    dh = MOBA_HEAD_DIM
    gw = heads * dh
    ng = MOBA_HEADS // heads
    kcol = kcol_w // gw
    nb = kmean.shape[1]
    once = pl.Buffered(1)
    return pl.pallas_call(
        functools.partial(_moba_kernel, t=t, heads=heads),
        grid=(batch, ng, nq),
        in_specs=[pl.BlockSpec((gw, t), lambda b, h, i: (h, b * nq + i)),
                  pl.BlockSpec((seq, gw), lambda b, h, i: (b, kcol + h), pipeline_mode=once),
                  pl.BlockSpec((nq, gw, t), lambda b, h, i: (b, h, 0), pipeline_mode=once),
                  pl.BlockSpec((1, nb, gw), lambda b, h, i: (b, 0, h)),
                  pl.BlockSpec((heads, 2, t, t), lambda b, h, i: (h, 0, 0, 0), pipeline_mode=once)],
        out_specs=pl.BlockSpec((t, gw), lambda b, h, i: (b * nq + i, h)),
        out_shape=jax.ShapeDtypeStruct((batch * seq, MOBA_HEADS * dh), BF16),
        scratch_shapes=[pltpu.VMEM((heads, 1, t), F32), pltpu.VMEM((heads, 1, t), F32),
                        pltpu.VMEM((heads, dh, t), F32), pltpu.VMEM((heads, nb, t), F32)],
        compiler_params=_cparams("arbitrary", "arbitrary", "arbitrary"),
        name="moba",
    )(qt, proj, vt, kmean, bias)


def _first_index_of_max(vals, ids, sentinel):
    vmax = jnp.max(vals, axis=0, keepdims=True)
    first = jnp.min(jnp.where(vals == vmax, ids, sentinel), axis=0, keepdims=True)
    return vmax, first


def _route(scores_t, bias_t):
    e, n = scores_t.shape
    per = e // N_GROUPS
    selv = scores_t + bias_t
    sub = lax.broadcasted_iota(jnp.int32, (per, n), 0)
    gscore = []
    for g in range(N_GROUPS):
        blk = selv[g * per:(g + 1) * per, :]
        top1, first = _first_index_of_max(blk, sub, per)
        top2 = jnp.max(jnp.where(sub == first, -jnp.inf, blk), axis=0, keepdims=True)
        gscore.append(top1 + top2)
    gs = jnp.concatenate(gscore, axis=0)
    gid = lax.broadcasted_iota(jnp.int32, gs.shape, 0)
    gsel = jnp.zeros(gs.shape, F32)
    for _ in range(TOPK_GROUPS):
        _, first = _first_index_of_max(gs, gid, N_GROUPS)
        gsel = jnp.where(gid == first, 1.0, gsel)
        gs = jnp.where(gid == first, -jnp.inf, gs)
    masked = jnp.concatenate(
        [jnp.where(gsel[g:g + 1, :] > 0.5, selv[g * per:(g + 1) * per, :], -jnp.inf)
         for g in range(N_GROUPS)], axis=0)
    eid = lax.broadcasted_iota(jnp.int32, masked.shape, 0)
    picked = jnp.zeros(masked.shape, F32)
    for _ in range(TOPK_EXPERTS):
        _, first = _first_index_of_max(masked, eid, e)
        picked = jnp.where(eid == first, 1.0, picked)
        masked = jnp.where(eid == first, -jnp.inf, masked)
    w = jnp.where(picked > 0.5, scores_t, 0.0)
    return w / jnp.sum(w, axis=0, keepdims=True) * ROUTED_SCALE


def _route_kernel(logits_ref, rb_ref, wt_ref):
    wt_ref[...] = _route(jax.nn.sigmoid(logits_ref[...]), rb_ref[...])


def _route_weights(logits_t, rb_t, tn=2048):
    e, t = logits_t.shape
    tn = min(tn, t)
    return pl.pallas_call(
        _route_kernel,
        grid=(t // tn,),
        in_specs=[pl.BlockSpec((e, tn), lambda i: (0, i)),
                  pl.BlockSpec((e, 1), lambda i: (0, 0))],
        out_specs=pl.BlockSpec((e, tn), lambda i: (0, i)),
        out_shape=jax.ShapeDtypeStruct((e, t), F32),
        compiler_params=_cparams("arbitrary"),
        name="route",
    )(logits_t, rb_t)


def _mix_kernel(od_ref, om_ref, gd_ref, gm_ref, x_ref, mod_ref, wod_ref, wom_ref, wout_ref,
                gffn_ref, wrh_ref, wrl_ref, x1_ref, h2_ref, lg_ref, *, parts):
    n = x_ref.shape[0] // parts
    rows = [slice(p * n, (p + 1) * n) for p in range(parts)]
    y = [(_dot(od_ref[r, :], wod_ref[...]), _dot(om_ref[r, :], wom_ref[...])) for r in rows]
    mixed = []
    for r, (yd, ym) in zip(rows, y):
        z = (jax.nn.sigmoid(gd_ref[r, :].astype(F32)) * yd
             + jax.nn.sigmoid(gm_ref[r, :].astype(F32)) * ym)
        mixed.append(_dot(z.astype(BF16), wout_ref[...]))
    for r, mx in zip(rows, mixed):
        x1 = x_ref[r, :] + mod_ref[0, 2:3, :] * mx
        x1_ref[r, :] = x1
        h2 = _modulated_norm(x1, gffn_ref[...], mod_ref[0, 4:5, :], mod_ref[0, 3:4, :])
        h_hi = h2.astype(BF16)
        h2_ref[r, :] = h_hi
        h_lo = (h2 - h_hi.astype(F32)).astype(BF16)
        lg_ref[r, :] = (_dot(h_hi, wrh_ref[...]) + _dot(h_lo, wrh_ref[...])) + _dot(h_hi, wrl_ref[...])


def _mix(od, om, proj, x2, mod3, wod, wom, wout, g_ffn, w_router, seq, gd_col, gm_col, tm=256):
    t, d = x2.shape
    wr_hi = w_router.astype(BF16)
    wr_lo = (w_router - wr_hi.astype(F32)).astype(BF16)
    per_batch = seq // tm
    const = lambda i: (0, 0)
    once = pl.Buffered(1)
    return pl.pallas_call(
        functools.partial(_mix_kernel, parts=2),
        grid=(t // tm,),
        in_specs=[pl.BlockSpec((tm, od.shape[1]), lambda i: (i, 0)),
                  pl.BlockSpec((tm, om.shape[1]), lambda i: (i, 0)),
                  pl.BlockSpec((tm, d), lambda i: (i, gd_col)),
                  pl.BlockSpec((tm, d), lambda i: (i, gm_col)),
                  pl.BlockSpec((tm, d), lambda i: (i, 0)),
                  pl.BlockSpec((1, 6, d), lambda i: (i // per_batch, 0, 0)),
                  pl.BlockSpec(wod.shape, const, pipeline_mode=once),
                  pl.BlockSpec(wom.shape, const, pipeline_mode=once),
                  pl.BlockSpec(wout.shape, const, pipeline_mode=once),
                  pl.BlockSpec((1, d), const),
                  pl.BlockSpec(wr_hi.shape, const),
                  pl.BlockSpec(wr_lo.shape, const)],
        out_specs=[pl.BlockSpec((tm, d), lambda i: (i, 0)),
                   pl.BlockSpec((tm, d), lambda i: (i, 0)),
                   pl.BlockSpec((tm, N_EXPERTS), lambda i: (i, 0))],
        out_shape=[jax.ShapeDtypeStruct((t, d), F32),
                   jax.ShapeDtypeStruct((t, d), BF16),
                   jax.ShapeDtypeStruct((t, N_EXPERTS), F32)],
        compiler_params=_cparams("arbitrary"),
        name="mix",
    )(od, om, proj, proj, x2, mod3, wod, wom, wout, g_ffn, wr_hi, wr_lo)


def _expert_kernel(be_ref, na_ref, x_ref, rw_ref, wg_ref, wu_ref, wd_ref, *rest, base):
    o_ref, wg_s, wu_s, wd_s = rest[-4:]
    i = pl.program_id(0)
    blk = base + i

    @pl.when(blk < na_ref[0])
    def _():
        prev = be_ref[jnp.maximum(blk - 1, 0)]

        @pl.when((i == 0) | (be_ref[blk] != prev))
        def _():
            wg_s[...] = wg_ref[0].astype(BF16)
            wu_s[...] = wu_ref[0].astype(BF16)
            wd_s[...] = wd_ref[0].astype(BF16)

        x = x_ref[...]
        g = _dot(x, wg_s[...])
        u = _dot(x, wu_s[...])
        a = (g * jax.nn.sigmoid(g) * u).astype(BF16)
        y = _dot(a, wd_s[...])
        rows = x.shape[0]
        w_row = rw_ref[0]
        diag = (lax.broadcasted_iota(jnp.int32, (rows, rows), 0)
                == lax.broadcasted_iota(jnp.int32, (rows, rows), 1))
        w_col = jnp.sum(jnp.where(diag, w_row, 0.0), axis=1, keepdims=True)
        o_ref[...] = (y * w_col).astype(BF16)

    @pl.when(blk >= na_ref[0])
    def _():
        o_ref[...] = jnp.zeros(o_ref.shape, o_ref.dtype)


def _experts(block_expert, n_active, x_chunk, row_w, w_gate, w_up, w_down, y_pad, base, n_blocks, rows):
    pc, d = x_chunk.shape
    f = w_gate.shape[2]
    nc = pc // rows

    def local(i, na):
        return jnp.clip(jnp.minimum(base + i, na[0] - 1) - base, 0, nc - 1)

    def xmap(i, be, na):
        return (local(i, na), 0)

    def rmap(i, be, na):
        return (base + local(i, na), 0, 0)

    def wmap(i, be, na):
        return (be[base + local(i, na)], 0, 0)

    in_specs = [pl.BlockSpec((rows, d), xmap),
                pl.BlockSpec((1, 1, rows), rmap),
                pl.BlockSpec((1, d, f), wmap),
                pl.BlockSpec((1, d, f), wmap),
                pl.BlockSpec((1, f, d), wmap)]
    args = [block_expert, n_active, x_chunk, row_w.reshape(n_blocks, 1, rows), w_gate, w_up, w_down]
    aliases = {}
    if y_pad is not None:
        in_specs.append(pl.BlockSpec(memory_space=pl.ANY))
        args.append(y_pad)
        aliases = {len(args) - 1: 0}
    grid_spec = pltpu.PrefetchScalarGridSpec(
        num_scalar_prefetch=2,
        grid=(nc,),
        in_specs=in_specs,
        out_specs=pl.BlockSpec((rows, d), lambda i, be, na: (base + i, 0)),
        scratch_shapes=[pltpu.VMEM((d, f), BF16), pltpu.VMEM((d, f), BF16), pltpu.VMEM((f, d), BF16)],
    )
    return pl.pallas_call(
        functools.partial(_expert_kernel, base=base),
        grid_spec=grid_spec,
        out_shape=jax.ShapeDtypeStruct((n_blocks * rows, d), BF16),
        input_output_aliases=aliases,
        compiler_params=_cparams("arbitrary"),
        name="experts",
    )(*args)


def _final_kernel(x1_ref, h2_ref, y_ref, mod_ref, wsg_ref, wsu_ref, wsd_ref, gf_ref, *rest):
    o_ref = rest[-1]
    h2 = h2_ref[...]
    g = _dot(h2, wsg_ref[...])
    u = _dot(h2, wsu_ref[...])
    shared = _dot((g * jax.nn.sigmoid(g) * u).astype(BF16), wsd_ref[...])
    routed = y_ref[0].astype(F32)
    for k in range(1, y_ref.shape[0]):
        routed = routed + y_ref[k].astype(F32)
    x2 = x1_ref[...] + mod_ref[0, 5:6, :] * (routed + shared)
    ms = jnp.mean(x2 * x2, axis=-1, keepdims=True)
    o_ref[...] = x2 * lax.rsqrt(ms + RMS_EPS) * gf_ref[...]


def _final(x1, h2, y_tok, mod3, wsg, wsu, wsd, g_final, out, base_tok, seq, tm=256):
    t, d = x1.shape
    nk, tc, _ = y_tok.shape
    per_batch = seq // tm
    b0 = base_tok // tm
    const = lambda i: (0, 0)
    row = pl.BlockSpec((tm, d), lambda i: (b0 + i, 0))
    in_specs = [row, row,
                pl.BlockSpec((nk, tm, d), lambda i: (0, i, 0)),
                pl.BlockSpec((1, 6, d), lambda i: ((b0 + i) // per_batch, 0, 0)),
                pl.BlockSpec(wsg.shape, const),
                pl.BlockSpec(wsu.shape, const),
                pl.BlockSpec(wsd.shape, const),
                pl.BlockSpec((1, d), const)]
    args = [x1, h2, y_tok, mod3, wsg, wsu, wsd, g_final]
    aliases = {}
    if out is not None:
        in_specs.append(pl.BlockSpec(memory_space=pl.ANY))
        args.append(out)
        aliases = {len(args) - 1: 0}
    return pl.pallas_call(
        _final_kernel,
        grid=(tc // tm,),
        in_specs=in_specs,
        out_specs=row,
        out_shape=jax.ShapeDtypeStruct((t, d), F32),
        input_output_aliases=aliases,
        compiler_params=_cparams("arbitrary"),
        name="final",
    )(*args)


def _bucket(rel):
    n = jnp.maximum(rel, 0)
    max_exact = REL_BUCKETS // 2
    nf = jnp.maximum(n, 1).astype(F32)
    large = max_exact + (jnp.log(nf / max_exact) / math.log(REL_MAX_DIST / max_exact)
                         * (REL_BUCKETS - max_exact)).astype(jnp.int32)
    large = jnp.minimum(large, REL_BUCKETS - 1)
    return jnp.where(n < max_exact, n, large)


def _bias_kernel(tab_ref, o_ref, *, t):
    h = pl.program_id(0)
    sub = pl.program_id(1)
    key = lax.broadcasted_iota(jnp.int32, (t, t), 0)
    qry = lax.broadcasted_iota(jnp.int32, (t, t), 1)
    rel = qry - key + sub * t
    bucket = _bucket(rel)
    far = tab_ref[h, REL_BUCKETS - 1]
    bias = jnp.zeros((t, t), F32)
    for b in range(REL_BUCKETS - 1):
        bias = jnp.where(bucket == b, tab_ref[h, b] - far, bias)
    o_ref[0, 0] = jnp.where(rel >= 0, bias * LOG2E, NEG_BIG)


def _near_bias(table, t):
    assert t >= REL_MAX_DIST
    heads = table.shape[0]
    return pl.pallas_call(
        functools.partial(_bias_kernel, t=t),
        grid=(heads, 2),
        in_specs=[pl.BlockSpec(memory_space=pltpu.SMEM)],
        out_specs=pl.BlockSpec((1, 1, t, t), lambda h, s: (h, s, 0, 0)),
        out_shape=jax.ShapeDtypeStruct((heads, 2, t, t), F32),
        compiler_params=_cparams("arbitrary", "arbitrary"),
        name="relbias",
    )(table)


def _dispatch(wt, rows):
    e, t = wt.shape
    k = TOPK_EXPERTS
    a = t * k
    w_tok, eidx = lax.top_k(wt.T, k)
    sel = (wt > 0).astype(jnp.int32)
    counts = jnp.sum(sel, axis=1)
    rank = jnp.cumsum(sel, axis=1) - sel
    padded = ((counts + rows - 1) // rows) * rows
    pad_ends = jnp.cumsum(padded)
    pad_starts = pad_ends - padded
    starts = jnp.cumsum(counts) - counts
    pos = pad_starts[eidx] + jnp.take_along_axis(rank.T, eidx, axis=1)
    n_blocks = a // rows + e
    n_active = (pad_ends[-1] // rows).astype(jnp.int32)
    end_blocks = pad_ends // rows
    block_expert = jnp.sum(jnp.arange(n_blocks)[:, None] >= end_blocks[None, :], axis=1)
    block_expert = jnp.minimum(block_expert, e - 1).astype(jnp.int32)
    order = jnp.argsort(eidx.reshape(a))
    tok_sorted = (order // k).astype(jnp.int32)
    p = jnp.arange(n_blocks * rows)
    ep = jnp.repeat(block_expert, rows)
    j = p - pad_starts[ep]
    valid = j < counts[ep]
    src = jnp.clip(starts[ep] + j, 0, a - 1)
    row_tok = jnp.where(valid, tok_sorted[src], p % t)
    row_w = jnp.where(valid, w_tok.reshape(a)[order][src], 0.0)
    return pos, row_tok, row_w, block_expert, n_active.reshape(1)


def kernel(x, c, w_ada, b_ada, g_mix, g_ffn, w_in, diff_lambda, diff_subln_g, rel_bias, w_o_diff, w_o_moba,
           w_out, w_router, router_bias, w_exp_gate, w_exp_up, w_exp_down, w_sh_gate, w_sh_up, w_sh_down,
           g_final):
    batch, seq, d = x.shape
    t = batch * seq
    depth = w_ada.shape[0]
    assert seq % MOBA_BLOCK == 0 and seq % min(DIFF_TILE, seq) == 0
    hw = 2 * DIFF_HEAD_DIM
    qk_w = DIFF_HEADS * hw
    moba_w = MOBA_HEADS * MOBA_HEAD_DIM
    o_qm = 3 * qk_w
    o_km = o_qm + moba_w
    o_vm = o_km + moba_w
    o_gd = o_vm + moba_w
    o_gm = o_gd + d
    table_diff = rel_bias[:, :DIFF_HEADS].T
    table_moba = rel_bias[:, DIFF_HEADS:].T
    bias_diff = _near_bias(table_diff, min(DIFF_TILE, seq))
    bias_moba = _near_bias(table_moba, MOBA_BLOCK)
    colscale = jnp.ones((1, w_in.shape[2]), F32)
    colscale = colscale.at[:, :qk_w].set(DIFF_HEAD_DIM ** -0.5 * LOG2E)
    colscale = colscale.at[:, o_qm:o_km].set(MOBA_HEAD_DIM ** -0.5 * LOG2E)
    c_pad = jnp.zeros((8, d), F32).at[:batch].set(c)

    assert depth == 1, "single-layer block: the final norm is fused into the last kernel"
    l = 0
    xc = x.reshape(t, d)
    mod3 = _ada(c_pad, w_ada[l], b_ada[l][None, :])[:batch].reshape(batch, 6, d)
    proj = _inproj(xc, mod3, g_mix[l][None, :], w_in[l].astype(BF16), colscale, seq, tm=min(1024, seq))
    lam_init = 0.8 - 0.6 * math.exp(-0.3 * l)
    lv = diff_lambda[l].astype(F32)
    lam = (jnp.exp(jnp.sum(lv[0] * lv[1])) - jnp.exp(jnp.sum(lv[2] * lv[3])) + lam_init).reshape(1)
    td = min(DIFF_TILE, seq)
    qd_t = proj[:, :qk_w].T
    vd_t = proj[:, 2 * qk_w:o_qm].reshape(t // td, td, qk_w).transpose(0, 2, 1)
    od = _diff_attention(proj, qd_t, vd_t, lam, bias_diff, diff_subln_g[l][:, None], batch, seq,
                         1.0 - lam_init, td)
    kmean = _kmean(proj, o_km // moba_w, moba_w).reshape(batch, seq // MOBA_BLOCK, moba_w)
    qm_t = proj[:, o_qm:o_km].T
    vm_t = proj[:, o_vm:o_gd].reshape(t // MOBA_BLOCK, MOBA_BLOCK, moba_w).transpose(0, 2, 1)
    om = _moba_attention(proj, qm_t, vm_t, kmean, bias_moba, batch, seq, o_km)
    x1, h2, logits = _mix(od, om, proj, xc, mod3, w_o_diff[l].astype(BF16), w_o_moba[l].astype(BF16),
                          w_out[l].astype(BF16), g_ffn[l][None, :], w_router[l], seq,
                          o_gd // d, o_gm // d)
    wt = _route_weights(logits.T, router_bias[l][:, None])
    pos, row_tok, row_w, block_expert, n_active = _dispatch(wt, EXPERT_ROWS)
    n_blocks = row_tok.shape[0] // EXPERT_ROWS
    assert n_blocks % MOE_CHUNKS == 0 and t % (MOE_CHUNKS * 256) == 0
    cb = n_blocks // MOE_CHUNKS
    y_pad = None
    for ci in range(MOE_CHUNKS):
        x_chunk = h2[row_tok[ci * cb * EXPERT_ROWS:(ci + 1) * cb * EXPERT_ROWS]]
        y_pad = _experts(block_expert, n_active, x_chunk, row_w, w_exp_gate[l], w_exp_up[l], w_exp_down[l],
                         y_pad, ci * cb, n_blocks, EXPERT_ROWS)
    tc = t // MOE_CHUNKS
    wsg, wsu, wsd = w_sh_gate[l].astype(BF16), w_sh_up[l].astype(BF16), w_sh_down[l].astype(BF16)
    out = None
    for ci in range(MOE_CHUNKS):
        idx = pos[ci * tc:(ci + 1) * tc].T.reshape(-1)
        y_tok = y_pad[idx].reshape(TOPK_EXPERTS, tc, d)
        out = _final(x1, h2, y_tok, mod3, wsg, wsu, wsd, g_final[None, :], out, ci * tc, seq)
    return out.reshape(batch, seq, d)
```

```python
import functools
import math

import jax
import jax.numpy as jnp
from jax import lax
from jax.experimental import pallas as pl
from jax.experimental.pallas import tpu as pltpu

F32 = jnp.float32
BF16 = jnp.bfloat16

DIFF_HEADS = 8
DIFF_HEAD_DIM = 64
MOBA_HEADS = 8
MOBA_HEAD_DIM = 128
MOBA_BLOCK = 256
MOBA_TOPK = 3
REL_BUCKETS = 32
REL_MAX_DIST = 128
N_EXPERTS = 64
N_GROUPS = 8
TOPK_GROUPS = 4
TOPK_EXPERTS = 8
ROUTED_SCALE = 2.5
RMS_EPS = 1e-6
LANES = 128
NEG_BIG = -1e30
LOG2E = 1.4426950408889634
DIFF_TILE = 512
DIFF_HEADS_PER_STEP = 4
MOBA_HEADS_PER_STEP = 4
EXPERT_ROWS = 512
MOE_CHUNKS = 4
VMEM_LIMIT = 56 * 1024 * 1024


def _cparams(*sem):
    return pltpu.CompilerParams(dimension_semantics=sem, vmem_limit_bytes=VMEM_LIMIT)


def _dot(a, b):
    return jnp.dot(a, b, preferred_element_type=F32)


def _dot_nt(a, b):
    return lax.dot_general(a, b, (((1,), (1,)), ((), ())), preferred_element_type=F32)


def _ada_kernel(c_ref, w_ref, b_ref, o_ref):
    c = c_ref[...]
    ca = c * jax.nn.sigmoid(c)
    o_ref[...] = jnp.dot(ca, w_ref[...], preferred_element_type=F32,
                         precision=lax.Precision.HIGHEST) + b_ref[...]


def _ada(c_pad, w_ada, b_ada, tn=1024):
    rows, d = c_pad.shape
    n = w_ada.shape[1]
    return pl.pallas_call(
        _ada_kernel,
        grid=(n // tn,),
        in_specs=[pl.BlockSpec((rows, d), lambda j: (0, 0)),
                  pl.BlockSpec((d, tn), lambda j: (0, j)),
                  pl.BlockSpec((1, tn), lambda j: (0, j))],
        out_specs=pl.BlockSpec((rows, tn), lambda j: (0, j)),
        out_shape=jax.ShapeDtypeStruct((rows, n), F32),
        compiler_params=_cparams("arbitrary"),
        name="ada",
    )(c_pad, w_ada, b_ada)


def _modulated_norm(x, g, scale, shift):
    ms = jnp.mean(x * x, axis=-1, keepdims=True)
    return (x * lax.rsqrt(ms + RMS_EPS) * g) * (1.0 + scale) + shift


def _inproj_kernel(x_ref, mod_ref, g_ref, w_ref, cs_ref, o_ref, h_ref, *, chunk):
    @pl.when(pl.program_id(1) == 0)
    def _():
        shift = mod_ref[0, 0:1, :]
        scale = mod_ref[0, 1:2, :]
        g = g_ref[...]

        def body(r, carry):
            rows = pl.ds(pl.multiple_of(r * chunk, chunk), chunk)
            h_ref[rows, :] = _modulated_norm(x_ref[rows, :], g, scale, shift).astype(BF16)
            return carry

        lax.fori_loop(0, x_ref.shape[0] // chunk, body, 0)

    o_ref[...] = (_dot(h_ref[...], w_ref[...]) * cs_ref[...]).astype(BF16)


def _inproj(x2, mod3, g_mix, w_in_bf, colscale, seq, tm=1024, tn=1024):
    t, d = x2.shape
    n = w_in_bf.shape[1]
    per_batch = seq // tm
    return pl.pallas_call(
        functools.partial(_inproj_kernel, chunk=128),
        grid=(t // tm, n // tn),
        in_specs=[pl.BlockSpec((tm, d), lambda i, j: (i, 0)),
                  pl.BlockSpec((1, 6, d), lambda i, j: (i // per_batch, 0, 0)),
                  pl.BlockSpec((1, d), lambda i, j: (0, 0)),
                  pl.BlockSpec((d, tn), lambda i, j: (0, j)),
                  pl.BlockSpec((1, tn), lambda i, j: (0, j))],
        out_specs=pl.BlockSpec((tm, tn), lambda i, j: (i, j)),
        out_shape=jax.ShapeDtypeStruct((t, n), BF16),
        scratch_shapes=[pltpu.VMEM((tm, d), BF16)],
        compiler_params=_cparams("arbitrary", "arbitrary"),
        name="inproj",
    )(x2, mod3, g_mix, w_in_bf, colscale)


def _kmean_kernel(k_ref, o_ref):
    o_ref[0] = jnp.mean(k_ref[...].astype(F32), axis=0, keepdims=True)


def _kmean(proj, col_block, width):
    t = proj.shape[0]
    nblk = t // MOBA_BLOCK
    return pl.pallas_call(
        _kmean_kernel,
        grid=(nblk,),
        in_specs=[pl.BlockSpec((MOBA_BLOCK, width), lambda i: (i, col_block))],
        out_specs=pl.BlockSpec((1, 1, width), lambda i: (i, 0, 0)),
        out_shape=jax.ShapeDtypeStruct((nblk, 1, width), F32),
        compiler_params=_cparams("arbitrary"),
        name="kmean",
    )(proj)


def _online_softmax(logits_fn, vt_fn, m, l, acc, heads, first):
    s_all = [logits_fn(g) for g in range(heads)]
    p_all, a_all = [], []
    for g, s in enumerate(s_all):
        smax = jnp.max(s, axis=0, keepdims=True)
        if first:
            mg = smax
            p = jnp.exp2(s - mg)
            l[g] = jnp.sum(p, axis=0, keepdims=True)
        else:
            m_old = m[g]
            mg = jnp.maximum(m_old, smax)
            a = jnp.exp2(m_old - mg)
            p = jnp.exp2(s - mg)
            l[g] = a * l[g] + jnp.sum(p, axis=0, keepdims=True)
            a_all.append(a)
        m[g] = mg
        p_all.append(p.astype(BF16))
    for g in range(heads):
        pv = _dot(vt_fn(g), p_all[g])
        acc[g] = pv if first else a_all[g] * acc[g] + pv


def _pipelined_sweep(n, logits_fn, update_fn, buf0, buf1, heads):
    def fill(buf, c):
        for g in range(heads):
            buf[g] = logits_fn(g, c)

    @pl.when(n >= 1)
    def _():
        fill(buf0, 0)

    def body(i, carry):
        a = 2 * i
        fill(buf1, a + 1)
        update_fn(a, buf0)
        fill(buf0, jnp.minimum(a + 2, n - 1))
        update_fn(a + 1, buf1)
        return carry

    lax.fori_loop(0, n // 2, body, 0)

    @pl.when(n % 2 == 1)
    def _():
        update_fn(n - 1, buf0)


def _diff_kernel(lam_ref, qt_ref, k_ref, vt_ref, bias_ref, g_ref, o_ref, w12, m, l, acc, s0, s1,
                 *, t, heads, out_scale):
    qi = pl.program_id(2)
    hw = 2 * DIFF_HEAD_DIM
    zeros = jnp.zeros((DIFF_HEAD_DIM, t), BF16)
    for g in range(heads):
        w12[g, :, :t] = jnp.concatenate([qt_ref[g * hw:g * hw + DIFF_HEAD_DIM, :], zeros], axis=0)
        w12[g, :, t:] = jnp.concatenate([zeros, qt_ref[g * hw + DIFF_HEAD_DIM:(g + 1) * hw, :]], axis=0)

    def logits(g, kj):
        rows = pl.ds(pl.multiple_of(kj * t, t), t)
        return _dot(k_ref[rows, g * hw:(g + 1) * hw], w12[g])

    def vt(g, kj):
        return vt_ref[kj, g * hw:(g + 1) * hw, :]

    def both(b):
        return jnp.concatenate([b, b], axis=1)

    def chunk(kj, bias_idx, first):
        def biased(g):
            s = logits(g, kj)
            return s if bias_idx is None else s + both(bias_ref[g, bias_idx])
        _online_softmax(biased, lambda g: vt(g, kj), m, l, acc, heads, first)

    chunk(qi, 0, True)

    @pl.when(qi >= 1)
    def _():
        chunk(qi - 1, 1, False)

    def far_update(kj, buf):
        _online_softmax(lambda g: buf[g], lambda g: vt(g, kj), m, l, acc, heads, False)

    _pipelined_sweep(jnp.maximum(qi - 1, 0), logits, far_update, s0, s1, heads)

    for g in range(heads):
        o12 = acc[g] / l[g]
        o = o12[:, :t] - lam_ref[0] * o12[:, t:]
        ms = jnp.mean(o * o, axis=0, keepdims=True)
        o = (o * lax.rsqrt(ms + RMS_EPS) * g_ref[...]) * out_scale
        o_ref[:, g * hw:(g + 1) * hw] = o.T.astype(BF16)


def _diff_attention(proj, qt, vt, lam, bias, subln_g, batch, seq, out_scale, t, heads=DIFF_HEADS_PER_STEP):
    nq = seq // t
    hw = 2 * DIFF_HEAD_DIM
    gw = heads * hw
    ng = DIFF_HEADS // heads
    kcol = DIFF_HEADS // heads
    once = pl.Buffered(1)
    return pl.pallas_call(
        functools.partial(_diff_kernel, t=t, heads=heads, out_scale=out_scale),
        grid=(batch, ng, nq),
        in_specs=[pl.BlockSpec(memory_space=pltpu.SMEM),
                  pl.BlockSpec((gw, t), lambda b, h, i: (h, b * nq + i)),
                  pl.BlockSpec((seq, gw), lambda b, h, i: (b, kcol + h), pipeline_mode=once),
                  pl.BlockSpec((nq, gw, t), lambda b, h, i: (b, h, 0), pipeline_mode=once),
                  pl.BlockSpec((heads, 2, t, t), lambda b, h, i: (h, 0, 0, 0), pipeline_mode=once),
                  pl.BlockSpec((hw, 1), lambda b, h, i: (0, 0))],
        out_specs=pl.BlockSpec((t, gw), lambda b, h, i: (b * nq + i, h)),
        out_shape=jax.ShapeDtypeStruct((batch * seq, DIFF_HEADS * hw), BF16),
        scratch_shapes=[pltpu.VMEM((heads, hw, 2 * t), BF16),
                        pltpu.VMEM((heads, 1, 2 * t), F32), pltpu.VMEM((heads, 1, 2 * t), F32),
                        pltpu.VMEM((heads, hw, 2 * t), F32),
                        pltpu.VMEM((heads, t, 2 * t), F32), pltpu.VMEM((heads, t, 2 * t), F32)],
        compiler_params=_cparams("arbitrary", "arbitrary", "arbitrary"),
        name="diffattn",
    )(lam, qt, proj, vt, bias, subln_g)


def _moba_kernel(qt_ref, k_ref, vt_ref, km_ref, bias_ref, o_ref, m, l, acc, sel_ref, s0, s1, *, t, heads):
    cur = pl.program_id(2)
    dh = MOBA_HEAD_DIM
    nb = km_ref.shape[1]

    def cols(g):
        return slice(g * dh, (g + 1) * dh)

    for g in range(heads):
        gate = jnp.dot(km_ref[0, :, cols(g)], qt_ref[cols(g), :].astype(F32), preferred_element_type=F32,
                       precision=lax.Precision.HIGHEST)
        blk = lax.broadcasted_iota(jnp.int32, gate.shape, 0)
        gate = jnp.where(blk < cur, gate, -jnp.inf)
        sel = jnp.zeros(gate.shape, F32)
        for _ in range(MOBA_TOPK):
            gmax = jnp.max(gate, axis=0, keepdims=True)
            first = jnp.min(jnp.where(gate == gmax, blk, nb), axis=0, keepdims=True)
            pick = (blk == first) & (gmax > -jnp.inf)
            sel = jnp.where(pick, 1.0, sel)
            gate = jnp.where(blk == first, -jnp.inf, gate)
        sel_ref[g] = sel

    def logits(g, j0, nblk):
        rows = pl.ds(pl.multiple_of(j0 * t, t), nblk * t)
        return _dot(k_ref[rows, cols(g)], qt_ref[cols(g), :])

    def keep(g, j, valid):
        return jnp.where(valid, sel_ref[g, pl.ds(j, 1), :], 0.0) > 0.5

    def pair_vt(g, j0):
        return jnp.concatenate([vt_ref[j0, cols(g), :], vt_ref[j0 + 1, cols(g), :]], axis=1)

    @pl.when(cur == 0)
    def _():
        _online_softmax(lambda g: logits(g, 0, 1) + bias_ref[g, 0], lambda g: vt_ref[0, cols(g), :],
                        m, l, acc, heads, True)

    @pl.when(cur >= 1)
    def _():
        def near(g):
            s = logits(g, cur - 1, 2)
            prev = jnp.where(keep(g, cur - 1, True), s[:t] + bias_ref[g, 1], NEG_BIG)
            return jnp.concatenate([prev, s[t:] + bias_ref[g, 0]], axis=0)
        _online_softmax(near, lambda g: pair_vt(g, cur - 1), m, l, acc, heads, True)

    nfar = jnp.maximum(cur - 1, 0)

    def far_update(c, buf):
        j0 = 2 * c

        def masked(g):
            top = jnp.where(keep(g, j0, True), buf[g, :t, :], NEG_BIG)
            bot = jnp.where(keep(g, j0 + 1, j0 + 1 < nfar), buf[g, t:, :], NEG_BIG)
            return jnp.concatenate([top, bot], axis=0)
        _online_softmax(masked, lambda g: pair_vt(g, j0), m, l, acc, heads, False)

    _pipelined_sweep((nfar + 1) // 2, lambda g, c: logits(g, 2 * c, 2), far_update, s0, s1, heads)
    for g in range(heads):
        o_ref[:, cols(g)] = (acc[g] / l[g]).T.astype(BF16)


def _moba_attention(proj, qt, vt, kmean, bias, batch, seq, kcol_w, heads=MOBA_HEADS_PER_STEP):
    t = MOBA_BLOCK
    nq = seq // t
    dh = MOBA_HEAD_DIM
    gw = heads * dh
    ng = MOBA_HEADS // heads
    kcol = kcol_w // gw
    nb = kmean.shape[1]
    once = pl.Buffered(1)
    return pl.pallas_call(
        functools.partial(_moba_kernel, t=t, heads=heads),
        grid=(batch, ng, nq),
        in_specs=[pl.BlockSpec((gw, t), lambda b, h, i: (h, b * nq + i)),
                  pl.BlockSpec((seq, gw), lambda b, h, i: (b, kcol + h), pipeline_mode=once),
                  pl.BlockSpec((nq, gw, t), lambda b, h, i: (b, h, 0), pipeline_mode=once),
                  pl.BlockSpec((1, nb, gw), lambda b, h, i: (b, 0, h)),
                  pl.BlockSpec((heads, 2, t, t), lambda b, h, i: (h, 0, 0, 0), pipeline_mode=once)],
        out_specs=pl.BlockSpec((t, gw), lambda b, h, i: (b * nq + i, h)),
        out_shape=jax.ShapeDtypeStruct((batch * seq, MOBA_HEADS * dh), BF16),
        scratch_shapes=[pltpu.VMEM((heads, 1, t), F32), pltpu.VMEM((heads, 1, t), F32),
                        pltpu.VMEM((heads, dh, t), F32), pltpu.VMEM((heads, nb, t), F32),
                        pltpu.VMEM((heads, 2 * t, t), F32), pltpu.VMEM((heads, 2 * t, t), F32)],
        compiler_params=_cparams("arbitrary", "arbitrary", "arbitrary"),
        name="moba",
    )(qt, proj, vt, kmean, bias)


def _first_index_of_max(vals, ids, sentinel):
    vmax = jnp.max(vals, axis=0, keepdims=True)
    first = jnp.min(jnp.where(vals == vmax, ids, sentinel), axis=0, keepdims=True)
    return vmax, first


def _route(scores_t, bias_t):
    e, n = scores_t.shape
    per = e // N_GROUPS
    selv = scores_t + bias_t
    sub = lax.broadcasted_iota(jnp.int32, (per, n), 0)
    gscore = []
    for g in range(N_GROUPS):
        blk = selv[g * per:(g + 1) * per, :]
        top1, first = _first_index_of_max(blk, sub, per)
        top2 = jnp.max(jnp.where(sub == first, -jnp.inf, blk), axis=0, keepdims=True)
        gscore.append(top1 + top2)
    gs = jnp.concatenate(gscore, axis=0)
    gid = lax.broadcasted_iota(jnp.int32, gs.shape, 0)
    gsel = jnp.zeros(gs.shape, F32)
    for _ in range(TOPK_GROUPS):
        _, first = _first_index_of_max(gs, gid, N_GROUPS)
        gsel = jnp.where(gid == first, 1.0, gsel)
        gs = jnp.where(gid == first, -jnp.inf, gs)
    masked = jnp.concatenate(
        [jnp.where(gsel[g:g + 1, :] > 0.5, selv[g * per:(g + 1) * per, :], -jnp.inf)
         for g in range(N_GROUPS)], axis=0)
    eid = lax.broadcasted_iota(jnp.int32, masked.shape, 0)
    picked = jnp.zeros(masked.shape, F32)
    for _ in range(TOPK_EXPERTS):
        _, first = _first_index_of_max(masked, eid, e)
        picked = jnp.where(eid == first, 1.0, picked)
        masked = jnp.where(eid == first, -jnp.inf, masked)
    w = jnp.where(picked > 0.5, scores_t, 0.0)
    return w / jnp.sum(w, axis=0, keepdims=True) * ROUTED_SCALE


def _route_kernel(logits_ref, rb_ref, wt_ref):
    wt_ref[...] = _route(jax.nn.sigmoid(logits_ref[...]), rb_ref[...])


def _route_weights(logits_t, rb_t, tn=2048):
    e, t = logits_t.shape
    tn = min(tn, t)
    return pl.pallas_call(
        _route_kernel,
        grid=(t // tn,),
        in_specs=[pl.BlockSpec((e, tn), lambda i: (0, i)),
                  pl.BlockSpec((e, 1), lambda i: (0, 0))],
        out_specs=pl.BlockSpec((e, tn), lambda i: (0, i)),
        out_shape=jax.ShapeDtypeStruct((e, t), F32),
        compiler_params=_cparams("arbitrary"),
        name="route",
    )(logits_t, rb_t)


def _mix_kernel(od_ref, om_ref, gd_ref, gm_ref, x_ref, mod_ref, wod_ref, wom_ref, wout_ref,
                gffn_ref, wrh_ref, wrl_ref, x1_ref, h2_ref, lg_ref, *, parts):
    n = x_ref.shape[0] // parts
    rows = [slice(p * n, (p + 1) * n) for p in range(parts)]
    y = [(_dot(od_ref[r, :], wod_ref[...]), _dot(om_ref[r, :], wom_ref[...])) for r in rows]
    mixed = []
    for r, (yd, ym) in zip(rows, y):
        z = (jax.nn.sigmoid(gd_ref[r, :].astype(F32)) * yd
             + jax.nn.sigmoid(gm_ref[r, :].astype(F32)) * ym)
        mixed.append(_dot(z.astype(BF16), wout_ref[...]))
    for r, mx in zip(rows, mixed):
        x1 = x_ref[r, :] + mod_ref[0, 2:3, :] * mx
        x1_ref[r, :] = x1
        h2 = _modulated_norm(x1, gffn_ref[...], mod_ref[0, 4:5, :], mod_ref[0, 3:4, :])
        h_hi = h2.astype(BF16)
        h2_ref[r, :] = h_hi
        h_lo = (h2 - h_hi.astype(F32)).astype(BF16)
        lg_ref[r, :] = (_dot(h_hi, wrh_ref[...]) + _dot(h_lo, wrh_ref[...])) + _dot(h_hi, wrl_ref[...])


def _mix(od, om, proj, x2, mod3, wod, wom, wout, g_ffn, w_router, seq, gd_col, gm_col, tm=256):
    t, d = x2.shape
    wr_hi = w_router.astype(BF16)
    wr_lo = (w_router - wr_hi.astype(F32)).astype(BF16)
    per_batch = seq // tm
    const = lambda i: (0, 0)
    once = pl.Buffered(1)
    return pl.pallas_call(
        functools.partial(_mix_kernel, parts=2),
        grid=(t // tm,),
        in_specs=[pl.BlockSpec((tm, od.shape[1]), lambda i: (i, 0)),
                  pl.BlockSpec((tm, om.shape[1]), lambda i: (i, 0)),
                  pl.BlockSpec((tm, d), lambda i: (i, gd_col)),
                  pl.BlockSpec((tm, d), lambda i: (i, gm_col)),
                  pl.BlockSpec((tm, d), lambda i: (i, 0)),
                  pl.BlockSpec((1, 6, d), lambda i: (i // per_batch, 0, 0)),
                  pl.BlockSpec(wod.shape, const, pipeline_mode=once),
                  pl.BlockSpec(wom.shape, const, pipeline_mode=once),
                  pl.BlockSpec(wout.shape, const, pipeline_mode=once),
                  pl.BlockSpec((1, d), const),
                  pl.BlockSpec(wr_hi.shape, const),
                  pl.BlockSpec(wr_lo.shape, const)],
        out_specs=[pl.BlockSpec((tm, d), lambda i: (i, 0)),
                   pl.BlockSpec((tm, d), lambda i: (i, 0)),
                   pl.BlockSpec((tm, N_EXPERTS), lambda i: (i, 0))],
        out_shape=[jax.ShapeDtypeStruct((t, d), F32),
                   jax.ShapeDtypeStruct((t, d), BF16),
                   jax.ShapeDtypeStruct((t, N_EXPERTS), F32)],
        compiler_params=_cparams("arbitrary"),
        name="mix",
    )(od, om, proj, proj, x2, mod3, wod, wom, wout, g_ffn, wr_hi, wr_lo)


def _expert_kernel(be_ref, na_ref, x_ref, rw_ref, wg_ref, wu_ref, wd_ref, *rest, base):
    o_ref, wg_s, wu_s, wd_s = rest[-4:]
    i = pl.program_id(0)
    blk = base + i

    @pl.when(blk < na_ref[0])
    def _():
        prev = be_ref[jnp.maximum(blk - 1, 0)]

        @pl.when((i == 0) | (be_ref[blk] != prev))
        def _():
            wg_s[...] = wg_ref[0].astype(BF16)
            wu_s[...] = wu_ref[0].astype(BF16)
            wd_s[...] = wd_ref[0].astype(BF16)

        x = x_ref[...]
        g = _dot(x, wg_s[...])
        u = _dot(x, wu_s[...])
        a = (g * jax.nn.sigmoid(g) * u).astype(BF16)
        y = _dot(a, wd_s[...])
        rows = x.shape[0]
        w_row = rw_ref[0]
        diag = (lax.broadcasted_iota(jnp.int32, (rows, rows), 0)
                == lax.broadcasted_iota(jnp.int32, (rows, rows), 1))
        w_col = jnp.sum(jnp.where(diag, w_row, 0.0), axis=1, keepdims=True)
        o_ref[...] = (y * w_col).astype(BF16)

    @pl.when(blk >= na_ref[0])
    def _():
        o_ref[...] = jnp.zeros(o_ref.shape, o_ref.dtype)


def _experts(block_expert, n_active, x_chunk, row_w, w_gate, w_up, w_down, y_pad, base, n_blocks, rows):
    pc, d = x_chunk.shape
    f = w_gate.shape[2]
    nc = pc // rows

    def local(i, na):
        return jnp.clip(jnp.minimum(base + i, na[0] - 1) - base, 0, nc - 1)

    def xmap(i, be, na):
        return (local(i, na), 0)

    def rmap(i, be, na):
        return (base + local(i, na), 0, 0)

    def wmap(i, be, na):
        return (be[base + local(i, na)], 0, 0)

    in_specs = [pl.BlockSpec((rows, d), xmap),
                pl.BlockSpec((1, 1, rows), rmap),
                pl.BlockSpec((1, d, f), wmap),
                pl.BlockSpec((1, d, f), wmap),
                pl.BlockSpec((1, f, d), wmap)]
    args = [block_expert, n_active, x_chunk, row_w.reshape(n_blocks, 1, rows), w_gate, w_up, w_down]
    aliases = {}
    if y_pad is not None:
        in_specs.append(pl.BlockSpec(memory_space=pl.ANY))
        args.append(y_pad)
        aliases = {len(args) - 1: 0}
    grid_spec = pltpu.PrefetchScalarGridSpec(
        num_scalar_prefetch=2,
        grid=(nc,),
        in_specs=in_specs,
        out_specs=pl.BlockSpec((rows, d), lambda i, be, na: (base + i, 0)),
        scratch_shapes=[pltpu.VMEM((d, f), BF16), pltpu.VMEM((d, f), BF16), pltpu.VMEM((f, d), BF16)],
    )
    return pl.pallas_call(
        functools.partial(_expert_kernel, base=base),
        grid_spec=grid_spec,
        out_shape=jax.ShapeDtypeStruct((n_blocks * rows, d), BF16),
        input_output_aliases=aliases,
        compiler_params=_cparams("arbitrary"),
        name="experts",
    )(*args)


def _final_kernel(x1_ref, h2_ref, y_ref, mod_ref, wsg_ref, wsu_ref, wsd_ref, gf_ref, *rest):
    o_ref = rest[-1]
    h2 = h2_ref[...]
    g = _dot(h2, wsg_ref[...])
    u = _dot(h2, wsu_ref[...])
    shared = _dot((g * jax.nn.sigmoid(g) * u).astype(BF16), wsd_ref[...])
    routed = y_ref[0].astype(F32)
    for k in range(1, y_ref.shape[0]):
        routed = routed + y_ref[k].astype(F32)
    x2 = x1_ref[...] + mod_ref[0, 5:6, :] * (routed + shared)
    ms = jnp.mean(x2 * x2, axis=-1, keepdims=True)
    o_ref[...] = x2 * lax.rsqrt(ms + RMS_EPS) * gf_ref[...]


def _final(x1, h2, y_tok, mod3, wsg, wsu, wsd, g_final, out, base_tok, seq, tm=256):
    t, d = x1.shape
    nk, tc, _ = y_tok.shape
    per_batch = seq // tm
    b0 = base_tok // tm
    const = lambda i: (0, 0)
    row = pl.BlockSpec((tm, d), lambda i: (b0 + i, 0))
    in_specs = [row, row,
                pl.BlockSpec((nk, tm, d), lambda i: (0, i, 0)),
                pl.BlockSpec((1, 6, d), lambda i: ((b0 + i) // per_batch, 0, 0)),
                pl.BlockSpec(wsg.shape, const),
                pl.BlockSpec(wsu.shape, const),
                pl.BlockSpec(wsd.shape, const),
                pl.BlockSpec((1, d), const)]
    args = [x1, h2, y_tok, mod3, wsg, wsu, wsd, g_final]
    aliases = {}
    if out is not None:
        in_specs.append(pl.BlockSpec(memory_space=pl.ANY))
        args.append(out)
        aliases = {len(args) - 1: 0}
    return pl.pallas_call(
        _final_kernel,
        grid=(tc // tm,),
        in_specs=in_specs,
        out_specs=row,
        out_shape=jax.ShapeDtypeStruct((t, d), F32),
        input_output_aliases=aliases,
        compiler_params=_cparams("arbitrary"),
        name="final",
    )(*args)


def _bucket(rel):
    n = jnp.maximum(rel, 0)
    max_exact = REL_BUCKETS // 2
    nf = jnp.maximum(n, 1).astype(F32)
    large = max_exact + (jnp.log(nf / max_exact) / math.log(REL_MAX_DIST / max_exact)
                         * (REL_BUCKETS - max_exact)).astype(jnp.int32)
    large = jnp.minimum(large, REL_BUCKETS - 1)
    return jnp.where(n < max_exact, n, large)


def _bias_kernel(tab_ref, o_ref, *, t):
    h = pl.program_id(0)
    sub = pl.program_id(1)
    key = lax.broadcasted_iota(jnp.int32, (t, t), 0)
    qry = lax.broadcasted_iota(jnp.int32, (t, t), 1)
    rel = qry - key + sub * t
    bucket = _bucket(rel)
    far = tab_ref[h, REL_BUCKETS - 1]
    bias = jnp.zeros((t, t), F32)
    for b in range(REL_BUCKETS - 1):
        bias = jnp.where(bucket == b, tab_ref[h, b] - far, bias)
    o_ref[0, 0] = jnp.where(rel >= 0, bias * LOG2E, NEG_BIG)


def _near_bias(table, t):
    assert t >= REL_MAX_DIST
    heads = table.shape[0]
    return pl.pallas_call(
        functools.partial(_bias_kernel, t=t),
        grid=(heads, 2),
        in_specs=[pl.BlockSpec(memory_space=pltpu.SMEM)],
        out_specs=pl.BlockSpec((1, 1, t, t), lambda h, s: (h, s, 0, 0)),
        out_shape=jax.ShapeDtypeStruct((heads, 2, t, t), F32),
        compiler_params=_cparams("arbitrary", "arbitrary"),
        name="relbias",
    )(table)


def _dispatch(wt, rows):
    e, t = wt.shape
    k = TOPK_EXPERTS
    a = t * k
    w_tok, eidx = lax.top_k(wt.T, k)
    sel = (wt > 0).astype(jnp.int32)
    counts = jnp.sum(sel, axis=1)
    rank = jnp.cumsum(sel, axis=1) - sel
    padded = ((counts + rows - 1) // rows) * rows
    pad_ends = jnp.cumsum(padded)
    pad_starts = pad_ends - padded
    starts = jnp.cumsum(counts) - counts
    pos = pad_starts[eidx] + jnp.take_along_axis(rank.T, eidx, axis=1)
    n_blocks = a // rows + e
    n_active = (pad_ends[-1] // rows).astype(jnp.int32)
    end_blocks = pad_ends // rows
    block_expert = jnp.sum(jnp.arange(n_blocks)[:, None] >= end_blocks[None, :], axis=1)
    block_expert = jnp.minimum(block_expert, e - 1).astype(jnp.int32)
    order = jnp.argsort(eidx.reshape(a))
    tok_sorted = (order // k).astype(jnp.int32)
    p = jnp.arange(n_blocks * rows)
    ep = jnp.repeat(block_expert, rows)
    j = p - pad_starts[ep]
    valid = j < counts[ep]
    src = jnp.clip(starts[ep] + j, 0, a - 1)
    row_tok = jnp.where(valid, tok_sorted[src], p % t)
    row_w = jnp.where(valid, w_tok.reshape(a)[order][src], 0.0)
    return pos, row_tok, row_w, block_expert, n_active.reshape(1)


def kernel(x, c, w_ada, b_ada, g_mix, g_ffn, w_in, diff_lambda, diff_subln_g, rel_bias, w_o_diff, w_o_moba,
           w_out, w_router, router_bias, w_exp_gate, w_exp_up, w_exp_down, w_sh_gate, w_sh_up, w_sh_down,
           g_final):
    batch, seq, d = x.shape
    t = batch * seq
    depth = w_ada.shape[0]
    assert seq % MOBA_BLOCK == 0 and seq % min(DIFF_TILE, seq) == 0
    hw = 2 * DIFF_HEAD_DIM
    qk_w = DIFF_HEADS * hw
    moba_w = MOBA_HEADS * MOBA_HEAD_DIM
    o_qm = 3 * qk_w
    o_km = o_qm + moba_w
    o_vm = o_km + moba_w
    o_gd = o_vm + moba_w
    o_gm = o_gd + d
    table_diff = rel_bias[:, :DIFF_HEADS].T
    table_moba = rel_bias[:, DIFF_HEADS:].T
    bias_diff = _near_bias(table_diff, min(DIFF_TILE, seq))
    bias_moba = _near_bias(table_moba, MOBA_BLOCK)
    colscale = jnp.ones((1, w_in.shape[2]), F32)
    colscale = colscale.at[:, :qk_w].set(DIFF_HEAD_DIM ** -0.5 * LOG2E)
    colscale = colscale.at[:, o_qm:o_km].set(MOBA_HEAD_DIM ** -0.5 * LOG2E)
    c_pad = jnp.zeros((8, d), F32).at[:batch].set(c)

    assert depth == 1, "single-layer block: the final norm is fused into the last kernel"
    l = 0
    xc = x.reshape(t, d)
    mod3 = _ada(c_pad, w_ada[l], b_ada[l][None, :])[:batch].reshape(batch, 6, d)
    proj = _inproj(xc, mod3, g_mix[l][None, :], w_in[l].astype(BF16), colscale, seq, tm=min(1024, seq))
    lam_init = 0.8 - 0.6 * math.exp(-0.3 * l)
    lv = diff_lambda[l].astype(F32)
    lam = (jnp.exp(jnp.sum(lv[0] * lv[1])) - jnp.exp(jnp.sum(lv[2] * lv[3])) + lam_init).reshape(1)
    td = min(DIFF_TILE, seq)
    qd_t = proj[:, :qk_w].T
    vd_t = proj[:, 2 * qk_w:o_qm].reshape(t // td, td, qk_w).transpose(0, 2, 1)
    od = _diff_attention(proj, qd_t, vd_t, lam, bias_diff, diff_subln_g[l][:, None], batch, seq,
                         1.0 - lam_init, td)
    kmean = _kmean(proj, o_km // moba_w, moba_w).reshape(batch, seq // MOBA_BLOCK, moba_w)
    qm_t = proj[:, o_qm:o_km].T
    vm_t = proj[:, o_vm:o_gd].reshape(t // MOBA_BLOCK, MOBA_BLOCK, moba_w).transpose(0, 2, 1)
    om = _moba_attention(proj, qm_t, vm_t, kmean, bias_moba, batch, seq, o_km)
    x1, h2, logits = _mix(od, om, proj, xc, mod3, w_o_diff[l].astype(BF16), w_o_moba[l].astype(BF16),
                          w_out[l].astype(BF16), g_ffn[l][None, :], w_router[l], seq,
                          o_gd // d, o_gm // d)
    wt = _route_weights(logits.T, router_bias[l][:, None])
    pos, row_tok, row_w, block_expert, n_active = _dispatch(wt, EXPERT_ROWS)
    n_blocks = row_tok.shape[0] // EXPERT_ROWS
    assert n_blocks % MOE_CHUNKS == 0 and t % (MOE_CHUNKS * 256) == 0
    cb = n_blocks // MOE_CHUNKS
    y_pad = None
    for ci in range(MOE_CHUNKS):
        x_chunk = h2[row_tok[ci * cb * EXPERT_ROWS:(ci + 1) * cb * EXPERT_ROWS]]
        y_pad = _experts(block_expert, n_active, x_chunk, row_w, w_exp_gate[l], w_exp_up[l], w_exp_down[l],
                         y_pad, ci * cb, n_blocks, EXPERT_ROWS)
    tc = t // MOE_CHUNKS
    wsg, wsu, wsd = w_sh_gate[l].astype(BF16), w_sh_up[l].astype(BF16), w_sh_down[l].astype(BF16)
    out = None
    for ci in range(MOE_CHUNKS):
        idx = pos[ci * tc:(ci + 1) * tc].T.reshape(-1)
        y_tok = y_pad[idx].reshape(TOPK_EXPERTS, tc, d)
        out = _final(x1, h2, y_tok, mod3, wsg, wsu, wsd, g_final[None, :], out, ci * tc, seq)
    return out.reshape(batch, seq, d)
```

```python
import functools
import math

import jax
import jax.numpy as jnp
from jax import lax
from jax.experimental import pallas as pl
from jax.experimental.pallas import tpu as pltpu

F32 = jnp.float32
BF16 = jnp.bfloat16

DIFF_HEADS = 8
DIFF_HEAD_DIM = 64
MOBA_HEADS = 8
MOBA_HEAD_DIM = 128
MOBA_BLOCK = 256
MOBA_TOPK = 3
REL_BUCKETS = 32
REL_MAX_DIST = 128
N_EXPERTS = 64
N_GROUPS = 8
TOPK_GROUPS = 4
TOPK_EXPERTS = 8
ROUTED_SCALE = 2.5
RMS_EPS = 1e-6
LANES = 128
NEG_BIG = -1e30
LOG2E = 1.4426950408889634
DIFF_TILE = 512
DIFF_HEADS_PER_STEP = 4
MOBA_HEADS_PER_STEP = 4
EXPERT_ROWS = 512
MOE_CHUNKS = 4
VMEM_LIMIT = 56 * 1024 * 1024


def _cparams(*sem):
    return pltpu.CompilerParams(dimension_semantics=sem, vmem_limit_bytes=VMEM_LIMIT)


def _dot(a, b):
    return jnp.dot(a, b, preferred_element_type=F32)


def _dot_nt(a, b):
    return lax.dot_general(a, b, (((1,), (1,)), ((), ())), preferred_element_type=F32)


def _ada_kernel(c_ref, w_ref, b_ref, o_ref):
    c = c_ref[...]
    ca = c * jax.nn.sigmoid(c)
    o_ref[...] = jnp.dot(ca, w_ref[...], preferred_element_type=F32,
                         precision=lax.Precision.HIGHEST) + b_ref[...]


def _ada(c_pad, w_ada, b_ada, tn=1024):
    rows, d = c_pad.shape
    n = w_ada.shape[1]
    return pl.pallas_call(
        _ada_kernel,
        grid=(n // tn,),
        in_specs=[pl.BlockSpec((rows, d), lambda j: (0, 0)),
                  pl.BlockSpec((d, tn), lambda j: (0, j)),
                  pl.BlockSpec((1, tn), lambda j: (0, j))],
        out_specs=pl.BlockSpec((rows, tn), lambda j: (0, j)),
        out_shape=jax.ShapeDtypeStruct((rows, n), F32),
        compiler_params=_cparams("arbitrary"),
        name="ada",
    )(c_pad, w_ada, b_ada)


def _modulated_norm(x, g, scale, shift):
    ms = jnp.mean(x * x, axis=-1, keepdims=True)
    return (x * lax.rsqrt(ms + RMS_EPS) * g) * (1.0 + scale) + shift


def _inproj_kernel(x_ref, mod_ref, g_ref, w_ref, cs_ref, o_ref, h_ref, *, chunk):
    @pl.when(pl.program_id(1) == 0)
    def _():
        shift = mod_ref[0, 0:1, :]
        scale = mod_ref[0, 1:2, :]
        g = g_ref[...]

        def body(r, carry):
            rows = pl.ds(pl.multiple_of(r * chunk, chunk), chunk)
            h_ref[rows, :] = _modulated_norm(x_ref[rows, :], g, scale, shift).astype(BF16)
            return carry

        lax.fori_loop(0, x_ref.shape[0] // chunk, body, 0)

    o_ref[...] = (_dot(h_ref[...], w_ref[...]) * cs_ref[...]).astype(BF16)


def _inproj(x2, mod3, g_mix, w_in_bf, colscale, seq, tm=1024, tn=1024):
    t, d = x2.shape
    n = w_in_bf.shape[1]
    per_batch = seq // tm
    return pl.pallas_call(
        functools.partial(_inproj_kernel, chunk=128),
        grid=(t // tm, n // tn),
        in_specs=[pl.BlockSpec((tm, d), lambda i, j: (i, 0)),
                  pl.BlockSpec((1, 6, d), lambda i, j: (i // per_batch, 0, 0)),
                  pl.BlockSpec((1, d), lambda i, j: (0, 0)),
                  pl.BlockSpec((d, tn), lambda i, j: (0, j)),
                  pl.BlockSpec((1, tn), lambda i, j: (0, j))],
        out_specs=pl.BlockSpec((tm, tn), lambda i, j: (i, j)),
        out_shape=jax.ShapeDtypeStruct((t, n), BF16),
        scratch_shapes=[pltpu.VMEM((tm, d), BF16)],
        compiler_params=_cparams("arbitrary", "arbitrary"),
        name="inproj",
    )(x2, mod3, g_mix, w_in_bf, colscale)


def _online_softmax(logits_fn, vt_fn, m, l, acc, heads, first):
    s_all = [logits_fn(g) for g in range(heads)]
    p_all, a_all = [], []
    for g, s in enumerate(s_all):
        smax = jnp.max(s, axis=0, keepdims=True)
        if first:
            mg = smax
            p = jnp.exp2(s - mg)
            l[g] = jnp.sum(p, axis=0, keepdims=True)
        else:
            m_old = m[g]
            mg = jnp.maximum(m_old, smax)
            a = jnp.exp2(m_old - mg)
            p = jnp.exp2(s - mg)
            l[g] = a * l[g] + jnp.sum(p, axis=0, keepdims=True)
            a_all.append(a)
        m[g] = mg
        p_all.append(p.astype(BF16))
    for g in range(heads):
        pv = _dot(vt_fn(g), p_all[g])
        acc[g] = pv if first else a_all[g] * acc[g] + pv


def _pipelined_sweep(n, logits_fn, update_fn, buf0, buf1, heads):
    def fill(buf, c):
        for g in range(heads):
            buf[g] = logits_fn(g, c)

    @pl.when(n >= 1)
    def _():
        fill(buf0, 0)

    def body(i, carry):
        a = 2 * i
        fill(buf1, a + 1)
        update_fn(a, buf0)
        fill(buf0, jnp.minimum(a + 2, n - 1))
        update_fn(a + 1, buf1)
        return carry

    lax.fori_loop(0, n // 2, body, 0)

    @pl.when(n % 2 == 1)
    def _():
        update_fn(n - 1, buf0)


def _diff_kernel(lam_ref, qt_ref, k_ref, vt_ref, bias_ref, g_ref, o_ref, w12, m, l, acc, s0, s1,
                 *, t, heads, out_scale):
    qi = pl.program_id(2)
    hw = 2 * DIFF_HEAD_DIM
    zeros = jnp.zeros((DIFF_HEAD_DIM, t), BF16)
    for g in range(heads):
        w12[g, :, :t] = jnp.concatenate([qt_ref[g * hw:g * hw + DIFF_HEAD_DIM, :], zeros], axis=0)
        w12[g, :, t:] = jnp.concatenate([zeros, qt_ref[g * hw + DIFF_HEAD_DIM:(g + 1) * hw, :]], axis=0)

    def logits(g, kj):
        rows = pl.ds(pl.multiple_of(kj * t, t), t)
        return _dot(k_ref[rows, g * hw:(g + 1) * hw], w12[g])

    def vt(g, kj):
        return vt_ref[kj, g * hw:(g + 1) * hw, :]

    def both(b):
        return jnp.concatenate([b, b], axis=1)

    def chunk(kj, bias_idx, first):
        def biased(g):
            s = logits(g, kj)
            return s if bias_idx is None else s + both(bias_ref[g, bias_idx])
        _online_softmax(biased, lambda g: vt(g, kj), m, l, acc, heads, first)

    chunk(qi, 0, True)

    @pl.when(qi >= 1)
    def _():
        chunk(qi - 1, 1, False)

    def far_update(kj, buf):
        _online_softmax(lambda g: buf[g], lambda g: vt(g, kj), m, l, acc, heads, False)

    _pipelined_sweep(jnp.maximum(qi - 1, 0), logits, far_update, s0, s1, heads)

    for g in range(heads):
        o12 = acc[g] / l[g]
        o = o12[:, :t] - lam_ref[0] * o12[:, t:]
        ms = jnp.mean(o * o, axis=0, keepdims=True)
        o = (o * lax.rsqrt(ms + RMS_EPS) * g_ref[...]) * out_scale
        o_ref[:, g * hw:(g + 1) * hw] = o.T.astype(BF16)


def _diff_attention(proj, qt, vt, lam, bias, subln_g, batch, seq, out_scale, t, heads=DIFF_HEADS_PER_STEP):
    nq = seq // t
    hw = 2 * DIFF_HEAD_DIM
    gw = heads * hw
    ng = DIFF_HEADS // heads
    kcol = DIFF_HEADS // heads
    once = pl.Buffered(1)
    return pl.pallas_call(
        functools.partial(_diff_kernel, t=t, heads=heads, out_scale=out_scale),
        grid=(batch, ng, nq),
        in_specs=[pl.BlockSpec(memory_space=pltpu.SMEM),
                  pl.BlockSpec((gw, t), lambda b, h, i: (h, b * nq + i)),
                  pl.BlockSpec((seq, gw), lambda b, h, i: (b, kcol + h), pipeline_mode=once),
                  pl.BlockSpec((nq, gw, t), lambda b, h, i: (b, h, 0), pipeline_mode=once),
                  pl.BlockSpec((heads, 2, t, t), lambda b, h, i: (h, 0, 0, 0), pipeline_mode=once),
                  pl.BlockSpec((hw, 1), lambda b, h, i: (0, 0))],
        out_specs=pl.BlockSpec((t, gw), lambda b, h, i: (b * nq + i, h)),
        out_shape=jax.ShapeDtypeStruct((batch * seq, DIFF_HEADS * hw), BF16),
        scratch_shapes=[pltpu.VMEM((heads, hw, 2 * t), BF16),
                        pltpu.VMEM((heads, 1, 2 * t), F32), pltpu.VMEM((heads, 1, 2 * t), F32),
                        pltpu.VMEM((heads, hw, 2 * t), F32),
                        pltpu.VMEM((heads, t, 2 * t), F32), pltpu.VMEM((heads, t, 2 * t), F32)],
        compiler_params=_cparams("arbitrary", "arbitrary", "arbitrary"),
        name="diffattn",
    )(lam, qt, proj, vt, bias, subln_g)


def _moba_kernel(qt_ref, k_ref, vt_ref, bias_ref, o_ref, m, l, acc, sel_ref, km, s0, s1, *, t, heads):
    cur = pl.program_id(2)
    dh = MOBA_HEAD_DIM
    nb = km.shape[0]

    def cols(g):
        return slice(g * dh, (g + 1) * dh)

    @pl.when(cur == 0)
    def _():
        def block_mean(j, carry):
            rows = pl.ds(pl.multiple_of(j * t, t), t)
            km[pl.ds(j, 1), :] = jnp.mean(k_ref[rows, :].astype(F32), axis=0, keepdims=True)
            return carry
        lax.fori_loop(0, nb, block_mean, 0)

    for g in range(heads):
        gate = jnp.dot(km[:, cols(g)], qt_ref[cols(g), :].astype(F32), preferred_element_type=F32,
                       precision=lax.Precision.HIGHEST)
        blk = lax.broadcasted_iota(jnp.int32, gate.shape, 0)
        gate = jnp.where(blk < cur, gate, -jnp.inf)
        sel = jnp.zeros(gate.shape, F32)
        for _ in range(MOBA_TOPK):
            gmax = jnp.max(gate, axis=0, keepdims=True)
            first = jnp.min(jnp.where(gate == gmax, blk, nb), axis=0, keepdims=True)
            pick = (blk == first) & (gmax > -jnp.inf)
            sel = jnp.where(pick, 1.0, sel)
            gate = jnp.where(blk == first, -jnp.inf, gate)
        sel_ref[g] = sel

    def logits(g, j0, nblk):
        rows = pl.ds(pl.multiple_of(j0 * t, t), nblk * t)
        return _dot(k_ref[rows, cols(g)], qt_ref[cols(g), :])

    def keep(g, j, valid):
        return jnp.where(valid, sel_ref[g, pl.ds(j, 1), :], 0.0) > 0.5

    def pair_vt(g, j0):
        return jnp.concatenate([vt_ref[j0, cols(g), :], vt_ref[j0 + 1, cols(g), :]], axis=1)

    @pl.when(cur == 0)
    def _():
        _online_softmax(lambda g: logits(g, 0, 1) + bias_ref[g, 0], lambda g: vt_ref[0, cols(g), :],
                        m, l, acc, heads, True)

    @pl.when(cur >= 1)
    def _():
        def near(g):
            s = logits(g, cur - 1, 2)
            prev = jnp.where(keep(g, cur - 1, True), s[:t] + bias_ref[g, 1], NEG_BIG)
            return jnp.concatenate([prev, s[t:] + bias_ref[g, 0]], axis=0)
        _online_softmax(near, lambda g: pair_vt(g, cur - 1), m, l, acc, heads, True)

    nfar = jnp.maximum(cur - 1, 0)

    def far_update(c, buf):
        j0 = 2 * c

        def masked(g):
            top = jnp.where(keep(g, j0, True), buf[g, :t, :], NEG_BIG)
            bot = jnp.where(keep(g, j0 + 1, j0 + 1 < nfar), buf[g, t:, :], NEG_BIG)
            return jnp.concatenate([top, bot], axis=0)
        _online_softmax(masked, lambda g: pair_vt(g, j0), m, l, acc, heads, False)

    _pipelined_sweep((nfar + 1) // 2, lambda g, c: logits(g, 2 * c, 2), far_update, s0, s1, heads)
    for g in range(heads):
        o_ref[:, cols(g)] = (acc[g] / l[g]).T.astype(BF16)


def _moba_attention(proj, qt, vt, bias, batch, seq, kcol_w, heads=MOBA_HEADS_PER_STEP):
    t = MOBA_BLOCK
    nq = seq // t
    dh = MOBA_HEAD_DIM
    gw = heads * dh
    ng = MOBA_HEADS // heads
    kcol = kcol_w // gw
    nb = nq
    once = pl.Buffered(1)
    return pl.pallas_call(
        functools.partial(_moba_kernel, t=t, heads=heads),
        grid=(batch, ng, nq),
        in_specs=[pl.BlockSpec((gw, t), lambda b, h, i: (h, b * nq + i)),
                  pl.BlockSpec((seq, gw), lambda b, h, i: (b, kcol + h), pipeline_mode=once),
                  pl.BlockSpec((nq, gw, t), lambda b, h, i: (b, h, 0), pipeline_mode=once),
                  pl.BlockSpec((heads, 2, t, t), lambda b, h, i: (h, 0, 0, 0), pipeline_mode=once)],
        out_specs=pl.BlockSpec((t, gw), lambda b, h, i: (b * nq + i, h)),
        out_shape=jax.ShapeDtypeStruct((batch * seq, MOBA_HEADS * dh), BF16),
        scratch_shapes=[pltpu.VMEM((heads, 1, t), F32), pltpu.VMEM((heads, 1, t), F32),
                        pltpu.VMEM((heads, dh, t), F32), pltpu.VMEM((heads, nb, t), F32),
                        pltpu.VMEM((nb, gw), F32),
                        pltpu.VMEM((heads, 2 * t, t), F32), pltpu.VMEM((heads, 2 * t, t), F32)],
        compiler_params=_cparams("arbitrary", "arbitrary", "arbitrary"),
        name="moba",
    )(qt, proj, vt, bias)


def _first_index_of_max(vals, ids, sentinel):
    vmax = jnp.max(vals, axis=0, keepdims=True)
    first = jnp.min(jnp.where(vals == vmax, ids, sentinel), axis=0, keepdims=True)
    return vmax, first


def _route(scores_t, bias_t):
    e, n = scores_t.shape
    per = e // N_GROUPS
    selv = scores_t + bias_t
    sub = lax.broadcasted_iota(jnp.int32, (per, n), 0)
    gscore = []
    for g in range(N_GROUPS):
        blk = selv[g * per:(g + 1) * per, :]
        top1, first = _first_index_of_max(blk, sub, per)
        top2 = jnp.max(jnp.where(sub == first, -jnp.inf, blk), axis=0, keepdims=True)
        gscore.append(top1 + top2)
    gs = jnp.concatenate(gscore, axis=0)
    gid = lax.broadcasted_iota(jnp.int32, gs.shape, 0)
    gsel = jnp.zeros(gs.shape, F32)
    for _ in range(TOPK_GROUPS):
        _, first = _first_index_of_max(gs, gid, N_GROUPS)
        gsel = jnp.where(gid == first, 1.0, gsel)
        gs = jnp.where(gid == first, -jnp.inf, gs)
    masked = jnp.concatenate(
        [jnp.where(gsel[g:g + 1, :] > 0.5, selv[g * per:(g + 1) * per, :], -jnp.inf)
         for g in range(N_GROUPS)], axis=0)
    eid = lax.broadcasted_iota(jnp.int32, masked.shape, 0)
    picked = jnp.zeros(masked.shape, jnp.int32)
    ids, vals = [], []
    for _ in range(TOPK_EXPERTS):
        _, first = _first_index_of_max(masked, eid, e)
        hit = eid == first
        picked = jnp.where(hit, 1, picked)
        ids.append(first)
        vals.append(jnp.sum(jnp.where(hit, scores_t, 0.0), axis=0, keepdims=True))
        masked = jnp.where(hit, -jnp.inf, masked)
    w = jnp.concatenate(vals, axis=0)
    w = w / jnp.sum(w, axis=0, keepdims=True) * ROUTED_SCALE
    return picked, jnp.concatenate(ids, axis=0), w


def _route_kernel(logits_ref, rb_ref, sel_ref, idx_ref, w_ref):
    sel_ref[...], idx_ref[...], w_ref[...] = _route(jax.nn.sigmoid(logits_ref[...]), rb_ref[...])


def _route_picks(logits_t, rb_t, tn=2048):
    e, t = logits_t.shape
    tn = min(tn, t)
    k = TOPK_EXPERTS
    return pl.pallas_call(
        _route_kernel,
        grid=(t // tn,),
        in_specs=[pl.BlockSpec((e, tn), lambda i: (0, i)),
                  pl.BlockSpec((e, 1), lambda i: (0, 0))],
        out_specs=[pl.BlockSpec((e, tn), lambda i: (0, i)),
                   pl.BlockSpec((k, tn), lambda i: (0, i)),
                   pl.BlockSpec((k, tn), lambda i: (0, i))],
        out_shape=[jax.ShapeDtypeStruct((e, t), jnp.int32),
                   jax.ShapeDtypeStruct((k, t), jnp.int32),
                   jax.ShapeDtypeStruct((k, t), F32)],
        compiler_params=_cparams("arbitrary"),
        name="route",
    )(logits_t, rb_t)


def _mix_kernel(od_ref, om_ref, gd_ref, gm_ref, x_ref, mod_ref, wod_ref, wom_ref, wout_ref,
                gffn_ref, wrh_ref, wrl_ref, x1_ref, h2_ref, lg_ref, *, parts):
    n = x_ref.shape[0] // parts
    rows = [slice(p * n, (p + 1) * n) for p in range(parts)]
    y = [(_dot(od_ref[r, :], wod_ref[...]), _dot(om_ref[r, :], wom_ref[...])) for r in rows]
    mixed = []
    for r, (yd, ym) in zip(rows, y):
        z = (jax.nn.sigmoid(gd_ref[r, :].astype(F32)) * yd
             + jax.nn.sigmoid(gm_ref[r, :].astype(F32)) * ym)
        mixed.append(_dot(z.astype(BF16), wout_ref[...]))
    for r, mx in zip(rows, mixed):
        x1 = x_ref[r, :] + mod_ref[0, 2:3, :] * mx
        x1_ref[r, :] = x1
        h2 = _modulated_norm(x1, gffn_ref[...], mod_ref[0, 4:5, :], mod_ref[0, 3:4, :])
        h_hi = h2.astype(BF16)
        h2_ref[r, :] = h_hi
        h_lo = (h2 - h_hi.astype(F32)).astype(BF16)
        lg_ref[r, :] = (_dot(h_hi, wrh_ref[...]) + _dot(h_lo, wrh_ref[...])) + _dot(h_hi, wrl_ref[...])


def _mix(od, om, proj, x2, mod3, wod, wom, wout, g_ffn, w_router, seq, gd_col, gm_col, tm=256):
    t, d = x2.shape
    wr_hi = w_router.astype(BF16)
    wr_lo = (w_router - wr_hi.astype(F32)).astype(BF16)
    per_batch = seq // tm
    const = lambda i: (0, 0)
    once = pl.Buffered(1)
    return pl.pallas_call(
        functools.partial(_mix_kernel, parts=2),
        grid=(t // tm,),
        in_specs=[pl.BlockSpec((tm, od.shape[1]), lambda i: (i, 0)),
                  pl.BlockSpec((tm, om.shape[1]), lambda i: (i, 0)),
                  pl.BlockSpec((tm, d), lambda i: (i, gd_col)),
                  pl.BlockSpec((tm, d), lambda i: (i, gm_col)),
                  pl.BlockSpec((tm, d), lambda i: (i, 0)),
                  pl.BlockSpec((1, 6, d), lambda i: (i // per_batch, 0, 0)),
                  pl.BlockSpec(wod.shape, const, pipeline_mode=once),
                  pl.BlockSpec(wom.shape, const, pipeline_mode=once),
                  pl.BlockSpec(wout.shape, const, pipeline_mode=once),
                  pl.BlockSpec((1, d), const),
                  pl.BlockSpec(wr_hi.shape, const),
                  pl.BlockSpec(wr_lo.shape, const)],
        out_specs=[pl.BlockSpec((tm, d), lambda i: (i, 0)),
                   pl.BlockSpec((tm, d), lambda i: (i, 0)),
                   pl.BlockSpec((tm, N_EXPERTS), lambda i: (i, 0))],
        out_shape=[jax.ShapeDtypeStruct((t, d), F32),
                   jax.ShapeDtypeStruct((t, d), BF16),
                   jax.ShapeDtypeStruct((t, N_EXPERTS), F32)],
        compiler_params=_cparams("arbitrary"),
        name="mix",
    )(od, om, proj, proj, x2, mod3, wod, wom, wout, g_ffn, wr_hi, wr_lo)


def _expert_kernel(be_ref, na_ref, x_ref, rw_ref, wg_ref, wu_ref, wd_ref, *rest, base):
    o_ref, wg_s, wu_s, wd_s = rest[-4:]
    i = pl.program_id(0)
    blk = base + i

    @pl.when(blk < na_ref[0])
    def _():
        prev = be_ref[jnp.maximum(blk - 1, 0)]

        @pl.when((i == 0) | (be_ref[blk] != prev))
        def _():
            wg_s[...] = wg_ref[0].astype(BF16)
            wu_s[...] = wu_ref[0].astype(BF16)
            wd_s[...] = wd_ref[0].astype(BF16)

        x = x_ref[...]
        g = _dot(x, wg_s[...])
        u = _dot(x, wu_s[...])
        a = (g * jax.nn.sigmoid(g) * u).astype(BF16)
        y = _dot(a, wd_s[...])
        rows = x.shape[0]
        w_row = rw_ref[0]
        diag = (lax.broadcasted_iota(jnp.int32, (rows, rows), 0)
                == lax.broadcasted_iota(jnp.int32, (rows, rows), 1))
        w_col = jnp.sum(jnp.where(diag, w_row, 0.0), axis=1, keepdims=True)
        o_ref[...] = (y * w_col).astype(BF16)

    @pl.when(blk >= na_ref[0])
    def _():
        o_ref[...] = jnp.zeros(o_ref.shape, o_ref.dtype)


def _experts(block_expert, n_active, x_chunk, row_w, w_gate, w_up, w_down, y_pad, base, n_blocks, rows):
    pc, d = x_chunk.shape
    f = w_gate.shape[2]
    nc = pc // rows

    def local(i, na):
        return jnp.clip(jnp.minimum(base + i, na[0] - 1) - base, 0, nc - 1)

    def xmap(i, be, na):
        return (local(i, na), 0)

    def rmap(i, be, na):
        return (base + local(i, na), 0, 0)

    def wmap(i, be, na):
        return (be[base + local(i, na)], 0, 0)

    in_specs = [pl.BlockSpec((rows, d), xmap),
                pl.BlockSpec((1, 1, rows), rmap),
                pl.BlockSpec((1, d, f), wmap),
                pl.BlockSpec((1, d, f), wmap),
                pl.BlockSpec((1, f, d), wmap)]
    args = [block_expert, n_active, x_chunk, row_w.reshape(n_blocks, 1, rows), w_gate, w_up, w_down]
    aliases = {}
    if y_pad is not None:
        in_specs.append(pl.BlockSpec(memory_space=pl.ANY))
        args.append(y_pad)
        aliases = {len(args) - 1: 0}
    grid_spec = pltpu.PrefetchScalarGridSpec(
        num_scalar_prefetch=2,
        grid=(nc,),
        in_specs=in_specs,
        out_specs=pl.BlockSpec((rows, d), lambda i, be, na: (base + i, 0)),
        scratch_shapes=[pltpu.VMEM((d, f), BF16), pltpu.VMEM((d, f), BF16), pltpu.VMEM((f, d), BF16)],
    )
    return pl.pallas_call(
        functools.partial(_expert_kernel, base=base),
        grid_spec=grid_spec,
        out_shape=jax.ShapeDtypeStruct((n_blocks * rows, d), BF16),
        input_output_aliases=aliases,
        compiler_params=_cparams("arbitrary"),
        name="experts",
    )(*args)


def _final_kernel(x1_ref, h2_ref, y_ref, mod_ref, wsg_ref, wsu_ref, wsd_ref, gf_ref, *rest):
    o_ref = rest[-1]
    h2 = h2_ref[...]
    g = _dot(h2, wsg_ref[...])
    u = _dot(h2, wsu_ref[...])
    shared = _dot((g * jax.nn.sigmoid(g) * u).astype(BF16), wsd_ref[...])
    routed = y_ref[0].astype(F32)
    for k in range(1, y_ref.shape[0]):
        routed = routed + y_ref[k].astype(F32)
    x2 = x1_ref[...] + mod_ref[0, 5:6, :] * (routed + shared)
    ms = jnp.mean(x2 * x2, axis=-1, keepdims=True)
    o_ref[...] = x2 * lax.rsqrt(ms + RMS_EPS) * gf_ref[...]


def _final(x1, h2, y_tok, mod3, wsg, wsu, wsd, g_final, out, base_tok, seq, tm=256):
    t, d = x1.shape
    nk, tc, _ = y_tok.shape
    per_batch = seq // tm
    b0 = base_tok // tm
    const = lambda i: (0, 0)
    row = pl.BlockSpec((tm, d), lambda i: (b0 + i, 0))
    in_specs = [row, row,
                pl.BlockSpec((nk, tm, d), lambda i: (0, i, 0)),
                pl.BlockSpec((1, 6, d), lambda i: ((b0 + i) // per_batch, 0, 0)),
                pl.BlockSpec(wsg.shape, const),
                pl.BlockSpec(wsu.shape, const),
                pl.BlockSpec(wsd.shape, const),
                pl.BlockSpec((1, d), const)]
    args = [x1, h2, y_tok, mod3, wsg, wsu, wsd, g_final]
    aliases = {}
    if out is not None:
        in_specs.append(pl.BlockSpec(memory_space=pl.ANY))
        args.append(out)
        aliases = {len(args) - 1: 0}
    return pl.pallas_call(
        _final_kernel,
        grid=(tc // tm,),
        in_specs=in_specs,
        out_specs=row,
        out_shape=jax.ShapeDtypeStruct((t, d), F32),
        input_output_aliases=aliases,
        compiler_params=_cparams("arbitrary"),
        name="final",
    )(*args)


def _bucket(rel):
    n = jnp.maximum(rel, 0)
    max_exact = REL_BUCKETS // 2
    nf = jnp.maximum(n, 1).astype(F32)
    large = max_exact + (jnp.log(nf / max_exact) / math.log(REL_MAX_DIST / max_exact)
                         * (REL_BUCKETS - max_exact)).astype(jnp.int32)
    large = jnp.minimum(large, REL_BUCKETS - 1)
    return jnp.where(n < max_exact, n, large)


def _bias_kernel(tab_ref, o_ref, *, t):
    h = pl.program_id(0)
    sub = pl.program_id(1)
    key = lax.broadcasted_iota(jnp.int32, (t, t), 0)
    qry = lax.broadcasted_iota(jnp.int32, (t, t), 1)
    rel = qry - key + sub * t
    bucket = _bucket(rel)
    far = tab_ref[h, REL_BUCKETS - 1]
    bias = jnp.zeros((t, t), F32)
    for b in range(REL_BUCKETS - 1):
        bias = jnp.where(bucket == b, tab_ref[h, b] - far, bias)
    o_ref[0, 0] = jnp.where(rel >= 0, bias * LOG2E, NEG_BIG)


def _near_bias(table, t):
    assert t >= REL_MAX_DIST
    heads = table.shape[0]
    return pl.pallas_call(
        functools.partial(_bias_kernel, t=t),
        grid=(heads, 2),
        in_specs=[pl.BlockSpec(memory_space=pltpu.SMEM)],
        out_specs=pl.BlockSpec((1, 1, t, t), lambda h, s: (h, s, 0, 0)),
        out_shape=jax.ShapeDtypeStruct((heads, 2, t, t), F32),
        compiler_params=_cparams("arbitrary", "arbitrary"),
        name="relbias",
    )(table)


def _dispatch(sel, eidx_t, w_t, rows):
    e, t = sel.shape
    k = eidx_t.shape[0]
    a = t * k
    eidx = eidx_t.T
    counts = jnp.sum(sel, axis=1)
    rank = jnp.cumsum(sel, axis=1) - sel
    padded = ((counts + rows - 1) // rows) * rows
    pad_ends = jnp.cumsum(padded)
    pad_starts = pad_ends - padded
    starts = jnp.cumsum(counts) - counts
    pos = pad_starts[eidx] + jnp.take_along_axis(rank.T, eidx, axis=1)
    n_blocks = a // rows + e
    n_active = (pad_ends[-1] // rows).astype(jnp.int32)
    end_blocks = pad_ends // rows
    block_expert = jnp.sum(jnp.arange(n_blocks)[:, None] >= end_blocks[None, :], axis=1)
    block_expert = jnp.minimum(block_expert, e - 1).astype(jnp.int32)
    order = jnp.argsort(eidx.reshape(a))
    tok_sorted = (order // k).astype(jnp.int32)
    w_sorted = w_t.T.reshape(a)[order]
    j = (jnp.arange(n_blocks) * rows - pad_starts[block_expert])[:, None] + jnp.arange(rows)[None, :]
    valid = (j < counts[block_expert][:, None]).reshape(-1)
    src = jnp.clip(starts[block_expert][:, None] + j, 0, a - 1).reshape(-1)
    row_tok = jnp.where(valid, tok_sorted[src], jnp.arange(n_blocks * rows) % t)
    row_w = jnp.where(valid, w_sorted[src], 0.0)
    return pos, row_tok, row_w, block_expert, n_active.reshape(1)


def kernel(x, c, w_ada, b_ada, g_mix, g_ffn, w_in, diff_lambda, diff_subln_g, rel_bias, w_o_diff, w_o_moba,
           w_out, w_router, router_bias, w_exp_gate, w_exp_up, w_exp_down, w_sh_gate, w_sh_up, w_sh_down,
           g_final):
    batch, seq, d = x.shape
    t = batch * seq
    depth = w_ada.shape[0]
    assert seq % MOBA_BLOCK == 0 and seq % min(DIFF_TILE, seq) == 0
    hw = 2 * DIFF_HEAD_DIM
    qk_w = DIFF_HEADS * hw
    moba_w = MOBA_HEADS * MOBA_HEAD_DIM
    o_qm = 3 * qk_w
    o_km = o_qm + moba_w
    o_vm = o_km + moba_w
    o_gd = o_vm + moba_w
    o_gm = o_gd + d
    table_diff = rel_bias[:, :DIFF_HEADS].T
    table_moba = rel_bias[:, DIFF_HEADS:].T
    bias_diff = _near_bias(table_diff, min(DIFF_TILE, seq))
    bias_moba = _near_bias(table_moba, MOBA_BLOCK)
    colscale = jnp.ones((1, w_in.shape[2]), F32)
    colscale = colscale.at[:, :qk_w].set(DIFF_HEAD_DIM ** -0.5 * LOG2E)
    colscale = colscale.at[:, o_qm:o_km].set(MOBA_HEAD_DIM ** -0.5 * LOG2E)
    c_pad = jnp.zeros((8, d), F32).at[:batch].set(c)

    assert depth == 1, "single-layer block: the final norm is fused into the last kernel"
    l = 0
    xc = x.reshape(t, d)
    mod3 = _ada(c_pad, w_ada[l], b_ada[l][None, :])[:batch].reshape(batch, 6, d)
    proj = _inproj(xc, mod3, g_mix[l][None, :], w_in[l].astype(BF16), colscale, seq, tm=min(1024, seq))
    lam_init = 0.8 - 0.6 * math.exp(-0.3 * l)
    lv = diff_lambda[l].astype(F32)
    lam = (jnp.exp(jnp.sum(lv[0] * lv[1])) - jnp.exp(jnp.sum(lv[2] * lv[3])) + lam_init).reshape(1)
    td = min(DIFF_TILE, seq)
    qd_t = proj[:, :qk_w].T
    vd_t = proj[:, 2 * qk_w:o_qm].reshape(t // td, td, qk_w).transpose(0, 2, 1)
    od = _diff_attention(proj, qd_t, vd_t, lam, bias_diff, diff_subln_g[l][:, None], batch, seq,
                         1.0 - lam_init, td)
    qm_t = proj[:, o_qm:o_km].T
    vm_t = proj[:, o_vm:o_gd].reshape(t // MOBA_BLOCK, MOBA_BLOCK, moba_w).transpose(0, 2, 1)
    om = _moba_attention(proj, qm_t, vm_t, bias_moba, batch, seq, o_km)
    x1, h2, logits = _mix(od, om, proj, xc, mod3, w_o_diff[l].astype(BF16), w_o_moba[l].astype(BF16),
                          w_out[l].astype(BF16), g_ffn[l][None, :], w_router[l], seq,
                          o_gd // d, o_gm // d)
    sel, eidx_t, w_t = _route_picks(logits.T, router_bias[l][:, None])
    pos, row_tok, row_w, block_expert, n_active = _dispatch(sel, eidx_t, w_t, EXPERT_ROWS)
    n_blocks = row_tok.shape[0] // EXPERT_ROWS
    assert n_blocks % MOE_CHUNKS == 0 and t % (MOE_CHUNKS * 256) == 0
    cb = n_blocks // MOE_CHUNKS
    y_pad = None
    for ci in range(MOE_CHUNKS):
        x_chunk = h2[row_tok[ci * cb * EXPERT_ROWS:(ci + 1) * cb * EXPERT_ROWS]]
        y_pad = _experts(block_expert, n_active, x_chunk, row_w, w_exp_gate[l], w_exp_up[l], w_exp_down[l],
                         y_pad, ci * cb, n_blocks, EXPERT_ROWS)
    tc = t // MOE_CHUNKS
    wsg, wsu, wsd = w_sh_gate[l].astype(BF16), w_sh_up[l].astype(BF16), w_sh_down[l].astype(BF16)
    out = None
    for ci in range(MOE_CHUNKS):
        idx = pos[ci * tc:(ci + 1) * tc].T.reshape(-1)
        y_tok = y_pad[idx].reshape(TOPK_EXPERTS, tc, d)
        out = _final(x1, h2, y_tok, mod3, wsg, wsu, wsd, g_final[None, :], out, ci * tc, seq)
    return out.reshape(batch, seq, d)
```

```python
import functools
import math

import jax
import jax.numpy as jnp
from jax import lax
from jax.experimental import pallas as pl
from jax.experimental.pallas import tpu as pltpu

F32 = jnp.float32
BF16 = jnp.bfloat16

DIFF_HEADS = 8
DIFF_HEAD_DIM = 64
MOBA_HEADS = 8
MOBA_HEAD_DIM = 128
MOBA_BLOCK = 256
MOBA_TOPK = 3
REL_BUCKETS = 32
REL_MAX_DIST = 128
N_EXPERTS = 64
N_GROUPS = 8
TOPK_GROUPS = 4
TOPK_EXPERTS = 8
ROUTED_SCALE = 2.5
RMS_EPS = 1e-6
LANES = 128
NEG_BIG = -1e30
LOG2E = 1.4426950408889634
DIFF_TILE = 512
DIFF_HEADS_PER_STEP = 4
MOBA_HEADS_PER_STEP = 4
DENOM_ROWS = 16
EXPERT_ROWS = 512
MOE_CHUNK_WEIGHTS = (1, 2, 3, 4)
FINAL_ROWS = 256
VMEM_LIMIT = 56 * 1024 * 1024


def _cparams(*sem):
    return pltpu.CompilerParams(dimension_semantics=sem, vmem_limit_bytes=VMEM_LIMIT)


def _dot(a, b):
    return jnp.dot(a, b, preferred_element_type=F32)


def _dot_nt(a, b):
    return lax.dot_general(a, b, (((1,), (1,)), ((), ())), preferred_element_type=F32)


def _ada_kernel(c_ref, w_ref, b_ref, o_ref):
    c = c_ref[...]
    ca = c * jax.nn.sigmoid(c)
    o_ref[...] = jnp.dot(ca, w_ref[...], preferred_element_type=F32,
                         precision=lax.Precision.HIGHEST) + b_ref[...]


def _ada(c_pad, w_ada, b_ada, tn=1024):
    rows, d = c_pad.shape
    n = w_ada.shape[1]
    return pl.pallas_call(
        _ada_kernel,
        grid=(n // tn,),
        in_specs=[pl.BlockSpec((rows, d), lambda j: (0, 0)),
                  pl.BlockSpec((d, tn), lambda j: (0, j)),
                  pl.BlockSpec((1, tn), lambda j: (0, j))],
        out_specs=pl.BlockSpec((rows, tn), lambda j: (0, j)),
        out_shape=jax.ShapeDtypeStruct((rows, n), F32),
        compiler_params=_cparams("arbitrary"),
        name="ada",
    )(c_pad, w_ada, b_ada)


def _modulated_norm(x, g, scale, shift):
    ms = jnp.mean(x * x, axis=-1, keepdims=True)
    return (x * lax.rsqrt(ms + RMS_EPS) * g) * (1.0 + scale) + shift


def _inproj_kernel(x_ref, mod_ref, g_ref, w_ref, cs_ref, o_ref, h_ref, *, chunk):
    @pl.when(pl.program_id(1) == 0)
    def _():
        shift = mod_ref[0, 0:1, :]
        scale = mod_ref[0, 1:2, :]
        g = g_ref[...]

        def body(r, carry):
            rows = pl.ds(pl.multiple_of(r * chunk, chunk), chunk)
            h_ref[rows, :] = _modulated_norm(x_ref[rows, :], g, scale, shift).astype(BF16)
            return carry

        lax.fori_loop(0, x_ref.shape[0] // chunk, body, 0)

    o_ref[...] = (_dot(h_ref[...], w_ref[...]) * cs_ref[...]).astype(BF16)


def _inproj(x2, mod3, g_mix, w_in_bf, colscale, seq, tm=1024, tn=1024):
    t, d = x2.shape
    n = w_in_bf.shape[1]
    per_batch = seq // tm
    return pl.pallas_call(
        functools.partial(_inproj_kernel, chunk=128),
        grid=(t // tm, n // tn),
        in_specs=[pl.BlockSpec((tm, d), lambda i, j: (i, 0)),
                  pl.BlockSpec((1, 6, d), lambda i, j: (i // per_batch, 0, 0)),
                  pl.BlockSpec((1, d), lambda i, j: (0, 0)),
                  pl.BlockSpec((d, tn), lambda i, j: (0, j)),
                  pl.BlockSpec((1, tn), lambda i, j: (0, j))],
        out_specs=pl.BlockSpec((tm, tn), lambda i, j: (i, j)),
        out_shape=jax.ShapeDtypeStruct((t, n), BF16),
        scratch_shapes=[pltpu.VMEM((tm, d), BF16)],
        compiler_params=_cparams("arbitrary", "arbitrary"),
        name="inproj",
    )(x2, mod3, g_mix, w_in_bf, colscale)


def _online_softmax(logits_fn, vt_fn, m, l, acc, heads, first):
    s_all = [logits_fn(g) for g in range(heads)]
    p_all, a_all = [], []
    for g, s in enumerate(s_all):
        smax = jnp.max(s, axis=0, keepdims=True)
        if first:
            mg = smax
            p = jnp.exp2(s - mg)
            if l is not None:
                l[g] = jnp.sum(p, axis=0, keepdims=True)
        else:
            m_old = m[g]
            mg = jnp.maximum(m_old, smax)
            a = jnp.exp2(m_old - mg)
            p = jnp.exp2(s - mg)
            if l is not None:
                l[g] = a * l[g] + jnp.sum(p, axis=0, keepdims=True)
            a_all.append(a)
        m[g] = mg
        p_all.append(p.astype(BF16))
    for g in range(heads):
        vt = vt_fn(g)
        if l is None:
            vt = jnp.concatenate([vt, jnp.ones((DENOM_ROWS, vt.shape[1]), vt.dtype)], axis=0)
        pv = _dot(vt, p_all[g])
        acc[g] = pv if first else a_all[g] * acc[g] + pv


def _pipelined_sweep(n, logits_fn, update_fn, buf0, buf1, heads):
    def fill(buf, c):
        for g in range(heads):
            buf[g] = logits_fn(g, c)

    @pl.when(n >= 1)
    def _():
        fill(buf0, 0)

    def body(i, carry):
        a = 2 * i
        fill(buf1, a + 1)
        update_fn(a, buf0)
        fill(buf0, jnp.minimum(a + 2, n - 1))
        update_fn(a + 1, buf1)
        return carry

    lax.fori_loop(0, n // 2, body, 0)

    @pl.when(n % 2 == 1)
    def _():
        update_fn(n - 1, buf0)


def _diff_kernel(lam_ref, qt_ref, k_ref, vt_ref, bias_ref, g_ref, o_ref, w12, m, l, acc, s0, s1,
                 *, t, heads, out_scale):
    qi = pl.program_id(2)
    hw = 2 * DIFF_HEAD_DIM
    zeros = jnp.zeros((DIFF_HEAD_DIM, t), BF16)
    for g in range(heads):
        w12[g, :, :t] = jnp.concatenate([qt_ref[g * hw:g * hw + DIFF_HEAD_DIM, :], zeros], axis=0)
        w12[g, :, t:] = jnp.concatenate([zeros, qt_ref[g * hw + DIFF_HEAD_DIM:(g + 1) * hw, :]], axis=0)

    def logits(g, kj):
        rows = pl.ds(pl.multiple_of(kj * t, t), t)
        return _dot(k_ref[rows, g * hw:(g + 1) * hw], w12[g])

    def vt(g, kj):
        return vt_ref[kj, g * hw:(g + 1) * hw, :]

    def both(b):
        return jnp.concatenate([b, b], axis=1)

    def chunk(kj, bias_idx, first):
        def biased(g):
            s = logits(g, kj)
            return s if bias_idx is None else s + both(bias_ref[g, bias_idx])
        _online_softmax(biased, lambda g: vt(g, kj), m, l, acc, heads, first)

    chunk(qi, 0, True)

    @pl.when(qi >= 1)
    def _():
        chunk(qi - 1, 1, False)

    def far_update(kj, buf):
        _online_softmax(lambda g: buf[g], lambda g: vt(g, kj), m, l, acc, heads, False)

    _pipelined_sweep(jnp.maximum(qi - 1, 0), logits, far_update, s0, s1, heads)

    for g in range(heads):
        o12 = acc[g] / l[g]
        o = o12[:, :t] - lam_ref[0] * o12[:, t:]
        ms = jnp.mean(o * o, axis=0, keepdims=True)
        o = (o * lax.rsqrt(ms + RMS_EPS) * g_ref[...]) * out_scale
        o_ref[:, g * hw:(g + 1) * hw] = o.T.astype(BF16)


def _diff_attention(proj, qt, vt, lam, bias, subln_g, batch, seq, out_scale, t, heads=DIFF_HEADS_PER_STEP):
    nq = seq // t
    hw = 2 * DIFF_HEAD_DIM
    gw = heads * hw
    ng = DIFF_HEADS // heads
    kcol = DIFF_HEADS // heads
    once = pl.Buffered(1)
    return pl.pallas_call(
        functools.partial(_diff_kernel, t=t, heads=heads, out_scale=out_scale),
        grid=(batch, ng, nq),
        in_specs=[pl.BlockSpec(memory_space=pltpu.SMEM),
                  pl.BlockSpec((gw, t), lambda b, h, i: (h, b * nq + i)),
                  pl.BlockSpec((seq, gw), lambda b, h, i: (b, kcol + h), pipeline_mode=once),
                  pl.BlockSpec((nq, gw, t), lambda b, h, i: (b, h, 0), pipeline_mode=once),
                  pl.BlockSpec((heads, 2, t, t), lambda b, h, i: (h, 0, 0, 0), pipeline_mode=once),
                  pl.BlockSpec((hw, 1), lambda b, h, i: (0, 0))],
        out_specs=pl.BlockSpec((t, gw), lambda b, h, i: (b * nq + i, h)),
        out_shape=jax.ShapeDtypeStruct((batch * seq, DIFF_HEADS * hw), BF16),
        scratch_shapes=[pltpu.VMEM((heads, hw, 2 * t), BF16),
                        pltpu.VMEM((heads, 1, 2 * t), F32), pltpu.VMEM((heads, 1, 2 * t), F32),
                        pltpu.VMEM((heads, hw, 2 * t), F32),
                        pltpu.VMEM((heads, t, 2 * t), F32), pltpu.VMEM((heads, t, 2 * t), F32)],
        compiler_params=_cparams("arbitrary", "arbitrary", "arbitrary"),
        name="diffattn",
    )(lam, qt, proj, vt, bias, subln_g)


def _moba_kernel(qt_ref, k_ref, vt_ref, bias_ref, o_ref, m, acc, sel_ref, km, s0, s1, *, t, heads):
    cur = pl.program_id(2)
    dh = MOBA_HEAD_DIM
    nb = km.shape[0]

    def cols(g):
        return slice(g * dh, (g + 1) * dh)

    @pl.when(cur == 0)
    def _():
        def block_mean(j, carry):
            rows = pl.ds(pl.multiple_of(j * t, t), t)
            km[pl.ds(j, 1), :] = jnp.mean(k_ref[rows, :].astype(F32), axis=0, keepdims=True)
            return carry
        lax.fori_loop(0, nb, block_mean, 0)

    for g in range(heads):
        gate = jnp.dot(km[:, cols(g)], qt_ref[cols(g), :].astype(F32), preferred_element_type=F32,
                       precision=lax.Precision.HIGHEST)
        blk = lax.broadcasted_iota(jnp.int32, gate.shape, 0)
        gate = jnp.where(blk < cur, gate, -jnp.inf)
        sel = jnp.zeros(gate.shape, F32)
        for _ in range(MOBA_TOPK):
            gmax = jnp.max(gate, axis=0, keepdims=True)
            first = jnp.min(jnp.where(gate == gmax, blk, nb), axis=0, keepdims=True)
            pick = (blk == first) & (gmax > -jnp.inf)
            sel = jnp.where(pick, 1.0, sel)
            gate = jnp.where(blk == first, -jnp.inf, gate)
        sel_ref[g] = sel

    def logits(g, j0, nblk):
        rows = pl.ds(pl.multiple_of(j0 * t, t), nblk * t)
        return _dot(k_ref[rows, cols(g)], qt_ref[cols(g), :])

    def keep(g, j, valid):
        return jnp.where(valid, sel_ref[g, pl.ds(j, 1), :], 0.0) > 0.5

    def pair_vt(g, j0):
        return jnp.concatenate([vt_ref[j0, cols(g), :], vt_ref[j0 + 1, cols(g), :]], axis=1)

    @pl.when(cur == 0)
    def _():
        _online_softmax(lambda g: logits(g, 0, 1) + bias_ref[g, 0], lambda g: vt_ref[0, cols(g), :],
                        m, None, acc, heads, True)

    @pl.when(cur >= 1)
    def _():
        def near(g):
            s = logits(g, cur - 1, 2)
            prev = jnp.where(keep(g, cur - 1, True), s[:t] + bias_ref[g, 1], NEG_BIG)
            return jnp.concatenate([prev, s[t:] + bias_ref[g, 0]], axis=0)
        _online_softmax(near, lambda g: pair_vt(g, cur - 1), m, None, acc, heads, True)

    nfar = jnp.maximum(cur - 1, 0)

    def far_update(c, buf):
        j0 = 2 * c

        def masked(g):
            top = jnp.where(keep(g, j0, True), buf[g, :t, :], NEG_BIG)
            bot = jnp.where(keep(g, j0 + 1, j0 + 1 < nfar), buf[g, t:, :], NEG_BIG)
            return jnp.concatenate([top, bot], axis=0)
        _online_softmax(masked, lambda g: pair_vt(g, j0), m, None, acc, heads, False)

    _pipelined_sweep((nfar + 1) // 2, lambda g, c: logits(g, 2 * c, 2), far_update, s0, s1, heads)
    for g in range(heads):
        o_ref[:, cols(g)] = (acc[g, :dh, :] / acc[g, dh:dh + 1, :]).T.astype(BF16)


def _moba_attention(proj, qt, vt, bias, batch, seq, kcol_w, heads=MOBA_HEADS_PER_STEP):
    t = MOBA_BLOCK
    nq = seq // t
    dh = MOBA_HEAD_DIM
    gw = heads * dh
    ng = MOBA_HEADS // heads
    kcol = kcol_w // gw
    nb = nq
    once = pl.Buffered(1)
    return pl.pallas_call(
        functools.partial(_moba_kernel, t=t, heads=heads),
        grid=(batch, ng, nq),
        in_specs=[pl.BlockSpec((gw, t), lambda b, h, i: (h, b * nq + i)),
                  pl.BlockSpec((seq, gw), lambda b, h, i: (b, kcol + h), pipeline_mode=once),
                  pl.BlockSpec((nq, gw, t), lambda b, h, i: (b, h, 0), pipeline_mode=once),
                  pl.BlockSpec((heads, 2, t, t), lambda b, h, i: (h, 0, 0, 0), pipeline_mode=once)],
        out_specs=pl.BlockSpec((t, gw), lambda b, h, i: (b * nq + i, h)),
        out_shape=jax.ShapeDtypeStruct((batch * seq, MOBA_HEADS * dh), BF16),
        scratch_shapes=[pltpu.VMEM((heads, 1, t), F32),
                        pltpu.VMEM((heads, dh + DENOM_ROWS, t), F32), pltpu.VMEM((heads, nb, t), F32),
                        pltpu.VMEM((nb, gw), F32),
                        pltpu.VMEM((heads, 2 * t, t), F32), pltpu.VMEM((heads, 2 * t, t), F32)],
        compiler_params=_cparams("arbitrary", "arbitrary", "arbitrary"),
        name="moba",
    )(qt, proj, vt, bias)


def _first_index_of_max(vals, ids, sentinel):
    vmax = jnp.max(vals, axis=0, keepdims=True)
    first = jnp.min(jnp.where(vals == vmax, ids, sentinel), axis=0, keepdims=True)
    return vmax, first


def _route(scores_t, bias_t):
    e, n = scores_t.shape
    per = e // N_GROUPS
    selv = scores_t + bias_t
    sub = lax.broadcasted_iota(jnp.int32, (per, n), 0)
    gscore = []
    for g in range(N_GROUPS):
        blk = selv[g * per:(g + 1) * per, :]
        top1, first = _first_index_of_max(blk, sub, per)
        top2 = jnp.max(jnp.where(sub == first, -jnp.inf, blk), axis=0, keepdims=True)
        gscore.append(top1 + top2)
    gs = jnp.concatenate(gscore, axis=0)
    gid = lax.broadcasted_iota(jnp.int32, gs.shape, 0)
    gsel = jnp.zeros(gs.shape, F32)
    for _ in range(TOPK_GROUPS):
        _, first = _first_index_of_max(gs, gid, N_GROUPS)
        gsel = jnp.where(gid == first, 1.0, gsel)
        gs = jnp.where(gid == first, -jnp.inf, gs)
    masked = jnp.concatenate(
        [jnp.where(gsel[g:g + 1, :] > 0.5, selv[g * per:(g + 1) * per, :], -jnp.inf)
         for g in range(N_GROUPS)], axis=0)
    eid = lax.broadcasted_iota(jnp.int32, masked.shape, 0)
    picked = jnp.zeros(masked.shape, jnp.int32)
    ids, vals = [], []
    for _ in range(TOPK_EXPERTS):
        _, first = _first_index_of_max(masked, eid, e)
        hit = eid == first
        picked = jnp.where(hit, 1, picked)
        ids.append(first)
        vals.append(jnp.sum(jnp.where(hit, scores_t, 0.0), axis=0, keepdims=True))
        masked = jnp.where(hit, -jnp.inf, masked)
    w = jnp.concatenate(vals, axis=0)
    w = w / jnp.sum(w, axis=0, keepdims=True) * ROUTED_SCALE
    return picked, jnp.concatenate(ids, axis=0), w


def _route_kernel(logits_ref, rb_ref, sel_ref, idx_ref, w_ref):
    sel_ref[...], idx_ref[...], w_ref[...] = _route(jax.nn.sigmoid(logits_ref[...]), rb_ref[...])


def _route_picks(logits_t, rb_t, tn=2048):
    e, t = logits_t.shape
    tn = min(tn, t)
    k = TOPK_EXPERTS
    return pl.pallas_call(
        _route_kernel,
        grid=(t // tn,),
        in_specs=[pl.BlockSpec((e, tn), lambda i: (0, i)),
                  pl.BlockSpec((e, 1), lambda i: (0, 0))],
        out_specs=[pl.BlockSpec((e, tn), lambda i: (0, i)),
                   pl.BlockSpec((k, tn), lambda i: (0, i)),
                   pl.BlockSpec((k, tn), lambda i: (0, i))],
        out_shape=[jax.ShapeDtypeStruct((e, t), jnp.int32),
                   jax.ShapeDtypeStruct((k, t), jnp.int32),
                   jax.ShapeDtypeStruct((k, t), F32)],
        compiler_params=_cparams("arbitrary"),
        name="route",
    )(logits_t, rb_t)


def _mix_kernel(od_ref, om_ref, gd_ref, gm_ref, x_ref, mod_ref, wod_ref, wom_ref, wout_ref,
                gffn_ref, wrh_ref, wrl_ref, x1_ref, h2_ref, lg_ref, *, parts):
    n = x_ref.shape[0] // parts
    rows = [slice(p * n, (p + 1) * n) for p in range(parts)]
    y = [(_dot(od_ref[r, :], wod_ref[...]), _dot(om_ref[r, :], wom_ref[...])) for r in rows]
    mixed = []
    for r, (yd, ym) in zip(rows, y):
        z = (jax.nn.sigmoid(gd_ref[r, :].astype(F32)) * yd
             + jax.nn.sigmoid(gm_ref[r, :].astype(F32)) * ym)
        mixed.append(_dot(z.astype(BF16), wout_ref[...]))
    for r, mx in zip(rows, mixed):
        x1 = x_ref[r, :] + mod_ref[0, 2:3, :] * mx
        x1_ref[r, :] = x1
        h2 = _modulated_norm(x1, gffn_ref[...], mod_ref[0, 4:5, :], mod_ref[0, 3:4, :])
        h_hi = h2.astype(BF16)
        h2_ref[r, :] = h_hi
        h_lo = (h2 - h_hi.astype(F32)).astype(BF16)
        lg_ref[r, :] = (_dot(h_hi, wrh_ref[...]) + _dot(h_lo, wrh_ref[...])) + _dot(h_hi, wrl_ref[...])


def _mix(od, om, proj, x2, mod3, wod, wom, wout, g_ffn, w_router, seq, gd_col, gm_col, tm=256):
    t, d = x2.shape
    wr_hi = w_router.astype(BF16)
    wr_lo = (w_router - wr_hi.astype(F32)).astype(BF16)
    per_batch = seq // tm
    const = lambda i: (0, 0)
    once = pl.Buffered(1)
    return pl.pallas_call(
        functools.partial(_mix_kernel, parts=2),
        grid=(t // tm,),
        in_specs=[pl.BlockSpec((tm, od.shape[1]), lambda i: (i, 0)),
                  pl.BlockSpec((tm, om.shape[1]), lambda i: (i, 0)),
                  pl.BlockSpec((tm, d), lambda i: (i, gd_col)),
                  pl.BlockSpec((tm, d), lambda i: (i, gm_col)),
                  pl.BlockSpec((tm, d), lambda i: (i, 0)),
                  pl.BlockSpec((1, 6, d), lambda i: (i // per_batch, 0, 0)),
                  pl.BlockSpec(wod.shape, const, pipeline_mode=once),
                  pl.BlockSpec(wom.shape, const, pipeline_mode=once),
                  pl.BlockSpec(wout.shape, const, pipeline_mode=once),
                  pl.BlockSpec((1, d), const),
                  pl.BlockSpec(wr_hi.shape, const),
                  pl.BlockSpec(wr_lo.shape, const)],
        out_specs=[pl.BlockSpec((tm, d), lambda i: (i, 0)),
                   pl.BlockSpec((tm, d), lambda i: (i, 0)),
                   pl.BlockSpec((tm, N_EXPERTS), lambda i: (i, 0))],
        out_shape=[jax.ShapeDtypeStruct((t, d), F32),
                   jax.ShapeDtypeStruct((t, d), BF16),
                   jax.ShapeDtypeStruct((t, N_EXPERTS), F32)],
        compiler_params=_cparams("arbitrary"),
        name="mix",
    )(od, om, proj, proj, x2, mod3, wod, wom, wout, g_ffn, wr_hi, wr_lo)


def _expert_kernel(be_ref, na_ref, x_ref, rw_ref, wg_ref, wu_ref, wd_ref, *rest, base):
    o_ref, wg_s, wu_s, wd_s = rest[-4:]
    i = pl.program_id(0)
    blk = base + i

    @pl.when(blk < na_ref[0])
    def _():
        prev = be_ref[jnp.maximum(blk - 1, 0)]

        @pl.when((i == 0) | (be_ref[blk] != prev))
        def _():
            wg_s[...] = wg_ref[0].astype(BF16)
            wu_s[...] = wu_ref[0].astype(BF16)
            wd_s[...] = wd_ref[0].astype(BF16)

        x = x_ref[...]
        g = _dot(x, wg_s[...])
        u = _dot(x, wu_s[...])
        a = (g * jax.nn.sigmoid(g) * u).astype(BF16)
        y = _dot(a, wd_s[...])
        rows = x.shape[0]
        w_row = rw_ref[0]
        diag = (lax.broadcasted_iota(jnp.int32, (rows, rows), 0)
                == lax.broadcasted_iota(jnp.int32, (rows, rows), 1))
        w_col = jnp.sum(jnp.where(diag, w_row, 0.0), axis=1, keepdims=True)
        o_ref[...] = (y * w_col).astype(BF16)

    @pl.when(blk >= na_ref[0])
    def _():
        o_ref[...] = jnp.zeros(o_ref.shape, o_ref.dtype)


def _experts(block_expert, n_active, x_chunk, row_w, w_gate, w_up, w_down, y_pad, base, n_blocks, rows):
    pc, d = x_chunk.shape
    f = w_gate.shape[2]
    nc = pc // rows

    def local(i, na):
        return jnp.clip(jnp.minimum(base + i, na[0] - 1) - base, 0, nc - 1)

    def xmap(i, be, na):
        return (local(i, na), 0)

    def rmap(i, be, na):
        return (base + local(i, na), 0, 0)

    def wmap(i, be, na):
        return (be[base + local(i, na)], 0, 0)

    in_specs = [pl.BlockSpec((rows, d), xmap),
                pl.BlockSpec((1, 1, rows), rmap),
                pl.BlockSpec((1, d, f), wmap),
                pl.BlockSpec((1, d, f), wmap),
                pl.BlockSpec((1, f, d), wmap)]
    args = [block_expert, n_active, x_chunk, row_w.reshape(n_blocks, 1, rows), w_gate, w_up, w_down]
    aliases = {}
    if y_pad is not None:
        in_specs.append(pl.BlockSpec(memory_space=pl.ANY))
        args.append(y_pad)
        aliases = {len(args) - 1: 0}
    grid_spec = pltpu.PrefetchScalarGridSpec(
        num_scalar_prefetch=2,
        grid=(nc,),
        in_specs=in_specs,
        out_specs=pl.BlockSpec((rows, d), lambda i, be, na: (base + i, 0)),
        scratch_shapes=[pltpu.VMEM((d, f), BF16), pltpu.VMEM((d, f), BF16), pltpu.VMEM((f, d), BF16)],
    )
    return pl.pallas_call(
        functools.partial(_expert_kernel, base=base),
        grid_spec=grid_spec,
        out_shape=jax.ShapeDtypeStruct((n_blocks * rows, d), BF16),
        input_output_aliases=aliases,
        compiler_params=_cparams("arbitrary"),
        name="experts",
    )(*args)


def _shared_kernel(h2_ref, wsg_ref, wsu_ref, wsd_ref, o_ref):
    h2 = h2_ref[...]
    g = _dot(h2, wsg_ref[...])
    u = _dot(h2, wsu_ref[...])
    o_ref[...] = _dot((g * jax.nn.sigmoid(g) * u).astype(BF16), wsd_ref[...]).astype(BF16)


def _shared(h2, wsg, wsu, wsd, tm=512):
    t, d = h2.shape
    tm = min(tm, t)
    const = lambda i: (0, 0)
    row = pl.BlockSpec((tm, d), lambda i: (i, 0))
    return pl.pallas_call(
        _shared_kernel,
        grid=(t // tm,),
        in_specs=[row, pl.BlockSpec(wsg.shape, const), pl.BlockSpec(wsu.shape, const),
                  pl.BlockSpec(wsd.shape, const)],
        out_specs=row,
        out_shape=jax.ShapeDtypeStruct((t, d), BF16),
        compiler_params=_cparams("arbitrary"),
        name="shared",
    )(h2, wsg, wsu, wsd)


def _final_kernel(x1_ref, sh_ref, y_ref, mod_ref, gf_ref, *rest):
    o_ref = rest[-1]
    moe = sh_ref[...].astype(F32)
    for k in range(y_ref.shape[0]):
        moe = moe + y_ref[k].astype(F32)
    x2 = x1_ref[...] + mod_ref[0, 5:6, :] * moe
    ms = jnp.mean(x2 * x2, axis=-1, keepdims=True)
    o_ref[...] = x2 * lax.rsqrt(ms + RMS_EPS) * gf_ref[...]


def _final(x1, shared, y_tok, mod3, g_final, out, base_tok, seq, tm=FINAL_ROWS):
    t, d = x1.shape
    nk, tc, _ = y_tok.shape
    per_batch = seq // tm
    b0 = base_tok // tm
    row = pl.BlockSpec((tm, d), lambda i: (b0 + i, 0))
    in_specs = [row, row,
                pl.BlockSpec((nk, tm, d), lambda i: (0, i, 0)),
                pl.BlockSpec((1, 6, d), lambda i: ((b0 + i) // per_batch, 0, 0)),
                pl.BlockSpec((1, d), lambda i: (0, 0))]
    args = [x1, shared, y_tok, mod3, g_final]
    aliases = {}
    if out is not None:
        in_specs.append(pl.BlockSpec(memory_space=pl.ANY))
        args.append(out)
        aliases = {len(args) - 1: 0}
    return pl.pallas_call(
        _final_kernel,
        grid=(tc // tm,),
        in_specs=in_specs,
        out_specs=row,
        out_shape=jax.ShapeDtypeStruct((t, d), F32),
        input_output_aliases=aliases,
        compiler_params=_cparams("arbitrary"),
        name="final",
    )(*args)


def _chunk_bounds(n):
    total = sum(MOE_CHUNK_WEIGHTS)
    edges = [0]
    for i in range(len(MOE_CHUNK_WEIGHTS)):
        edges.append(round(n * sum(MOE_CHUNK_WEIGHTS[:i + 1]) / total))
    return [(a, b) for a, b in zip(edges[:-1], edges[1:]) if b > a]


def _bucket(rel):
    n = jnp.maximum(rel, 0)
    max_exact = REL_BUCKETS // 2
    nf = jnp.maximum(n, 1).astype(F32)
    large = max_exact + (jnp.log(nf / max_exact) / math.log(REL_MAX_DIST / max_exact)
                         * (REL_BUCKETS - max_exact)).astype(jnp.int32)
    large = jnp.minimum(large, REL_BUCKETS - 1)
    return jnp.where(n < max_exact, n, large)


def _bias_kernel(tab_ref, o_ref, *, t):
    h = pl.program_id(0)
    sub = pl.program_id(1)
    key = lax.broadcasted_iota(jnp.int32, (t, t), 0)
    qry = lax.broadcasted_iota(jnp.int32, (t, t), 1)
    rel = qry - key + sub * t
    bucket = _bucket(rel)
    far = tab_ref[h, REL_BUCKETS - 1]
    bias = jnp.zeros((t, t), F32)
    for b in range(REL_BUCKETS - 1):
        bias = jnp.where(bucket == b, tab_ref[h, b] - far, bias)
    o_ref[0, 0] = jnp.where(rel >= 0, bias * LOG2E, NEG_BIG)


def _near_bias(table, t):
    assert t >= REL_MAX_DIST
    heads = table.shape[0]
    return pl.pallas_call(
        functools.partial(_bias_kernel, t=t),
        grid=(heads, 2),
        in_specs=[pl.BlockSpec(memory_space=pltpu.SMEM)],
        out_specs=pl.BlockSpec((1, 1, t, t), lambda h, s: (h, s, 0, 0)),
        out_shape=jax.ShapeDtypeStruct((heads, 2, t, t), F32),
        compiler_params=_cparams("arbitrary", "arbitrary"),
        name="relbias",
    )(table)


def _dispatch(sel, eidx_t, w_t, rows):
    e, t = sel.shape
    k = eidx_t.shape[0]
    a = t * k
    eidx = eidx_t.T
    counts = jnp.sum(sel, axis=1)
    rank = jnp.cumsum(sel, axis=1) - sel
    padded = ((counts + rows - 1) // rows) * rows
    pad_ends = jnp.cumsum(padded)
    pad_starts = pad_ends - padded
    starts = jnp.cumsum(counts) - counts
    pos = pad_starts[eidx] + jnp.take_along_axis(rank.T, eidx, axis=1)
    n_blocks = a // rows + e
    n_active = (pad_ends[-1] // rows).astype(jnp.int32)
    end_blocks = pad_ends // rows
    block_expert = jnp.sum(jnp.arange(n_blocks)[:, None] >= end_blocks[None, :], axis=1)
    block_expert = jnp.minimum(block_expert, e - 1).astype(jnp.int32)
    order = jnp.argsort(eidx.reshape(a))
    tok_sorted = (order // k).astype(jnp.int32)
    w_sorted = w_t.T.reshape(a)[order]
    j = (jnp.arange(n_blocks) * rows - pad_starts[block_expert])[:, None] + jnp.arange(rows)[None, :]
    valid = (j < counts[block_expert][:, None]).reshape(-1)
    src = jnp.clip(starts[block_expert][:, None] + j, 0, a - 1).reshape(-1)
    row_tok = jnp.where(valid, tok_sorted[src], jnp.arange(n_blocks * rows) % t)
    row_w = jnp.where(valid, w_sorted[src], 0.0)
    return pos, row_tok, row_w, block_expert, n_active.reshape(1)


def kernel(x, c, w_ada, b_ada, g_mix, g_ffn, w_in, diff_lambda, diff_subln_g, rel_bias, w_o_diff, w_o_moba,
           w_out, w_router, router_bias, w_exp_gate, w_exp_up, w_exp_down, w_sh_gate, w_sh_up, w_sh_down,
           g_final):
    batch, seq, d = x.shape
    t = batch * seq
    depth = w_ada.shape[0]
    assert seq % MOBA_BLOCK == 0 and seq % min(DIFF_TILE, seq) == 0
    hw = 2 * DIFF_HEAD_DIM
    qk_w = DIFF_HEADS * hw
    moba_w = MOBA_HEADS * MOBA_HEAD_DIM
    o_qm = 3 * qk_w
    o_km = o_qm + moba_w
    o_vm = o_km + moba_w
    o_gd = o_vm + moba_w
    o_gm = o_gd + d
    table_diff = rel_bias[:, :DIFF_HEADS].T
    table_moba = rel_bias[:, DIFF_HEADS:].T
    bias_diff = _near_bias(table_diff, min(DIFF_TILE, seq))
    bias_moba = _near_bias(table_moba, MOBA_BLOCK)
    colscale = jnp.ones((1, w_in.shape[2]), F32)
    colscale = colscale.at[:, :qk_w].set(DIFF_HEAD_DIM ** -0.5 * LOG2E)
    colscale = colscale.at[:, o_qm:o_km].set(MOBA_HEAD_DIM ** -0.5 * LOG2E)
    c_pad = jnp.zeros((8, d), F32).at[:batch].set(c)

    assert depth == 1, "single-layer block: the final norm is fused into the last kernel"
    l = 0
    xc = x.reshape(t, d)
    mod3 = _ada(c_pad, w_ada[l], b_ada[l][None, :])[:batch].reshape(batch, 6, d)
    proj = _inproj(xc, mod3, g_mix[l][None, :], w_in[l].astype(BF16), colscale, seq, tm=min(1024, seq))
    lam_init = 0.8 - 0.6 * math.exp(-0.3 * l)
    lv = diff_lambda[l].astype(F32)
    lam = (jnp.exp(jnp.sum(lv[0] * lv[1])) - jnp.exp(jnp.sum(lv[2] * lv[3])) + lam_init).reshape(1)
    td = min(DIFF_TILE, seq)
    qd_t = proj[:, :qk_w].T
    vd_t = proj[:, 2 * qk_w:o_qm].reshape(t // td, td, qk_w).transpose(0, 2, 1)
    od = _diff_attention(proj, qd_t, vd_t, lam, bias_diff, diff_subln_g[l][:, None], batch, seq,
                         1.0 - lam_init, td)
    qm_t = proj[:, o_qm:o_km].T
    vm_t = proj[:, o_vm:o_gd].reshape(t // MOBA_BLOCK, MOBA_BLOCK, moba_w).transpose(0, 2, 1)
    om = _moba_attention(proj, qm_t, vm_t, bias_moba, batch, seq, o_km)
    x1, h2, logits = _mix(od, om, proj, xc, mod3, w_o_diff[l].astype(BF16), w_o_moba[l].astype(BF16),
                          w_out[l].astype(BF16), g_ffn[l][None, :], w_router[l], seq,
                          o_gd // d, o_gm // d)
    sel, eidx_t, w_t = _route_picks(logits.T, router_bias[l][:, None])
    pos, row_tok, row_w, block_expert, n_active = _dispatch(sel, eidx_t, w_t, EXPERT_ROWS)
    shared = _shared(h2, w_sh_gate[l].astype(BF16), w_sh_up[l].astype(BF16), w_sh_down[l].astype(BF16))
    n_blocks = row_tok.shape[0] // EXPERT_ROWS
    y_pad = None
    for b0, b1 in _chunk_bounds(n_blocks):
        x_chunk = h2[row_tok[b0 * EXPERT_ROWS:b1 * EXPERT_ROWS]]
        y_pad = _experts(block_expert, n_active, x_chunk, row_w, w_exp_gate[l], w_exp_up[l], w_exp_down[l],
                         y_pad, b0, n_blocks, EXPERT_ROWS)
    out = None
    for u0, u1 in _chunk_bounds(t // FINAL_ROWS):
        t0, t1 = u0 * FINAL_ROWS, u1 * FINAL_ROWS
        idx = pos[t0:t1].T.reshape(-1)
        y_tok = y_pad[idx].reshape(TOPK_EXPERTS, t1 - t0, d)
        out = _final(x1, shared, y_tok, mod3, g_final[None, :], out, t0, seq)
    return out.reshape(batch, seq, d)
```

```python
import functools
import math

import jax
import jax.numpy as jnp
from jax import lax
from jax.experimental import pallas as pl
from jax.experimental.pallas import tpu as pltpu

F32 = jnp.float32
BF16 = jnp.bfloat16

DIFF_HEADS = 8
DIFF_HEAD_DIM = 64
MOBA_HEADS = 8
MOBA_HEAD_DIM = 128
MOBA_BLOCK = 256
MOBA_TOPK = 3
REL_BUCKETS = 32
REL_MAX_DIST = 128
N_EXPERTS = 64
N_GROUPS = 8
TOPK_GROUPS = 4
TOPK_EXPERTS = 8
ROUTED_SCALE = 2.5
RMS_EPS = 1e-6
LANES = 128
NEG_BIG = -1e30
LOG2E = 1.4426950408889634
DIFF_TILE = 512
DIFF_HEADS_PER_STEP = 4
MOBA_HEADS_PER_STEP = 4
DENOM_ROWS = 16
EXPERT_ROWS = 512
MOE_CHUNK_WEIGHTS = (1, 2, 3, 4)
FINAL_ROWS = 256
VMEM_LIMIT = 56 * 1024 * 1024


def _cparams(*sem):
    return pltpu.CompilerParams(dimension_semantics=sem, vmem_limit_bytes=VMEM_LIMIT)


def _dot(a, b):
    return jnp.dot(a, b, preferred_element_type=F32)


def _dot_nt(a, b):
    return lax.dot_general(a, b, (((1,), (1,)), ((), ())), preferred_element_type=F32)


def _ada_kernel(c_ref, w_ref, b_ref, o_ref):
    c = c_ref[...]
    ca = c * jax.nn.sigmoid(c)
    o_ref[...] = jnp.dot(ca, w_ref[...], preferred_element_type=F32,
                         precision=lax.Precision.HIGHEST) + b_ref[...]


def _ada(c_pad, w_ada, b_ada, tn=1024):
    rows, d = c_pad.shape
    n = w_ada.shape[1]
    return pl.pallas_call(
        _ada_kernel,
        grid=(n // tn,),
        in_specs=[pl.BlockSpec((rows, d), lambda j: (0, 0)),
                  pl.BlockSpec((d, tn), lambda j: (0, j)),
                  pl.BlockSpec((1, tn), lambda j: (0, j))],
        out_specs=pl.BlockSpec((rows, tn), lambda j: (0, j)),
        out_shape=jax.ShapeDtypeStruct((rows, n), F32),
        compiler_params=_cparams("arbitrary"),
        name="ada",
    )(c_pad, w_ada, b_ada)


def _modulated_norm(x, g, scale, shift):
    ms = jnp.mean(x * x, axis=-1, keepdims=True)
    return (x * lax.rsqrt(ms + RMS_EPS) * g) * (1.0 + scale) + shift


def _inproj_kernel(x_ref, mod_ref, g_ref, w_ref, cs_ref, o_ref, h_ref, *, chunk):
    @pl.when(pl.program_id(1) == 0)
    def _():
        shift = mod_ref[0, 0:1, :]
        scale = mod_ref[0, 1:2, :]
        g = g_ref[...]

        def body(r, carry):
            rows = pl.ds(pl.multiple_of(r * chunk, chunk), chunk)
            h_ref[rows, :] = _modulated_norm(x_ref[rows, :], g, scale, shift).astype(BF16)
            return carry

        lax.fori_loop(0, x_ref.shape[0] // chunk, body, 0)

    o_ref[...] = (_dot(h_ref[...], w_ref[...]) * cs_ref[...]).astype(BF16)


def _inproj(x2, mod3, g_mix, w_in_bf, colscale, seq, tm=1024, tn=1024):
    t, d = x2.shape
    n = w_in_bf.shape[1]
    per_batch = seq // tm
    return pl.pallas_call(
        functools.partial(_inproj_kernel, chunk=128),
        grid=(t // tm, n // tn),
        in_specs=[pl.BlockSpec((tm, d), lambda i, j: (i, 0)),
                  pl.BlockSpec((1, 6, d), lambda i, j: (i // per_batch, 0, 0)),
                  pl.BlockSpec((1, d), lambda i, j: (0, 0)),
                  pl.BlockSpec((d, tn), lambda i, j: (0, j)),
                  pl.BlockSpec((1, tn), lambda i, j: (0, j))],
        out_specs=pl.BlockSpec((tm, tn), lambda i, j: (i, j)),
        out_shape=jax.ShapeDtypeStruct((t, n), BF16),
        scratch_shapes=[pltpu.VMEM((tm, d), BF16)],
        compiler_params=_cparams("arbitrary", "arbitrary"),
        name="inproj",
    )(x2, mod3, g_mix, w_in_bf, colscale)


def _online_softmax(logits_fn, vt_fn, m, l, acc, heads, first):
    s_all = [logits_fn(g) for g in range(heads)]
    p_all, a_all = [], []
    for g, s in enumerate(s_all):
        smax = jnp.max(s, axis=0, keepdims=True)
        if first:
            mg = smax
            p = jnp.exp2(s - mg)
            if l is not None:
                l[g] = jnp.sum(p, axis=0, keepdims=True)
        else:
            m_old = m[g]
            mg = jnp.maximum(m_old, smax)
            a = jnp.exp2(m_old - mg)
            p = jnp.exp2(s - mg)
            if l is not None:
                l[g] = a * l[g] + jnp.sum(p, axis=0, keepdims=True)
            a_all.append(a)
        m[g] = mg
        p_all.append(p.astype(BF16))
    for g in range(heads):
        vt = vt_fn(g)
        if l is None:
            vt = jnp.concatenate([vt, jnp.ones((DENOM_ROWS, vt.shape[1]), vt.dtype)], axis=0)
        pv = _dot(vt, p_all[g])
        acc[g] = pv if first else a_all[g] * acc[g] + pv


def _pipelined_sweep(n, logits_fn, update_fn, buf0, buf1, heads):
    def fill(buf, c):
        for g in range(heads):
            buf[g] = logits_fn(g, c)

    @pl.when(n >= 1)
    def _():
        fill(buf0, 0)

    def body(i, carry):
        a = 2 * i
        fill(buf1, a + 1)
        update_fn(a, buf0)
        fill(buf0, jnp.minimum(a + 2, n - 1))
        update_fn(a + 1, buf1)
        return carry

    lax.fori_loop(0, n // 2, body, 0)

    @pl.when(n % 2 == 1)
    def _():
        update_fn(n - 1, buf0)


def _diff_kernel(lam_ref, qt_ref, k_ref, vt_ref, bias_ref, g_ref, o_ref, w12, m, acc, s0, s1,
                 *, t, heads, out_scale):
    qi = pl.program_id(2)
    hw = 2 * DIFF_HEAD_DIM
    zeros = jnp.zeros((DIFF_HEAD_DIM, t), BF16)
    for g in range(heads):
        w12[g, :, :t] = jnp.concatenate([qt_ref[g * hw:g * hw + DIFF_HEAD_DIM, :], zeros], axis=0)
        w12[g, :, t:] = jnp.concatenate([zeros, qt_ref[g * hw + DIFF_HEAD_DIM:(g + 1) * hw, :]], axis=0)

    def logits(g, kj):
        rows = pl.ds(pl.multiple_of(kj * t, t), t)
        return _dot(k_ref[rows, g * hw:(g + 1) * hw], w12[g])

    def vt(g, kj):
        return vt_ref[kj, g * hw:(g + 1) * hw, :]

    def both(b):
        return jnp.concatenate([b, b], axis=1)

    def chunk(kj, bias_idx, first):
        def biased(g):
            s = logits(g, kj)
            return s if bias_idx is None else s + both(bias_ref[g, bias_idx])
        _online_softmax(biased, lambda g: vt(g, kj), m, None, acc, heads, first)

    chunk(qi, 0, True)

    @pl.when(qi >= 1)
    def _():
        chunk(qi - 1, 1, False)

    def far_update(kj, buf):
        _online_softmax(lambda g: buf[g], lambda g: vt(g, kj), m, None, acc, heads, False)

    _pipelined_sweep(jnp.maximum(qi - 1, 0), logits, far_update, s0, s1, heads)

    for g in range(heads):
        o12 = acc[g, :hw, :] / acc[g, hw:hw + 1, :]
        o = o12[:, :t] - lam_ref[0] * o12[:, t:]
        ms = jnp.mean(o * o, axis=0, keepdims=True)
        o = (o * lax.rsqrt(ms + RMS_EPS) * g_ref[...]) * out_scale
        o_ref[:, g * hw:(g + 1) * hw] = o.T.astype(BF16)


def _diff_attention(proj, qt, vt, lam, bias, subln_g, batch, seq, out_scale, t, heads=DIFF_HEADS_PER_STEP):
    nq = seq // t
    hw = 2 * DIFF_HEAD_DIM
    gw = heads * hw
    ng = DIFF_HEADS // heads
    kcol = DIFF_HEADS // heads
    once = pl.Buffered(1)
    return pl.pallas_call(
        functools.partial(_diff_kernel, t=t, heads=heads, out_scale=out_scale),
        grid=(batch, ng, nq),
        in_specs=[pl.BlockSpec(memory_space=pltpu.SMEM),
                  pl.BlockSpec((gw, t), lambda b, h, i: (h, b * nq + i)),
                  pl.BlockSpec((seq, gw), lambda b, h, i: (b, kcol + h), pipeline_mode=once),
                  pl.BlockSpec((nq, gw, t), lambda b, h, i: (b, h, 0), pipeline_mode=once),
                  pl.BlockSpec((heads, 2, t, t), lambda b, h, i: (h, 0, 0, 0), pipeline_mode=once),
                  pl.BlockSpec((hw, 1), lambda b, h, i: (0, 0))],
        out_specs=pl.BlockSpec((t, gw), lambda b, h, i: (b * nq + i, h)),
        out_shape=jax.ShapeDtypeStruct((batch * seq, DIFF_HEADS * hw), BF16),
        scratch_shapes=[pltpu.VMEM((heads, hw, 2 * t), BF16),
                        pltpu.VMEM((heads, 1, 2 * t), F32),
                        pltpu.VMEM((heads, hw + DENOM_ROWS, 2 * t), F32),
                        pltpu.VMEM((heads, t, 2 * t), F32), pltpu.VMEM((heads, t, 2 * t), F32)],
        compiler_params=_cparams("arbitrary", "arbitrary", "arbitrary"),
        name="diffattn",
    )(lam, qt, proj, vt, bias, subln_g)


def _moba_kernel(qt_ref, k_ref, vt_ref, bias_ref, o_ref, m, acc, sel_ref, km, s0, s1, *, t, heads):
    cur = pl.program_id(2)
    dh = MOBA_HEAD_DIM
    nb = km.shape[0]

    def cols(g):
        return slice(g * dh, (g + 1) * dh)

    @pl.when(cur == 0)
    def _():
        def block_mean(j, carry):
            rows = pl.ds(pl.multiple_of(j * t, t), t)
            km[pl.ds(j, 1), :] = jnp.mean(k_ref[rows, :].astype(F32), axis=0, keepdims=True)
            return carry
        lax.fori_loop(0, nb, block_mean, 0)

    for g in range(heads):
        gate = jnp.dot(km[:, cols(g)], qt_ref[cols(g), :].astype(F32), preferred_element_type=F32,
                       precision=lax.Precision.HIGHEST)
        blk = lax.broadcasted_iota(jnp.int32, gate.shape, 0)
        gate = jnp.where(blk < cur, gate, -jnp.inf)
        sel = jnp.zeros(gate.shape, F32)
        for _ in range(MOBA_TOPK):
            gmax = jnp.max(gate, axis=0, keepdims=True)
            first = jnp.min(jnp.where(gate == gmax, blk, nb), axis=0, keepdims=True)
            pick = (blk == first) & (gmax > -jnp.inf)
            sel = jnp.where(pick, 1.0, sel)
            gate = jnp.where(blk == first, -jnp.inf, gate)
        sel_ref[g] = sel

    def logits(g, j0, nblk):
        rows = pl.ds(pl.multiple_of(j0 * t, t), nblk * t)
        return _dot(k_ref[rows, cols(g)], qt_ref[cols(g), :])

    def keep(g, j, valid):
        return jnp.where(valid, sel_ref[g, pl.ds(j, 1), :], 0.0) > 0.5

    def pair_vt(g, j0):
        return jnp.concatenate([vt_ref[j0, cols(g), :], vt_ref[j0 + 1, cols(g), :]], axis=1)

    @pl.when(cur == 0)
    def _():
        _online_softmax(lambda g: logits(g, 0, 1) + bias_ref[g, 0], lambda g: vt_ref[0, cols(g), :],
                        m, None, acc, heads, True)

    @pl.when(cur >= 1)
    def _():
        def near(g):
            s = logits(g, cur - 1, 2)
            prev = jnp.where(keep(g, cur - 1, True), s[:t] + bias_ref[g, 1], NEG_BIG)
            return jnp.concatenate([prev, s[t:] + bias_ref[g, 0]], axis=0)
        _online_softmax(near, lambda g: pair_vt(g, cur - 1), m, None, acc, heads, True)

    nfar = jnp.maximum(cur - 1, 0)

    def far_update(c, buf):
        j0 = 2 * c

        def masked(g):
            top = jnp.where(keep(g, j0, True), buf[g, :t, :], NEG_BIG)
            bot = jnp.where(keep(g, j0 + 1, j0 + 1 < nfar), buf[g, t:, :], NEG_BIG)
            return jnp.concatenate([top, bot], axis=0)
        _online_softmax(masked, lambda g: pair_vt(g, j0), m, None, acc, heads, False)

    _pipelined_sweep((nfar + 1) // 2, lambda g, c: logits(g, 2 * c, 2), far_update, s0, s1, heads)
    for g in range(heads):
        o_ref[:, cols(g)] = (acc[g, :dh, :] / acc[g, dh:dh + 1, :]).T.astype(BF16)


def _moba_attention(proj, qt, vt, bias, batch, seq, kcol_w, heads=MOBA_HEADS_PER_STEP):
    t = MOBA_BLOCK
    nq = seq // t
    dh = MOBA_HEAD_DIM
    gw = heads * dh
    ng = MOBA_HEADS // heads
    kcol = kcol_w // gw
    nb = nq
    once = pl.Buffered(1)
    return pl.pallas_call(
        functools.partial(_moba_kernel, t=t, heads=heads),
        grid=(batch, ng, nq),
        in_specs=[pl.BlockSpec((gw, t), lambda b, h, i: (h, b * nq + i)),
                  pl.BlockSpec((seq, gw), lambda b, h, i: (b, kcol + h), pipeline_mode=once),
                  pl.BlockSpec((nq, gw, t), lambda b, h, i: (b, h, 0), pipeline_mode=once),
                  pl.BlockSpec((heads, 2, t, t), lambda b, h, i: (h, 0, 0, 0), pipeline_mode=once)],
        out_specs=pl.BlockSpec((t, gw), lambda b, h, i: (b * nq + i, h)),
        out_shape=jax.ShapeDtypeStruct((batch * seq, MOBA_HEADS * dh), BF16),
        scratch_shapes=[pltpu.VMEM((heads, 1, t), F32),
                        pltpu.VMEM((heads, dh + DENOM_ROWS, t), F32), pltpu.VMEM((heads, nb, t), F32),
                        pltpu.VMEM((nb, gw), F32),
                        pltpu.VMEM((heads, 2 * t, t), F32), pltpu.VMEM((heads, 2 * t, t), F32)],
        compiler_params=_cparams("arbitrary", "arbitrary", "arbitrary"),
        name="moba",
    )(qt, proj, vt, bias)


def _first_index_of_max(vals, ids, sentinel):
    vmax = jnp.max(vals, axis=0, keepdims=True)
    first = jnp.min(jnp.where(vals == vmax, ids, sentinel), axis=0, keepdims=True)
    return vmax, first


def _route(scores_t, bias_t):
    e, n = scores_t.shape
    per = e // N_GROUPS
    selv = scores_t + bias_t
    sub = lax.broadcasted_iota(jnp.int32, (per, n), 0)
    gscore = []
    for g in range(N_GROUPS):
        blk = selv[g * per:(g + 1) * per, :]
        top1, first = _first_index_of_max(blk, sub, per)
        top2 = jnp.max(jnp.where(sub == first, -jnp.inf, blk), axis=0, keepdims=True)
        gscore.append(top1 + top2)
    gs = jnp.concatenate(gscore, axis=0)
    gid = lax.broadcasted_iota(jnp.int32, gs.shape, 0)
    gsel = jnp.zeros(gs.shape, F32)
    for _ in range(TOPK_GROUPS):
        _, first = _first_index_of_max(gs, gid, N_GROUPS)
        gsel = jnp.where(gid == first, 1.0, gsel)
        gs = jnp.where(gid == first, -jnp.inf, gs)
    masked = jnp.concatenate(
        [jnp.where(gsel[g:g + 1, :] > 0.5, selv[g * per:(g + 1) * per, :], -jnp.inf)
         for g in range(N_GROUPS)], axis=0)
    eid = lax.broadcasted_iota(jnp.int32, masked.shape, 0)
    picked = jnp.zeros(masked.shape, jnp.int32)
    ids, vals = [], []
    for _ in range(TOPK_EXPERTS):
        _, first = _first_index_of_max(masked, eid, e)
        hit = eid == first
        picked = jnp.where(hit, 1, picked)
        ids.append(first)
        vals.append(jnp.sum(jnp.where(hit, scores_t, 0.0), axis=0, keepdims=True))
        masked = jnp.where(hit, -jnp.inf, masked)
    w = jnp.concatenate(vals, axis=0)
    w = w / jnp.sum(w, axis=0, keepdims=True) * ROUTED_SCALE
    return picked, jnp.concatenate(ids, axis=0), w


def _route_kernel(logits_ref, rb_ref, sel_ref, idx_ref, w_ref):
    sel_ref[...], idx_ref[...], w_ref[...] = _route(jax.nn.sigmoid(logits_ref[...]), rb_ref[...])


def _route_picks(logits_t, rb_t, tn=2048):
    e, t = logits_t.shape
    tn = min(tn, t)
    k = TOPK_EXPERTS
    return pl.pallas_call(
        _route_kernel,
        grid=(t // tn,),
        in_specs=[pl.BlockSpec((e, tn), lambda i: (0, i)),
                  pl.BlockSpec((e, 1), lambda i: (0, 0))],
        out_specs=[pl.BlockSpec((e, tn), lambda i: (0, i)),
                   pl.BlockSpec((k, tn), lambda i: (0, i)),
                   pl.BlockSpec((k, tn), lambda i: (0, i))],
        out_shape=[jax.ShapeDtypeStruct((e, t), jnp.int32),
                   jax.ShapeDtypeStruct((k, t), jnp.int32),
                   jax.ShapeDtypeStruct((k, t), F32)],
        compiler_params=_cparams("arbitrary"),
        name="route",
    )(logits_t, rb_t)


def _mix_kernel(od_ref, om_ref, gd_ref, gm_ref, x_ref, mod_ref, wod_ref, wom_ref, wout_ref,
                gffn_ref, wrh_ref, wrl_ref, x1_ref, h2_ref, lg_ref, *, parts):
    n = x_ref.shape[0] // parts
    rows = [slice(p * n, (p + 1) * n) for p in range(parts)]
    y = [(_dot(od_ref[r, :], wod_ref[...]), _dot(om_ref[r, :], wom_ref[...])) for r in rows]
    mixed = []
    for r, (yd, ym) in zip(rows, y):
        z = (jax.nn.sigmoid(gd_ref[r, :].astype(F32)) * yd
             + jax.nn.sigmoid(gm_ref[r, :].astype(F32)) * ym)
        mixed.append(_dot(z.astype(BF16), wout_ref[...]))
    for r, mx in zip(rows, mixed):
        x1 = x_ref[r, :] + mod_ref[0, 2:3, :] * mx
        x1_ref[r, :] = x1
        h2 = _modulated_norm(x1, gffn_ref[...], mod_ref[0, 4:5, :], mod_ref[0, 3:4, :])
        h_hi = h2.astype(BF16)
        h2_ref[r, :] = h_hi
        h_lo = (h2 - h_hi.astype(F32)).astype(BF16)
        lg_ref[r, :] = (_dot(h_hi, wrh_ref[...]) + _dot(h_lo, wrh_ref[...])) + _dot(h_hi, wrl_ref[...])


def _mix(od, om, proj, x2, mod3, wod, wom, wout, g_ffn, w_router, seq, gd_col, gm_col, tm=256):
    t, d = x2.shape
    wr_hi = w_router.astype(BF16)
    wr_lo = (w_router - wr_hi.astype(F32)).astype(BF16)
    per_batch = seq // tm
    const = lambda i: (0, 0)
    once = pl.Buffered(1)
    return pl.pallas_call(
        functools.partial(_mix_kernel, parts=2),
        grid=(t // tm,),
        in_specs=[pl.BlockSpec((tm, od.shape[1]), lambda i: (i, 0)),
                  pl.BlockSpec((tm, om.shape[1]), lambda i: (i, 0)),
                  pl.BlockSpec((tm, d), lambda i: (i, gd_col)),
                  pl.BlockSpec((tm, d), lambda i: (i, gm_col)),
                  pl.BlockSpec((tm, d), lambda i: (i, 0)),
                  pl.BlockSpec((1, 6, d), lambda i: (i // per_batch, 0, 0)),
                  pl.BlockSpec(wod.shape, const, pipeline_mode=once),
                  pl.BlockSpec(wom.shape, const, pipeline_mode=once),
                  pl.BlockSpec(wout.shape, const, pipeline_mode=once),
                  pl.BlockSpec((1, d), const),
                  pl.BlockSpec(wr_hi.shape, const),
                  pl.BlockSpec(wr_lo.shape, const)],
        out_specs=[pl.BlockSpec((tm, d), lambda i: (i, 0)),
                   pl.BlockSpec((tm, d), lambda i: (i, 0)),
                   pl.BlockSpec((tm, N_EXPERTS), lambda i: (i, 0))],
        out_shape=[jax.ShapeDtypeStruct((t, d), F32),
                   jax.ShapeDtypeStruct((t, d), BF16),
                   jax.ShapeDtypeStruct((t, N_EXPERTS), F32)],
        compiler_params=_cparams("arbitrary"),
        name="mix",
    )(od, om, proj, proj, x2, mod3, wod, wom, wout, g_ffn, wr_hi, wr_lo)


def _expert_kernel(be_ref, na_ref, x_ref, rw_ref, wg_ref, wu_ref, wd_ref, *rest, base):
    o_ref, wg_s, wu_s, wd_s = rest[-4:]
    i = pl.program_id(0)
    blk = base + i

    @pl.when(blk < na_ref[0])
    def _():
        prev = be_ref[jnp.maximum(blk - 1, 0)]

        @pl.when((i == 0) | (be_ref[blk] != prev))
        def _():
            wg_s[...] = wg_ref[0].astype(BF16)
            wu_s[...] = wu_ref[0].astype(BF16)
            wd_s[...] = wd_ref[0].astype(BF16)

        x = x_ref[...]
        g = _dot(x, wg_s[...])
        u = _dot(x, wu_s[...])
        a = (g * jax.nn.sigmoid(g) * u).astype(BF16)
        y = _dot(a, wd_s[...])
        rows = x.shape[0]
        w_row = rw_ref[0]
        diag = (lax.broadcasted_iota(jnp.int32, (rows, rows), 0)
                == lax.broadcasted_iota(jnp.int32, (rows, rows), 1))
        w_col = jnp.sum(jnp.where(diag, w_row, 0.0), axis=1, keepdims=True)
        o_ref[...] = (y * w_col).astype(BF16)

    @pl.when(blk >= na_ref[0])
    def _():
        o_ref[...] = jnp.zeros(o_ref.shape, o_ref.dtype)


def _experts(block_expert, n_active, x_chunk, row_w, w_gate, w_up, w_down, y_pad, base, n_blocks, rows):
    pc, d = x_chunk.shape
    f = w_gate.shape[2]
    nc = pc // rows

    def local(i, na):
        return jnp.clip(jnp.minimum(base + i, na[0] - 1) - base, 0, nc - 1)

    def xmap(i, be, na):
        return (local(i, na), 0)

    def rmap(i, be, na):
        return (base + local(i, na), 0, 0)

    def wmap(i, be, na):
        return (be[base + local(i, na)], 0, 0)

    in_specs = [pl.BlockSpec((rows, d), xmap),
                pl.BlockSpec((1, 1, rows), rmap),
                pl.BlockSpec((1, d, f), wmap),
                pl.BlockSpec((1, d, f), wmap),
                pl.BlockSpec((1, f, d), wmap)]
    args = [block_expert, n_active, x_chunk, row_w.reshape(n_blocks, 1, rows), w_gate, w_up, w_down]
    aliases = {}
    if y_pad is not None:
        in_specs.append(pl.BlockSpec(memory_space=pl.ANY))
        args.append(y_pad)
        aliases = {len(args) - 1: 0}
    grid_spec = pltpu.PrefetchScalarGridSpec(
        num_scalar_prefetch=2,
        grid=(nc,),
        in_specs=in_specs,
        out_specs=pl.BlockSpec((rows, d), lambda i, be, na: (base + i, 0)),
        scratch_shapes=[pltpu.VMEM((d, f), BF16), pltpu.VMEM((d, f), BF16), pltpu.VMEM((f, d), BF16)],
    )
    return pl.pallas_call(
        functools.partial(_expert_kernel, base=base),
        grid_spec=grid_spec,
        out_shape=jax.ShapeDtypeStruct((n_blocks * rows, d), BF16),
        input_output_aliases=aliases,
        compiler_params=_cparams("arbitrary"),
        name="experts",
    )(*args)


def _shared_kernel(h2_ref, wsg_ref, wsu_ref, wsd_ref, o_ref):
    h2 = h2_ref[...]
    g = _dot(h2, wsg_ref[...])
    u = _dot(h2, wsu_ref[...])
    o_ref[...] = _dot((g * jax.nn.sigmoid(g) * u).astype(BF16), wsd_ref[...]).astype(BF16)


def _shared(h2, wsg, wsu, wsd, tm=512):
    t, d = h2.shape
    tm = min(tm, t)
    const = lambda i: (0, 0)
    row = pl.BlockSpec((tm, d), lambda i: (i, 0))
    return pl.pallas_call(
        _shared_kernel,
        grid=(t // tm,),
        in_specs=[row, pl.BlockSpec(wsg.shape, const), pl.BlockSpec(wsu.shape, const),
                  pl.BlockSpec(wsd.shape, const)],
        out_specs=row,
        out_shape=jax.ShapeDtypeStruct((t, d), BF16),
        compiler_params=_cparams("arbitrary"),
        name="shared",
    )(h2, wsg, wsu, wsd)


def _final_kernel(x1_ref, sh_ref, y_ref, mod_ref, gf_ref, *rest):
    o_ref = rest[-1]
    moe = sh_ref[...].astype(F32)
    for k in range(y_ref.shape[0]):
        moe = moe + y_ref[k].astype(F32)
    x2 = x1_ref[...] + mod_ref[0, 5:6, :] * moe
    ms = jnp.mean(x2 * x2, axis=-1, keepdims=True)
    o_ref[...] = x2 * lax.rsqrt(ms + RMS_EPS) * gf_ref[...]


def _final(x1, shared, y_tok, mod3, g_final, out, base_tok, seq, tm=FINAL_ROWS):
    t, d = x1.shape
    nk, tc, _ = y_tok.shape
    per_batch = seq // tm
    b0 = base_tok // tm
    row = pl.BlockSpec((tm, d), lambda i: (b0 + i, 0))
    in_specs = [row, row,
                pl.BlockSpec((nk, tm, d), lambda i: (0, i, 0)),
                pl.BlockSpec((1, 6, d), lambda i: ((b0 + i) // per_batch, 0, 0)),
                pl.BlockSpec((1, d), lambda i: (0, 0))]
    args = [x1, shared, y_tok, mod3, g_final]
    aliases = {}
    if out is not None:
        in_specs.append(pl.BlockSpec(memory_space=pl.ANY))
        args.append(out)
        aliases = {len(args) - 1: 0}
    return pl.pallas_call(
        _final_kernel,
        grid=(tc // tm,),
        in_specs=in_specs,
        out_specs=row,
        out_shape=jax.ShapeDtypeStruct((t, d), F32),
        input_output_aliases=aliases,
        compiler_params=_cparams("arbitrary"),
        name="final",
    )(*args)


def _chunk_bounds(n):
    total = sum(MOE_CHUNK_WEIGHTS)
    edges = [0]
    for i in range(len(MOE_CHUNK_WEIGHTS)):
        edges.append(round(n * sum(MOE_CHUNK_WEIGHTS[:i + 1]) / total))
    return [(a, b) for a, b in zip(edges[:-1], edges[1:]) if b > a]


def _bucket(rel):
    n = jnp.maximum(rel, 0)
    max_exact = REL_BUCKETS // 2
    nf = jnp.maximum(n, 1).astype(F32)
    large = max_exact + (jnp.log(nf / max_exact) / math.log(REL_MAX_DIST / max_exact)
                         * (REL_BUCKETS - max_exact)).astype(jnp.int32)
    large = jnp.minimum(large, REL_BUCKETS - 1)
    return jnp.where(n < max_exact, n, large)


def _bias_kernel(tab_ref, o_ref, *, t):
    h = pl.program_id(0)
    sub = pl.program_id(1)
    key = lax.broadcasted_iota(jnp.int32, (t, t), 0)
    qry = lax.broadcasted_iota(jnp.int32, (t, t), 1)
    rel = qry - key + sub * t
    bucket = _bucket(rel)
    far = tab_ref[h, REL_BUCKETS - 1]
    bias = jnp.zeros((t, t), F32)
    for b in range(REL_BUCKETS - 1):
        bias = jnp.where(bucket == b, tab_ref[h, b] - far, bias)
    o_ref[0, 0] = jnp.where(rel >= 0, bias * LOG2E, NEG_BIG)


def _near_bias(table, t):
    assert t >= REL_MAX_DIST
    heads = table.shape[0]
    return pl.pallas_call(
        functools.partial(_bias_kernel, t=t),
        grid=(heads, 2),
        in_specs=[pl.BlockSpec(memory_space=pltpu.SMEM)],
        out_specs=pl.BlockSpec((1, 1, t, t), lambda h, s: (h, s, 0, 0)),
        out_shape=jax.ShapeDtypeStruct((heads, 2, t, t), F32),
        compiler_params=_cparams("arbitrary", "arbitrary"),
        name="relbias",
    )(table)


def _dispatch(sel, eidx_t, w_t, rows):
    e, t = sel.shape
    k = eidx_t.shape[0]
    a = t * k
    eidx = eidx_t.T
    counts = jnp.sum(sel, axis=1)
    rank = jnp.cumsum(sel, axis=1) - sel
    padded = ((counts + rows - 1) // rows) * rows
    pad_ends = jnp.cumsum(padded)
    pad_starts = pad_ends - padded
    starts = jnp.cumsum(counts) - counts
    pos = pad_starts[eidx] + jnp.take_along_axis(rank.T, eidx, axis=1)
    n_blocks = a // rows + e
    n_active = (pad_ends[-1] // rows).astype(jnp.int32)
    end_blocks = pad_ends // rows
    block_expert = jnp.sum(jnp.arange(n_blocks)[:, None] >= end_blocks[None, :], axis=1)
    block_expert = jnp.minimum(block_expert, e - 1).astype(jnp.int32)
    order = jnp.argsort(eidx.reshape(a))
    tok_sorted = (order // k).astype(jnp.int32)
    w_sorted = w_t.T.reshape(a)[order]
    j = (jnp.arange(n_blocks) * rows - pad_starts[block_expert])[:, None] + jnp.arange(rows)[None, :]
    valid = (j < counts[block_expert][:, None]).reshape(-1)
    src = jnp.clip(starts[block_expert][:, None] + j, 0, a - 1).reshape(-1)
    row_tok = jnp.where(valid, tok_sorted[src], jnp.arange(n_blocks * rows) % t)
    row_w = jnp.where(valid, w_sorted[src], 0.0)
    return pos, row_tok, row_w, block_expert, n_active.reshape(1)


def kernel(x, c, w_ada, b_ada, g_mix, g_ffn, w_in, diff_lambda, diff_subln_g, rel_bias, w_o_diff, w_o_moba,
           w_out, w_router, router_bias, w_exp_gate, w_exp_up, w_exp_down, w_sh_gate, w_sh_up, w_sh_down,
           g_final):
    batch, seq, d = x.shape
    t = batch * seq
    depth = w_ada.shape[0]
    assert seq % MOBA_BLOCK == 0 and seq % min(DIFF_TILE, seq) == 0
    hw = 2 * DIFF_HEAD_DIM
    qk_w = DIFF_HEADS * hw
    moba_w = MOBA_HEADS * MOBA_HEAD_DIM
    o_qm = 3 * qk_w
    o_km = o_qm + moba_w
    o_vm = o_km + moba_w
    o_gd = o_vm + moba_w
    o_gm = o_gd + d
    table_diff = rel_bias[:, :DIFF_HEADS].T
    table_moba = rel_bias[:, DIFF_HEADS:].T
    bias_diff = _near_bias(table_diff, min(DIFF_TILE, seq))
    bias_moba = _near_bias(table_moba, MOBA_BLOCK)
    colscale = jnp.ones((1, w_in.shape[2]), F32)
    colscale = colscale.at[:, :qk_w].set(DIFF_HEAD_DIM ** -0.5 * LOG2E)
    colscale = colscale.at[:, o_qm:o_km].set(MOBA_HEAD_DIM ** -0.5 * LOG2E)
    c_pad = jnp.zeros((8, d), F32).at[:batch].set(c)

    assert depth == 1, "single-layer block: the final norm is fused into the last kernel"
    l = 0
    xc = x.reshape(t, d)
    mod3 = _ada(c_pad, w_ada[l], b_ada[l][None, :])[:batch].reshape(batch, 6, d)
    proj = _inproj(xc, mod3, g_mix[l][None, :], w_in[l].astype(BF16), colscale, seq, tm=min(1024, seq))
    lam_init = 0.8 - 0.6 * math.exp(-0.3 * l)
    lv = diff_lambda[l].astype(F32)
    lam = (jnp.exp(jnp.sum(lv[0] * lv[1])) - jnp.exp(jnp.sum(lv[2] * lv[3])) + lam_init).reshape(1)
    td = min(DIFF_TILE, seq)
    qd_t = proj[:, :qk_w].T
    vd_t = proj[:, 2 * qk_w:o_qm].reshape(t // td, td, qk_w).transpose(0, 2, 1)
    od = _diff_attention(proj, qd_t, vd_t, lam, bias_diff, diff_subln_g[l][:, None], batch, seq,
                         1.0 - lam_init, td)
    qm_t = proj[:, o_qm:o_km].T
    vm_t = proj[:, o_vm:o_gd].reshape(t // MOBA_BLOCK, MOBA_BLOCK, moba_w).transpose(0, 2, 1)
    om = _moba_attention(proj, qm_t, vm_t, bias_moba, batch, seq, o_km)
    x1, h2, logits = _mix(od, om, proj, xc, mod3, w_o_diff[l].astype(BF16), w_o_moba[l].astype(BF16),
                          w_out[l].astype(BF16), g_ffn[l][None, :], w_router[l], seq,
                          o_gd // d, o_gm // d)
    sel, eidx_t, w_t = _route_picks(logits.T, router_bias[l][:, None])
    pos, row_tok, row_w, block_expert, n_active = _dispatch(sel, eidx_t, w_t, EXPERT_ROWS)
    shared = _shared(h2, w_sh_gate[l].astype(BF16), w_sh_up[l].astype(BF16), w_sh_down[l].astype(BF16))
    n_blocks = row_tok.shape[0] // EXPERT_ROWS
    y_pad = None
    for b0, b1 in _chunk_bounds(n_blocks):
        x_chunk = h2[row_tok[b0 * EXPERT_ROWS:b1 * EXPERT_ROWS]]
        y_pad = _experts(block_expert, n_active, x_chunk, row_w, w_exp_gate[l], w_exp_up[l], w_exp_down[l],
                         y_pad, b0, n_blocks, EXPERT_ROWS)
    out = None
    for u0, u1 in _chunk_bounds(t // FINAL_ROWS):
        t0, t1 = u0 * FINAL_ROWS, u1 * FINAL_ROWS
        idx = pos[t0:t1].T.reshape(-1)
        y_tok = y_pad[idx].reshape(TOPK_EXPERTS, t1 - t0, d)
        out = _final(x1, shared, y_tok, mod3, g_final[None, :], out, t0, seq)
    return out.reshape(batch, seq, d)
```

```python
import functools
import math

import jax
import jax.numpy as jnp
from jax import lax
from jax.experimental import pallas as pl
from jax.experimental.pallas import tpu as pltpu

F32 = jnp.float32
BF16 = jnp.bfloat16

DIFF_HEADS = 8
DIFF_HEAD_DIM = 64
MOBA_HEADS = 8
MOBA_HEAD_DIM = 128
MOBA_BLOCK = 256
MOBA_TOPK = 3
REL_BUCKETS = 32
REL_MAX_DIST = 128
N_EXPERTS = 64
N_GROUPS = 8
TOPK_GROUPS = 4
TOPK_EXPERTS = 8
ROUTED_SCALE = 2.5
RMS_EPS = 1e-6
NEG_BIG = -1e30
LOG2E = 1.4426950408889634
DIFF_TILE = 512
DIFF_HEADS_PER_STEP = 4
MOBA_HEADS_PER_STEP = 4
DENOM_ROWS = 16
EXPERT_ROWS = 512
MOE_CHUNK_WEIGHTS = (1, 2, 3, 4)
FINAL_ROWS = 256
VMEM_LIMIT = 56 * 1024 * 1024


def _cparams(*sem):
    return pltpu.CompilerParams(dimension_semantics=sem, vmem_limit_bytes=VMEM_LIMIT)


def _dot(a, b):
    return jnp.dot(a, b, preferred_element_type=F32)


def _ada_kernel(c_ref, w_ref, b_ref, o_ref):
    c = c_ref[...]
    ca = c * jax.nn.sigmoid(c)
    o_ref[...] = jnp.dot(ca, w_ref[...], preferred_element_type=F32,
                         precision=lax.Precision.HIGHEST) + b_ref[...]


def _ada(c_pad, w_ada, b_ada, tn=1024):
    rows, d = c_pad.shape
    n = w_ada.shape[1]
    return pl.pallas_call(
        _ada_kernel,
        grid=(n // tn,),
        in_specs=[pl.BlockSpec((rows, d), lambda j: (0, 0)),
                  pl.BlockSpec((d, tn), lambda j: (0, j)),
                  pl.BlockSpec((1, tn), lambda j: (0, j))],
        out_specs=pl.BlockSpec((rows, tn), lambda j: (0, j)),
        out_shape=jax.ShapeDtypeStruct((rows, n), F32),
        compiler_params=_cparams("arbitrary"),
        name="ada",
    )(c_pad, w_ada, b_ada)


def _modulated_norm(x, g, scale, shift):
    ms = jnp.mean(x * x, axis=-1, keepdims=True)
    return (x * lax.rsqrt(ms + RMS_EPS) * g) * (1.0 + scale) + shift


def _inproj_kernel(x_ref, mod_ref, g_ref, w_ref, cs_ref, o_ref, h_ref, *, chunk):
    @pl.when(pl.program_id(1) == 0)
    def _():
        shift = mod_ref[0, 0:1, :]
        scale = mod_ref[0, 1:2, :]
        g = g_ref[...]

        def body(r, carry):
            rows = pl.ds(pl.multiple_of(r * chunk, chunk), chunk)
            h_ref[rows, :] = _modulated_norm(x_ref[rows, :], g, scale, shift).astype(BF16)
            return carry

        lax.fori_loop(0, x_ref.shape[0] // chunk, body, 0)

    o_ref[...] = (_dot(h_ref[...], w_ref[...]) * cs_ref[...]).astype(BF16)


def _inproj(x2, mod3, g_mix, w_in_bf, colscale, seq, tm=1024, tn=1024):
    t, d = x2.shape
    n = w_in_bf.shape[1]
    per_batch = seq // tm
    return pl.pallas_call(
        functools.partial(_inproj_kernel, chunk=128),
        grid=(t // tm, n // tn),
        in_specs=[pl.BlockSpec((tm, d), lambda i, j: (i, 0)),
                  pl.BlockSpec((1, 6, d), lambda i, j: (i // per_batch, 0, 0)),
                  pl.BlockSpec((1, d), lambda i, j: (0, 0)),
                  pl.BlockSpec((d, tn), lambda i, j: (0, j)),
                  pl.BlockSpec((1, tn), lambda i, j: (0, j))],
        out_specs=pl.BlockSpec((tm, tn), lambda i, j: (i, j)),
        out_shape=jax.ShapeDtypeStruct((t, n), BF16),
        scratch_shapes=[pltpu.VMEM((tm, d), BF16)],
        compiler_params=_cparams("arbitrary", "arbitrary"),
        name="inproj",
    )(x2, mod3, g_mix, w_in_bf, colscale)


def _online_softmax(logits_fn, vt_fn, m, l, acc, heads, first):
    s_all = [logits_fn(g) for g in range(heads)]
    p_all, a_all = [], []
    for g, s in enumerate(s_all):
        smax = jnp.max(s, axis=0, keepdims=True)
        if first:
            mg = smax
            p = jnp.exp2(s - mg)
            if l is not None:
                l[g] = jnp.sum(p, axis=0, keepdims=True)
        else:
            m_old = m[g]
            mg = jnp.maximum(m_old, smax)
            a = jnp.exp2(m_old - mg)
            p = jnp.exp2(s - mg)
            if l is not None:
                l[g] = a * l[g] + jnp.sum(p, axis=0, keepdims=True)
            a_all.append(a)
        m[g] = mg
        p_all.append(p.astype(BF16))
    for g in range(heads):
        vt = vt_fn(g)
        if l is None:
            vt = jnp.concatenate([vt, jnp.ones((DENOM_ROWS, vt.shape[1]), vt.dtype)], axis=0)
        pv = _dot(vt, p_all[g])
        acc[g] = pv if first else a_all[g] * acc[g] + pv


def _pipelined_sweep(n, logits_fn, update_fn, buf0, buf1, heads):
    def fill(buf, c):
        for g in range(heads):
            buf[g] = logits_fn(g, c)

    @pl.when(n >= 1)
    def _():
        fill(buf0, 0)

    def body(i, carry):
        a = 2 * i
        fill(buf1, a + 1)
        update_fn(a, buf0)
        fill(buf0, jnp.minimum(a + 2, n - 1))
        update_fn(a + 1, buf1)
        return carry

    lax.fori_loop(0, n // 2, body, 0)

    @pl.when(n % 2 == 1)
    def _():
        update_fn(n - 1, buf0)


def _diff_kernel(lam_ref, qt_ref, k_ref, vt_ref, bias_ref, g_ref, o_ref, w12, m, acc, s0, s1,
                 *, t, heads, out_scale):
    qi = pl.program_id(2)
    hw = 2 * DIFF_HEAD_DIM
    zeros = jnp.zeros((DIFF_HEAD_DIM, t), BF16)
    for g in range(heads):
        w12[g, :, :t] = jnp.concatenate([qt_ref[g * hw:g * hw + DIFF_HEAD_DIM, :], zeros], axis=0)
        w12[g, :, t:] = jnp.concatenate([zeros, qt_ref[g * hw + DIFF_HEAD_DIM:(g + 1) * hw, :]], axis=0)

    def logits(g, kj):
        rows = pl.ds(pl.multiple_of(kj * t, t), t)
        return _dot(k_ref[rows, g * hw:(g + 1) * hw], w12[g])

    def vt(g, kj):
        return vt_ref[kj, g * hw:(g + 1) * hw, :]

    def both(b):
        return jnp.concatenate([b, b], axis=1)

    def chunk(kj, bias_idx, first):
        def biased(g):
            s = logits(g, kj)
            return s if bias_idx is None else s + both(bias_ref[g, bias_idx])
        _online_softmax(biased, lambda g: vt(g, kj), m, None, acc, heads, first)

    chunk(qi, 0, True)

    @pl.when(qi >= 1)
    def _():
        chunk(qi - 1, 1, False)

    def far_update(kj, buf):
        _online_softmax(lambda g: buf[g], lambda g: vt(g, kj), m, None, acc, heads, False)

    _pipelined_sweep(jnp.maximum(qi - 1, 0), logits, far_update, s0, s1, heads)

    for g in range(heads):
        o12 = acc[g, :hw, :] / acc[g, hw:hw + 1, :]
        o = o12[:, :t] - lam_ref[0] * o12[:, t:]
        ms = jnp.mean(o * o, axis=0, keepdims=True)
        o = (o * lax.rsqrt(ms + RMS_EPS) * g_ref[...]) * out_scale
        o_ref[:, g * hw:(g + 1) * hw] = o.T.astype(BF16)


def _diff_attention(proj, qt, vt, lam, bias, subln_g, batch, seq, out_scale, t, heads=DIFF_HEADS_PER_STEP):
    nq = seq // t
    hw = 2 * DIFF_HEAD_DIM
    gw = heads * hw
    ng = DIFF_HEADS // heads
    kcol = DIFF_HEADS // heads
    once = pl.Buffered(1)
    return pl.pallas_call(
        functools.partial(_diff_kernel, t=t, heads=heads, out_scale=out_scale),
        grid=(batch, ng, nq),
        in_specs=[pl.BlockSpec(memory_space=pltpu.SMEM),
                  pl.BlockSpec((gw, t), lambda b, h, i: (h, b * nq + i)),
                  pl.BlockSpec((seq, gw), lambda b, h, i: (b, kcol + h), pipeline_mode=once),
                  pl.BlockSpec((nq, gw, t), lambda b, h, i: (b, h, 0), pipeline_mode=once),
                  pl.BlockSpec((heads, 2, t, t), lambda b, h, i: (h, 0, 0, 0), pipeline_mode=once),
                  pl.BlockSpec((hw, 1), lambda b, h, i: (0, 0))],
        out_specs=pl.BlockSpec((t, gw), lambda b, h, i: (b * nq + i, h)),
        out_shape=jax.ShapeDtypeStruct((batch * seq, DIFF_HEADS * hw), BF16),
        scratch_shapes=[pltpu.VMEM((heads, hw, 2 * t), BF16),
                        pltpu.VMEM((heads, 1, 2 * t), F32),
                        pltpu.VMEM((heads, hw + DENOM_ROWS, 2 * t), F32),
                        pltpu.VMEM((heads, t, 2 * t), F32), pltpu.VMEM((heads, t, 2 * t), F32)],
        compiler_params=_cparams("arbitrary", "arbitrary", "arbitrary"),
        name="diffattn",
    )(lam, qt, proj, vt, bias, subln_g)


def _moba_kernel(qt_ref, k_ref, vt_ref, bias_ref, o_ref, m, acc, sel_ref, km, s0, s1, *, t, heads):
    cur = pl.program_id(2)
    dh = MOBA_HEAD_DIM
    nb = km.shape[0]

    def cols(g):
        return slice(g * dh, (g + 1) * dh)

    @pl.when(cur == 0)
    def _():
        def block_mean(j, carry):
            rows = pl.ds(pl.multiple_of(j * t, t), t)
            km[pl.ds(j, 1), :] = jnp.mean(k_ref[rows, :].astype(F32), axis=0, keepdims=True)
            return carry
        lax.fori_loop(0, nb, block_mean, 0)

    for g in range(heads):
        gate = jnp.dot(km[:, cols(g)], qt_ref[cols(g), :].astype(F32), preferred_element_type=F32,
                       precision=lax.Precision.HIGHEST)
        blk = lax.broadcasted_iota(jnp.int32, gate.shape, 0)
        gate = jnp.where(blk < cur, gate, -jnp.inf)
        sel = jnp.zeros(gate.shape, F32)
        for _ in range(MOBA_TOPK):
            gmax = jnp.max(gate, axis=0, keepdims=True)
            first = jnp.min(jnp.where(gate == gmax, blk, nb), axis=0, keepdims=True)
            pick = (blk == first) & (gmax > -jnp.inf)
            sel = jnp.where(pick, 1.0, sel)
            gate = jnp.where(blk == first, -jnp.inf, gate)
        sel_ref[g] = sel

    def logits(g, j0, nblk):
        rows = pl.ds(pl.multiple_of(j0 * t, t), nblk * t)
        return _dot(k_ref[rows, cols(g)], qt_ref[cols(g), :])

    def keep(g, j, valid):
        return jnp.where(valid, sel_ref[g, pl.ds(j, 1), :], 0.0) > 0.5

    def pair_vt(g, j0):
        return jnp.concatenate([vt_ref[j0, cols(g), :], vt_ref[j0 + 1, cols(g), :]], axis=1)

    @pl.when(cur == 0)
    def _():
        _online_softmax(lambda g: logits(g, 0, 1) + bias_ref[g, 0], lambda g: vt_ref[0, cols(g), :],
                        m, None, acc, heads, True)

    @pl.when(cur >= 1)
    def _():
        def near(g):
            s = logits(g, cur - 1, 2)
            prev = jnp.where(keep(g, cur - 1, True), s[:t] + bias_ref[g, 1], NEG_BIG)
            return jnp.concatenate([prev, s[t:] + bias_ref[g, 0]], axis=0)
        _online_softmax(near, lambda g: pair_vt(g, cur - 1), m, None, acc, heads, True)

    nfar = jnp.maximum(cur - 1, 0)

    def far_update(c, buf):
        j0 = 2 * c

        def masked(g):
            top = jnp.where(keep(g, j0, True), buf[g, :t, :], NEG_BIG)
            bot = jnp.where(keep(g, j0 + 1, j0 + 1 < nfar), buf[g, t:, :], NEG_BIG)
            return jnp.concatenate([top, bot], axis=0)
        _online_softmax(masked, lambda g: pair_vt(g, j0), m, None, acc, heads, False)

    _pipelined_sweep((nfar + 1) // 2, lambda g, c: logits(g, 2 * c, 2), far_update, s0, s1, heads)
    for g in range(heads):
        o_ref[:, cols(g)] = (acc[g, :dh, :] / acc[g, dh:dh + 1, :]).T.astype(BF16)


def _moba_attention(proj, qt, vt, bias, batch, seq, kcol_w, heads=MOBA_HEADS_PER_STEP):
    t = MOBA_BLOCK
    nq = seq // t
    dh = MOBA_HEAD_DIM
    gw = heads * dh
    ng = MOBA_HEADS // heads
    kcol = kcol_w // gw
    nb = nq
    once = pl.Buffered(1)
    return pl.pallas_call(
        functools.partial(_moba_kernel, t=t, heads=heads),
        grid=(batch, ng, nq),
        in_specs=[pl.BlockSpec((gw, t), lambda b, h, i: (h, b * nq + i)),
                  pl.BlockSpec((seq, gw), lambda b, h, i: (b, kcol + h), pipeline_mode=once),
                  pl.BlockSpec((nq, gw, t), lambda b, h, i: (b, h, 0), pipeline_mode=once),
                  pl.BlockSpec((heads, 2, t, t), lambda b, h, i: (h, 0, 0, 0), pipeline_mode=once)],
        out_specs=pl.BlockSpec((t, gw), lambda b, h, i: (b * nq + i, h)),
        out_shape=jax.ShapeDtypeStruct((batch * seq, MOBA_HEADS * dh), BF16),
        scratch_shapes=[pltpu.VMEM((heads, 1, t), F32),
                        pltpu.VMEM((heads, dh + DENOM_ROWS, t), F32), pltpu.VMEM((heads, nb, t), F32),
                        pltpu.VMEM((nb, gw), F32),
                        pltpu.VMEM((heads, 2 * t, t), F32), pltpu.VMEM((heads, 2 * t, t), F32)],
        compiler_params=_cparams("arbitrary", "arbitrary", "arbitrary"),
        name="moba",
    )(qt, proj, vt, bias)


def _first_index_of_max(vals, ids, sentinel):
    vmax = jnp.max(vals, axis=0, keepdims=True)
    first = jnp.min(jnp.where(vals == vmax, ids, sentinel), axis=0, keepdims=True)
    return vmax, first


def _route(scores_t, bias_t):
    e, n = scores_t.shape
    per = e // N_GROUPS
    selv = scores_t + bias_t
    sub = lax.broadcasted_iota(jnp.int32, (per, n), 0)
    gscore = []
    for g in range(N_GROUPS):
        blk = selv[g * per:(g + 1) * per, :]
        top1, first = _first_index_of_max(blk, sub, per)
        top2 = jnp.max(jnp.where(sub == first, -jnp.inf, blk), axis=0, keepdims=True)
        gscore.append(top1 + top2)
    gs = jnp.concatenate(gscore, axis=0)
    gid = lax.broadcasted_iota(jnp.int32, gs.shape, 0)
    gsel = jnp.zeros(gs.shape, F32)
    for _ in range(TOPK_GROUPS):
        _, first = _first_index_of_max(gs, gid, N_GROUPS)
        gsel = jnp.where(gid == first, 1.0, gsel)
        gs = jnp.where(gid == first, -jnp.inf, gs)
    masked = jnp.concatenate(
        [jnp.where(gsel[g:g + 1, :] > 0.5, selv[g * per:(g + 1) * per, :], -jnp.inf)
         for g in range(N_GROUPS)], axis=0)
    eid = lax.broadcasted_iota(jnp.int32, masked.shape, 0)
    picked = jnp.zeros(masked.shape, jnp.int32)
    ids, vals = [], []
    for _ in range(TOPK_EXPERTS):
        _, first = _first_index_of_max(masked, eid, e)
        hit = eid == first
        picked = jnp.where(hit, 1, picked)
        ids.append(first)
        vals.append(jnp.sum(jnp.where(hit, scores_t, 0.0), axis=0, keepdims=True))
        masked = jnp.where(hit, -jnp.inf, masked)
    w = jnp.concatenate(vals, axis=0)
    w = w / jnp.sum(w, axis=0, keepdims=True) * ROUTED_SCALE
    return picked, jnp.concatenate(ids, axis=0), w


def _route_kernel(logits_ref, rb_ref, sel_ref, idx_ref, w_ref):
    sel_ref[...], idx_ref[...], w_ref[...] = _route(jax.nn.sigmoid(logits_ref[...]), rb_ref[...])


def _route_picks(logits_t, rb_t, tn=2048):
    e, t = logits_t.shape
    tn = min(tn, t)
    k = TOPK_EXPERTS
    return pl.pallas_call(
        _route_kernel,
        grid=(t // tn,),
        in_specs=[pl.BlockSpec((e, tn), lambda i: (0, i)),
                  pl.BlockSpec((e, 1), lambda i: (0, 0))],
        out_specs=[pl.BlockSpec((e, tn), lambda i: (0, i)),
                   pl.BlockSpec((k, tn), lambda i: (0, i)),
                   pl.BlockSpec((k, tn), lambda i: (0, i))],
        out_shape=[jax.ShapeDtypeStruct((e, t), jnp.int32),
                   jax.ShapeDtypeStruct((k, t), jnp.int32),
                   jax.ShapeDtypeStruct((k, t), F32)],
        compiler_params=_cparams("arbitrary"),
        name="route",
    )(logits_t, rb_t)


def _mix_kernel(od_ref, om_ref, gd_ref, gm_ref, x_ref, mod_ref, wod_ref, wom_ref, wout_ref,
                gffn_ref, wrh_ref, wrl_ref, x1_ref, h2_ref, lg_ref, *, parts):
    n = x_ref.shape[0] // parts
    rows = [slice(p * n, (p + 1) * n) for p in range(parts)]
    y = [(_dot(od_ref[r, :], wod_ref[...]), _dot(om_ref[r, :], wom_ref[...])) for r in rows]
    mixed = []
    for r, (yd, ym) in zip(rows, y):
        z = (jax.nn.sigmoid(gd_ref[r, :].astype(F32)) * yd
             + jax.nn.sigmoid(gm_ref[r, :].astype(F32)) * ym)
        mixed.append(_dot(z.astype(BF16), wout_ref[...]))
    for r, mx in zip(rows, mixed):
        x1 = x_ref[r, :] + mod_ref[0, 2:3, :] * mx
        x1_ref[r, :] = x1
        h2 = _modulated_norm(x1, gffn_ref[...], mod_ref[0, 4:5, :], mod_ref[0, 3:4, :])
        h_hi = h2.astype(BF16)
        h2_ref[r, :] = h_hi
        h_lo = (h2 - h_hi.astype(F32)).astype(BF16)
        lg_ref[r, :] = (_dot(h_hi, wrh_ref[...]) + _dot(h_lo, wrh_ref[...])) + _dot(h_hi, wrl_ref[...])


def _mix(od, om, proj, x2, mod3, wod, wom, wout, g_ffn, w_router, seq, gd_col, gm_col, tm=256):
    t, d = x2.shape
    wr_hi = w_router.astype(BF16)
    wr_lo = (w_router - wr_hi.astype(F32)).astype(BF16)
    per_batch = seq // tm
    const = lambda i: (0, 0)
    once = pl.Buffered(1)
    return pl.pallas_call(
        functools.partial(_mix_kernel, parts=2),
        grid=(t // tm,),
        in_specs=[pl.BlockSpec((tm, od.shape[1]), lambda i: (i, 0)),
                  pl.BlockSpec((tm, om.shape[1]), lambda i: (i, 0)),
                  pl.BlockSpec((tm, d), lambda i: (i, gd_col)),
                  pl.BlockSpec((tm, d), lambda i: (i, gm_col)),
                  pl.BlockSpec((tm, d), lambda i: (i, 0)),
                  pl.BlockSpec((1, 6, d), lambda i: (i // per_batch, 0, 0)),
                  pl.BlockSpec(wod.shape, const, pipeline_mode=once),
                  pl.BlockSpec(wom.shape, const, pipeline_mode=once),
                  pl.BlockSpec(wout.shape, const, pipeline_mode=once),
                  pl.BlockSpec((1, d), const),
                  pl.BlockSpec(wr_hi.shape, const),
                  pl.BlockSpec(wr_lo.shape, const)],
        out_specs=[pl.BlockSpec((tm, d), lambda i: (i, 0)),
                   pl.BlockSpec((tm, d), lambda i: (i, 0)),
                   pl.BlockSpec((tm, N_EXPERTS), lambda i: (i, 0))],
        out_shape=[jax.ShapeDtypeStruct((t, d), F32),
                   jax.ShapeDtypeStruct((t, d), BF16),
                   jax.ShapeDtypeStruct((t, N_EXPERTS), F32)],
        compiler_params=_cparams("arbitrary"),
        name="mix",
    )(od, om, proj, proj, x2, mod3, wod, wom, wout, g_ffn, wr_hi, wr_lo)


def _expert_kernel(be_ref, na_ref, x_ref, rw_ref, wg_ref, wu_ref, wd_ref, *rest, base):
    o_ref, wg_s, wu_s, wd_s = rest[-4:]
    i = pl.program_id(0)
    blk = base + i

    @pl.when(blk < na_ref[0])
    def _():
        prev = be_ref[jnp.maximum(blk - 1, 0)]

        @pl.when((i == 0) | (be_ref[blk] != prev))
        def _():
            wg_s[...] = wg_ref[0].astype(BF16)
            wu_s[...] = wu_ref[0].astype(BF16)
            wd_s[...] = wd_ref[0].astype(BF16)

        x = x_ref[...]
        g = _dot(x, wg_s[...])
        u = _dot(x, wu_s[...])
        a = (g * jax.nn.sigmoid(g) * u).astype(BF16)
        y = _dot(a, wd_s[...])
        rows = x.shape[0]
        w_row = rw_ref[0]
        diag = (lax.broadcasted_iota(jnp.int32, (rows, rows), 0)
                == lax.broadcasted_iota(jnp.int32, (rows, rows), 1))
        w_col = jnp.sum(jnp.where(diag, w_row, 0.0), axis=1, keepdims=True)
        o_ref[...] = (y * w_col).astype(BF16)

    @pl.when(blk >= na_ref[0])
    def _():
        o_ref[...] = jnp.zeros(o_ref.shape, o_ref.dtype)


def _experts(block_expert, n_active, x_chunk, row_w, w_gate, w_up, w_down, y_pad, base, n_blocks, rows):
    pc, d = x_chunk.shape
    f = w_gate.shape[2]
    nc = pc // rows

    def local(i, na):
        return jnp.clip(jnp.minimum(base + i, na[0] - 1) - base, 0, nc - 1)

    def xmap(i, be, na):
        return (local(i, na), 0)

    def rmap(i, be, na):
        return (base + local(i, na), 0, 0)

    def wmap(i, be, na):
        return (be[base + local(i, na)], 0, 0)

    in_specs = [pl.BlockSpec((rows, d), xmap),
                pl.BlockSpec((1, 1, rows), rmap),
                pl.BlockSpec((1, d, f), wmap),
                pl.BlockSpec((1, d, f), wmap),
                pl.BlockSpec((1, f, d), wmap)]
    args = [block_expert, n_active, x_chunk, row_w.reshape(n_blocks, 1, rows), w_gate, w_up, w_down]
    aliases = {}
    if y_pad is not None:
        in_specs.append(pl.BlockSpec(memory_space=pl.ANY))
        args.append(y_pad)
        aliases = {len(args) - 1: 0}
    grid_spec = pltpu.PrefetchScalarGridSpec(
        num_scalar_prefetch=2,
        grid=(nc,),
        in_specs=in_specs,
        out_specs=pl.BlockSpec((rows, d), lambda i, be, na: (base + i, 0)),
        scratch_shapes=[pltpu.VMEM((d, f), BF16), pltpu.VMEM((d, f), BF16), pltpu.VMEM((f, d), BF16)],
    )
    return pl.pallas_call(
        functools.partial(_expert_kernel, base=base),
        grid_spec=grid_spec,
        out_shape=jax.ShapeDtypeStruct((n_blocks * rows, d), BF16),
        input_output_aliases=aliases,
        compiler_params=_cparams("arbitrary"),
        name="experts",
    )(*args)


def _shared_kernel(h2_ref, wsg_ref, wsu_ref, wsd_ref, o_ref):
    h2 = h2_ref[...]
    g = _dot(h2, wsg_ref[...])
    u = _dot(h2, wsu_ref[...])
    o_ref[...] = _dot((g * jax.nn.sigmoid(g) * u).astype(BF16), wsd_ref[...]).astype(BF16)


def _shared(h2, wsg, wsu, wsd, tm=512):
    t, d = h2.shape
    tm = min(tm, t)
    const = lambda i: (0, 0)
    row = pl.BlockSpec((tm, d), lambda i: (i, 0))
    return pl.pallas_call(
        _shared_kernel,
        grid=(t // tm,),
        in_specs=[row, pl.BlockSpec(wsg.shape, const), pl.BlockSpec(wsu.shape, const),
                  pl.BlockSpec(wsd.shape, const)],
        out_specs=row,
        out_shape=jax.ShapeDtypeStruct((t, d), BF16),
        compiler_params=_cparams("arbitrary"),
        name="shared",
    )(h2, wsg, wsu, wsd)


def _final_kernel(x1_ref, sh_ref, y_ref, mod_ref, gf_ref, *rest):
    o_ref = rest[-1]
    moe = sh_ref[...].astype(F32)
    for k in range(y_ref.shape[0]):
        moe = moe + y_ref[k].astype(F32)
    x2 = x1_ref[...] + mod_ref[0, 5:6, :] * moe
    ms = jnp.mean(x2 * x2, axis=-1, keepdims=True)
    o_ref[...] = x2 * lax.rsqrt(ms + RMS_EPS) * gf_ref[...]


def _final(x1, shared, y_tok, mod3, g_final, out, base_tok, seq, tm=FINAL_ROWS):
    t, d = x1.shape
    nk, tc, _ = y_tok.shape
    per_batch = seq // tm
    b0 = base_tok // tm
    row = pl.BlockSpec((tm, d), lambda i: (b0 + i, 0))
    in_specs = [row, row,
                pl.BlockSpec((nk, tm, d), lambda i: (0, i, 0)),
                pl.BlockSpec((1, 6, d), lambda i: ((b0 + i) // per_batch, 0, 0)),
                pl.BlockSpec((1, d), lambda i: (0, 0))]
    args = [x1, shared, y_tok, mod3, g_final]
    aliases = {}
    if out is not None:
        in_specs.append(pl.BlockSpec(memory_space=pl.ANY))
        args.append(out)
        aliases = {len(args) - 1: 0}
    return pl.pallas_call(
        _final_kernel,
        grid=(tc // tm,),
        in_specs=in_specs,
        out_specs=row,
        out_shape=jax.ShapeDtypeStruct((t, d), F32),
        input_output_aliases=aliases,
        compiler_params=_cparams("arbitrary"),
        name="final",
    )(*args)


def _chunk_bounds(n):
    total = sum(MOE_CHUNK_WEIGHTS)
    edges = [0]
    for i in range(len(MOE_CHUNK_WEIGHTS)):
        edges.append(round(n * sum(MOE_CHUNK_WEIGHTS[:i + 1]) / total))
    return [(a, b) for a, b in zip(edges[:-1], edges[1:]) if b > a]


def _bucket(rel):
    n = jnp.maximum(rel, 0)
    max_exact = REL_BUCKETS // 2
    nf = jnp.maximum(n, 1).astype(F32)
    large = max_exact + (jnp.log(nf / max_exact) / math.log(REL_MAX_DIST / max_exact)
                         * (REL_BUCKETS - max_exact)).astype(jnp.int32)
    large = jnp.minimum(large, REL_BUCKETS - 1)
    return jnp.where(n < max_exact, n, large)


def _bias_kernel(tab_ref, o_ref, *, t):
    h = pl.program_id(0)
    sub = pl.program_id(1)
    key = lax.broadcasted_iota(jnp.int32, (t, t), 0)
    qry = lax.broadcasted_iota(jnp.int32, (t, t), 1)
    rel = qry - key + sub * t
    bucket = _bucket(rel)
    far = tab_ref[h, REL_BUCKETS - 1]
    bias = jnp.zeros((t, t), F32)
    for b in range(REL_BUCKETS - 1):
        bias = jnp.where(bucket == b, tab_ref[h, b] - far, bias)
    o_ref[0, 0] = jnp.where(rel >= 0, bias * LOG2E, NEG_BIG)


def _near_bias(table, t):
    assert t >= REL_MAX_DIST
    heads = table.shape[0]
    return pl.pallas_call(
        functools.partial(_bias_kernel, t=t),
        grid=(heads, 2),
        in_specs=[pl.BlockSpec(memory_space=pltpu.SMEM)],
        out_specs=pl.BlockSpec((1, 1, t, t), lambda h, s: (h, s, 0, 0)),
        out_shape=jax.ShapeDtypeStruct((heads, 2, t, t), F32),
        compiler_params=_cparams("arbitrary", "arbitrary"),
        name="relbias",
    )(table)


def _dispatch(sel, eidx_t, w_t, rows):
    e, t = sel.shape
    k = eidx_t.shape[0]
    a = t * k
    eidx = eidx_t.T
    counts = jnp.sum(sel, axis=1)
    rank = jnp.cumsum(sel, axis=1) - sel
    padded = ((counts + rows - 1) // rows) * rows
    pad_ends = jnp.cumsum(padded)
    pad_starts = pad_ends - padded
    starts = jnp.cumsum(counts) - counts
    dest = rank + pad_starts[:, None]
    pos_t = jnp.sum(jnp.where(eidx_t[:, None, :] == jnp.arange(e)[None, :, None], dest[None], 0), axis=1)
    n_blocks = a // rows + e
    n_active = (pad_ends[-1] // rows).astype(jnp.int32)
    end_blocks = pad_ends // rows
    block_expert = jnp.sum(jnp.arange(n_blocks)[:, None] >= end_blocks[None, :], axis=1)
    block_expert = jnp.minimum(block_expert, e - 1).astype(jnp.int32)
    order = jnp.argsort(eidx.reshape(a))
    tok_sorted = (order // k).astype(jnp.int32)
    w_sorted = w_t.T.reshape(a)[order]
    j = (jnp.arange(n_blocks) * rows - pad_starts[block_expert])[:, None] + jnp.arange(rows)[None, :]
    valid = (j < counts[block_expert][:, None]).reshape(-1)
    src = jnp.clip(starts[block_expert][:, None] + j, 0, a - 1).reshape(-1)
    row_tok = jnp.where(valid, tok_sorted[src], jnp.arange(n_blocks * rows) % t)
    row_w = jnp.where(valid, w_sorted[src], 0.0)
    return pos_t, row_tok, row_w, block_expert, n_active.reshape(1)


def kernel(x, c, w_ada, b_ada, g_mix, g_ffn, w_in, diff_lambda, diff_subln_g, rel_bias, w_o_diff, w_o_moba,
           w_out, w_router, router_bias, w_exp_gate, w_exp_up, w_exp_down, w_sh_gate, w_sh_up, w_sh_down,
           g_final):
    batch, seq, d = x.shape
    t = batch * seq
    depth = w_ada.shape[0]
    assert seq % MOBA_BLOCK == 0 and seq % min(DIFF_TILE, seq) == 0
    hw = 2 * DIFF_HEAD_DIM
    qk_w = DIFF_HEADS * hw
    moba_w = MOBA_HEADS * MOBA_HEAD_DIM
    o_qm = 3 * qk_w
    o_km = o_qm + moba_w
    o_vm = o_km + moba_w
    o_gd = o_vm + moba_w
    o_gm = o_gd + d
    table_diff = rel_bias[:, :DIFF_HEADS].T
    table_moba = rel_bias[:, DIFF_HEADS:].T
    bias_diff = _near_bias(table_diff, min(DIFF_TILE, seq))
    bias_moba = _near_bias(table_moba, MOBA_BLOCK)
    colscale = jnp.ones((1, w_in.shape[2]), F32)
    colscale = colscale.at[:, :qk_w].set(DIFF_HEAD_DIM ** -0.5 * LOG2E)
    colscale = colscale.at[:, o_qm:o_km].set(MOBA_HEAD_DIM ** -0.5 * LOG2E)
    c_pad = jnp.zeros((8, d), F32).at[:batch].set(c)

    assert depth == 1, "single-layer block: the final norm is fused into the last kernel"
    l = 0
    xc = x.reshape(t, d)
    mod3 = _ada(c_pad, w_ada[l], b_ada[l][None, :])[:batch].reshape(batch, 6, d)
    proj = _inproj(xc, mod3, g_mix[l][None, :], w_in[l].astype(BF16), colscale, seq, tm=min(1024, seq))
    lam_init = 0.8 - 0.6 * math.exp(-0.3 * l)
    lv = diff_lambda[l].astype(F32)
    lam = (jnp.exp(jnp.sum(lv[0] * lv[1])) - jnp.exp(jnp.sum(lv[2] * lv[3])) + lam_init).reshape(1)
    td = min(DIFF_TILE, seq)
    qd_t = proj[:, :qk_w].T
    vd_t = proj[:, 2 * qk_w:o_qm].reshape(t // td, td, qk_w).transpose(0, 2, 1)
    od = _diff_attention(proj, qd_t, vd_t, lam, bias_diff, diff_subln_g[l][:, None], batch, seq,
                         1.0 - lam_init, td)
    qm_t = proj[:, o_qm:o_km].T
    vm_t = proj[:, o_vm:o_gd].reshape(t // MOBA_BLOCK, MOBA_BLOCK, moba_w).transpose(0, 2, 1)
    om = _moba_attention(proj, qm_t, vm_t, bias_moba, batch, seq, o_km)
    x1, h2, logits = _mix(od, om, proj, xc, mod3, w_o_diff[l].astype(BF16), w_o_moba[l].astype(BF16),
                          w_out[l].astype(BF16), g_ffn[l][None, :], w_router[l], seq,
                          o_gd // d, o_gm // d)
    sel, eidx_t, w_t = _route_picks(logits.T, router_bias[l][:, None])
    pos_t, row_tok, row_w, block_expert, n_active = _dispatch(sel, eidx_t, w_t, EXPERT_ROWS)
    shared = _shared(h2, w_sh_gate[l].astype(BF16), w_sh_up[l].astype(BF16), w_sh_down[l].astype(BF16))
    n_blocks = row_tok.shape[0] // EXPERT_ROWS
    y_pad = None
    for b0, b1 in _chunk_bounds(n_blocks):
        x_chunk = h2[row_tok[b0 * EXPERT_ROWS:b1 * EXPERT_ROWS]]
        y_pad = _experts(block_expert, n_active, x_chunk, row_w, w_exp_gate[l], w_exp_up[l], w_exp_down[l],
                         y_pad, b0, n_blocks, EXPERT_ROWS)
    out = None
    for u0, u1 in _chunk_bounds(t // FINAL_ROWS):
        t0, t1 = u0 * FINAL_ROWS, u1 * FINAL_ROWS
        idx = pos_t[:, t0:t1].reshape(-1)
        y_tok = y_pad[idx].reshape(TOPK_EXPERTS, t1 - t0, d)
        out = _final(x1, shared, y_tok, mod3, g_final[None, :], out, t0, seq)
    return out.reshape(batch, seq, d)
```

```python
import functools
import math

import jax
import jax.numpy as jnp
from jax import lax
from jax.experimental import pallas as pl
from jax.experimental.pallas import tpu as pltpu

F32 = jnp.float32
BF16 = jnp.bfloat16

DIFF_HEADS = 8
DIFF_HEAD_DIM = 64
MOBA_HEADS = 8
MOBA_HEAD_DIM = 128
MOBA_BLOCK = 256
MOBA_TOPK = 3
REL_BUCKETS = 32
REL_MAX_DIST = 128
N_EXPERTS = 64
N_GROUPS = 8
TOPK_GROUPS = 4
TOPK_EXPERTS = 8
ROUTED_SCALE = 2.5
RMS_EPS = 1e-6
NEG_BIG = -1e30
LOG2E = 1.4426950408889634
DIFF_TILE = 512
DIFF_HEADS_PER_STEP = 4
MOBA_HEADS_PER_STEP = 4
DENOM_ROWS = 16
EXPERT_ROWS = 256
MOE_CHUNK_WEIGHTS = (1, 2, 3, 4)
FINAL_ROWS = 256
VMEM_LIMIT = 56 * 1024 * 1024


def _cparams(*sem):
    return pltpu.CompilerParams(dimension_semantics=sem, vmem_limit_bytes=VMEM_LIMIT)


def _dot(a, b):
    return jnp.dot(a, b, preferred_element_type=F32)


def _ada_kernel(c_ref, w_ref, b_ref, o_ref):
    c = c_ref[...]
    ca = c * jax.nn.sigmoid(c)
    o_ref[...] = jnp.dot(ca, w_ref[...], preferred_element_type=F32,
                         precision=lax.Precision.HIGHEST) + b_ref[...]


def _ada(c_pad, w_ada, b_ada, tn=1024):
    rows, d = c_pad.shape
    n = w_ada.shape[1]
    return pl.pallas_call(
        _ada_kernel,
        grid=(n // tn,),
        in_specs=[pl.BlockSpec((rows, d), lambda j: (0, 0)),
                  pl.BlockSpec((d, tn), lambda j: (0, j)),
                  pl.BlockSpec((1, tn), lambda j: (0, j))],
        out_specs=pl.BlockSpec((rows, tn), lambda j: (0, j)),
        out_shape=jax.ShapeDtypeStruct((rows, n), F32),
        compiler_params=_cparams("arbitrary"),
        name="ada",
    )(c_pad, w_ada, b_ada)


def _modulated_norm(x, g, scale, shift):
    ms = jnp.mean(x * x, axis=-1, keepdims=True)
    return (x * lax.rsqrt(ms + RMS_EPS) * g) * (1.0 + scale) + shift


def _inproj_kernel(x_ref, mod_ref, g_ref, w_ref, cs_ref, o_ref, h_ref, *, chunk):
    @pl.when(pl.program_id(1) == 0)
    def _():
        shift = mod_ref[0, 0:1, :]
        scale = mod_ref[0, 1:2, :]
        g = g_ref[...]

        def body(r, carry):
            rows = pl.ds(pl.multiple_of(r * chunk, chunk), chunk)
            h_ref[rows, :] = _modulated_norm(x_ref[rows, :], g, scale, shift).astype(BF16)
            return carry

        lax.fori_loop(0, x_ref.shape[0] // chunk, body, 0)

    o_ref[...] = (_dot(h_ref[...], w_ref[...]) * cs_ref[...]).astype(BF16)


def _inproj(x2, mod3, g_mix, w_in_bf, colscale, seq, tm=1024, tn=1024):
    t, d = x2.shape
    n = w_in_bf.shape[1]
    per_batch = seq // tm
    return pl.pallas_call(
        functools.partial(_inproj_kernel, chunk=128),
        grid=(t // tm, n // tn),
        in_specs=[pl.BlockSpec((tm, d), lambda i, j: (i, 0)),
                  pl.BlockSpec((1, 6, d), lambda i, j: (i // per_batch, 0, 0)),
                  pl.BlockSpec((1, d), lambda i, j: (0, 0)),
                  pl.BlockSpec((d, tn), lambda i, j: (0, j)),
                  pl.BlockSpec((1, tn), lambda i, j: (0, j))],
        out_specs=pl.BlockSpec((tm, tn), lambda i, j: (i, j)),
        out_shape=jax.ShapeDtypeStruct((t, n), BF16),
        scratch_shapes=[pltpu.VMEM((tm, d), BF16)],
        compiler_params=_cparams("arbitrary", "arbitrary"),
        name="inproj",
    )(x2, mod3, g_mix, w_in_bf, colscale)


def _online_softmax(logits_fn, vt_fn, m, l, acc, heads, first):
    s_all = [logits_fn(g) for g in range(heads)]
    p_all, a_all = [], []
    for g, s in enumerate(s_all):
        smax = jnp.max(s, axis=0, keepdims=True)
        if first:
            mg = smax
            p = jnp.exp2(s - mg)
            if l is not None:
                l[g] = jnp.sum(p, axis=0, keepdims=True)
        else:
            m_old = m[g]
            mg = jnp.maximum(m_old, smax)
            a = jnp.exp2(m_old - mg)
            p = jnp.exp2(s - mg)
            if l is not None:
                l[g] = a * l[g] + jnp.sum(p, axis=0, keepdims=True)
            a_all.append(a)
        m[g] = mg
        p_all.append(p.astype(BF16))
    for g in range(heads):
        vt = vt_fn(g)
        if l is None:
            vt = jnp.concatenate([vt, jnp.ones((DENOM_ROWS, vt.shape[1]), vt.dtype)], axis=0)
        pv = _dot(vt, p_all[g])
        acc[g] = pv if first else a_all[g] * acc[g] + pv


def _pipelined_sweep(n, logits_fn, update_fn, buf0, buf1, heads):
    def fill(buf, c):
        for g in range(heads):
            buf[g] = logits_fn(g, c)

    @pl.when(n >= 1)
    def _():
        fill(buf0, 0)

    def body(i, carry):
        a = 2 * i
        fill(buf1, a + 1)
        update_fn(a, buf0)
        fill(buf0, jnp.minimum(a + 2, n - 1))
        update_fn(a + 1, buf1)
        return carry

    lax.fori_loop(0, n // 2, body, 0)

    @pl.when(n % 2 == 1)
    def _():
        update_fn(n - 1, buf0)


def _diff_kernel(lam_ref, qt_ref, k_ref, vt_ref, bias_ref, g_ref, o_ref, w12, m, acc, s0, s1,
                 *, t, heads, out_scale):
    qi = pl.program_id(2)
    hw = 2 * DIFF_HEAD_DIM
    zeros = jnp.zeros((DIFF_HEAD_DIM, t), BF16)
    for g in range(heads):
        w12[g, :, :t] = jnp.concatenate([qt_ref[g * hw:g * hw + DIFF_HEAD_DIM, :], zeros], axis=0)
        w12[g, :, t:] = jnp.concatenate([zeros, qt_ref[g * hw + DIFF_HEAD_DIM:(g + 1) * hw, :]], axis=0)

    def logits(g, kj):
        rows = pl.ds(pl.multiple_of(kj * t, t), t)
        return _dot(k_ref[rows, g * hw:(g + 1) * hw], w12[g])

    def vt(g, kj):
        return vt_ref[kj, g * hw:(g + 1) * hw, :]

    def both(b):
        return jnp.concatenate([b, b], axis=1)

    def chunk(kj, bias_idx, first):
        def biased(g):
            s = logits(g, kj)
            return s if bias_idx is None else s + both(bias_ref[g, bias_idx])
        _online_softmax(biased, lambda g: vt(g, kj), m, None, acc, heads, first)

    chunk(qi, 0, True)

    @pl.when(qi >= 1)
    def _():
        chunk(qi - 1, 1, False)

    def far_update(kj, buf):
        _online_softmax(lambda g: buf[g], lambda g: vt(g, kj), m, None, acc, heads, False)

    _pipelined_sweep(jnp.maximum(qi - 1, 0), logits, far_update, s0, s1, heads)

    for g in range(heads):
        o12 = acc[g, :hw, :] / acc[g, hw:hw + 1, :]
        o = o12[:, :t] - lam_ref[0] * o12[:, t:]
        ms = jnp.mean(o * o, axis=0, keepdims=True)
        o = (o * lax.rsqrt(ms + RMS_EPS) * g_ref[...]) * out_scale
        o_ref[:, g * hw:(g + 1) * hw] = o.T.astype(BF16)


def _diff_attention(proj, qt, vt, lam, bias, subln_g, batch, seq, out_scale, t, heads=DIFF_HEADS_PER_STEP):
    nq = seq // t
    hw = 2 * DIFF_HEAD_DIM
    gw = heads * hw
    ng = DIFF_HEADS // heads
    kcol = DIFF_HEADS // heads
    once = pl.Buffered(1)
    return pl.pallas_call(
        functools.partial(_diff_kernel, t=t, heads=heads, out_scale=out_scale),
        grid=(batch, ng, nq),
        in_specs=[pl.BlockSpec(memory_space=pltpu.SMEM),
                  pl.BlockSpec((gw, t), lambda b, h, i: (h, b * nq + i)),
                  pl.BlockSpec((seq, gw), lambda b, h, i: (b, kcol + h), pipeline_mode=once),
                  pl.BlockSpec((nq, gw, t), lambda b, h, i: (b, h, 0), pipeline_mode=once),
                  pl.BlockSpec((heads, 2, t, t), lambda b, h, i: (h, 0, 0, 0), pipeline_mode=once),
                  pl.BlockSpec((hw, 1), lambda b, h, i: (0, 0))],
        out_specs=pl.BlockSpec((t, gw), lambda b, h, i: (b * nq + i, h)),
        out_shape=jax.ShapeDtypeStruct((batch * seq, DIFF_HEADS * hw), BF16),
        scratch_shapes=[pltpu.VMEM((heads, hw, 2 * t), BF16),
                        pltpu.VMEM((heads, 1, 2 * t), F32),
                        pltpu.VMEM((heads, hw + DENOM_ROWS, 2 * t), F32),
                        pltpu.VMEM((heads, t, 2 * t), F32), pltpu.VMEM((heads, t, 2 * t), F32)],
        compiler_params=_cparams("arbitrary", "arbitrary", "arbitrary"),
        name="diffattn",
    )(lam, qt, proj, vt, bias, subln_g)


def _moba_kernel(qt_ref, k_ref, vt_ref, bias_ref, o_ref, m, acc, sel_ref, km, s0, s1, *, t, heads):
    cur = pl.program_id(2)
    dh = MOBA_HEAD_DIM
    nb = km.shape[0]

    def cols(g):
        return slice(g * dh, (g + 1) * dh)

    @pl.when(cur == 0)
    def _():
        def block_mean(j, carry):
            rows = pl.ds(pl.multiple_of(j * t, t), t)
            km[pl.ds(j, 1), :] = jnp.mean(k_ref[rows, :].astype(F32), axis=0, keepdims=True)
            return carry
        lax.fori_loop(0, nb, block_mean, 0)

    for g in range(heads):
        gate = jnp.dot(km[:, cols(g)], qt_ref[cols(g), :].astype(F32), preferred_element_type=F32,
                       precision=lax.Precision.HIGHEST)
        blk = lax.broadcasted_iota(jnp.int32, gate.shape, 0)
        gate = jnp.where(blk < cur, gate, -jnp.inf)
        sel = jnp.zeros(gate.shape, F32)
        for _ in range(MOBA_TOPK):
            gmax = jnp.max(gate, axis=0, keepdims=True)
            first = jnp.min(jnp.where(gate == gmax, blk, nb), axis=0, keepdims=True)
            pick = (blk == first) & (gmax > -jnp.inf)
            sel = jnp.where(pick, 1.0, sel)
            gate = jnp.where(blk == first, -jnp.inf, gate)
        sel_ref[g] = sel

    def logits(g, j0, nblk):
        rows = pl.ds(pl.multiple_of(j0 * t, t), nblk * t)
        return _dot(k_ref[rows, cols(g)], qt_ref[cols(g), :])

    def keep(g, j, valid):
        return jnp.where(valid, sel_ref[g, pl.ds(j, 1), :], 0.0) > 0.5

    def pair_vt(g, j0):
        return jnp.concatenate([vt_ref[j0, cols(g), :], vt_ref[j0 + 1, cols(g), :]], axis=1)

    @pl.when(cur == 0)
    def _():
        _online_softmax(lambda g: logits(g, 0, 1) + bias_ref[g, 0], lambda g: vt_ref[0, cols(g), :],
                        m, None, acc, heads, True)

    @pl.when(cur >= 1)
    def _():
        def near(g):
            s = logits(g, cur - 1, 2)
            prev = jnp.where(keep(g, cur - 1, True), s[:t] + bias_ref[g, 1], NEG_BIG)
            return jnp.concatenate([prev, s[t:] + bias_ref[g, 0]], axis=0)
        _online_softmax(near, lambda g: pair_vt(g, cur - 1), m, None, acc, heads, True)

    nfar = jnp.maximum(cur - 1, 0)

    def far_update(c, buf):
        j0 = 2 * c

        def masked(g):
            top = jnp.where(keep(g, j0, True), buf[g, :t, :], NEG_BIG)
            bot = jnp.where(keep(g, j0 + 1, j0 + 1 < nfar), buf[g, t:, :], NEG_BIG)
            return jnp.concatenate([top, bot], axis=0)
        _online_softmax(masked, lambda g: pair_vt(g, j0), m, None, acc, heads, False)

    _pipelined_sweep((nfar + 1) // 2, lambda g, c: logits(g, 2 * c, 2), far_update, s0, s1, heads)
    for g in range(heads):
        o_ref[:, cols(g)] = (acc[g, :dh, :] / acc[g, dh:dh + 1, :]).T.astype(BF16)


def _moba_attention(proj, qt, vt, bias, batch, seq, kcol_w, heads=MOBA_HEADS_PER_STEP):
    t = MOBA_BLOCK
    nq = seq // t
    dh = MOBA_HEAD_DIM
    gw = heads * dh
    ng = MOBA_HEADS // heads
    kcol = kcol_w // gw
    nb = nq
    once = pl.Buffered(1)
    return pl.pallas_call(
        functools.partial(_moba_kernel, t=t, heads=heads),
        grid=(batch, ng, nq),
        in_specs=[pl.BlockSpec((gw, t), lambda b, h, i: (h, b * nq + i)),
                  pl.BlockSpec((seq, gw), lambda b, h, i: (b, kcol + h), pipeline_mode=once),
                  pl.BlockSpec((nq, gw, t), lambda b, h, i: (b, h, 0), pipeline_mode=once),
                  pl.BlockSpec((heads, 2, t, t), lambda b, h, i: (h, 0, 0, 0), pipeline_mode=once)],
        out_specs=pl.BlockSpec((t, gw), lambda b, h, i: (b * nq + i, h)),
        out_shape=jax.ShapeDtypeStruct((batch * seq, MOBA_HEADS * dh), BF16),
        scratch_shapes=[pltpu.VMEM((heads, 1, t), F32),
                        pltpu.VMEM((heads, dh + DENOM_ROWS, t), F32), pltpu.VMEM((heads, nb, t), F32),
                        pltpu.VMEM((nb, gw), F32),
                        pltpu.VMEM((heads, 2 * t, t), F32), pltpu.VMEM((heads, 2 * t, t), F32)],
        compiler_params=_cparams("arbitrary", "arbitrary", "arbitrary"),
        name="moba",
    )(qt, proj, vt, bias)


def _first_index_of_max(vals, ids, sentinel):
    vmax = jnp.max(vals, axis=0, keepdims=True)
    first = jnp.min(jnp.where(vals == vmax, ids, sentinel), axis=0, keepdims=True)
    return vmax, first


def _route(scores_t, bias_t):
    e, n = scores_t.shape
    per = e // N_GROUPS
    selv = scores_t + bias_t
    sub = lax.broadcasted_iota(jnp.int32, (per, n), 0)
    gscore = []
    for g in range(N_GROUPS):
        blk = selv[g * per:(g + 1) * per, :]
        top1, first = _first_index_of_max(blk, sub, per)
        top2 = jnp.max(jnp.where(sub == first, -jnp.inf, blk), axis=0, keepdims=True)
        gscore.append(top1 + top2)
    gs = jnp.concatenate(gscore, axis=0)
    gid = lax.broadcasted_iota(jnp.int32, gs.shape, 0)
    gsel = jnp.zeros(gs.shape, F32)
    for _ in range(TOPK_GROUPS):
        _, first = _first_index_of_max(gs, gid, N_GROUPS)
        gsel = jnp.where(gid == first, 1.0, gsel)
        gs = jnp.where(gid == first, -jnp.inf, gs)
    masked = jnp.concatenate(
        [jnp.where(gsel[g:g + 1, :] > 0.5, selv[g * per:(g + 1) * per, :], -jnp.inf)
         for g in range(N_GROUPS)], axis=0)
    eid = lax.broadcasted_iota(jnp.int32, masked.shape, 0)
    picked = jnp.zeros(masked.shape, jnp.int32)
    ids, vals = [], []
    for _ in range(TOPK_EXPERTS):
        _, first = _first_index_of_max(masked, eid, e)
        hit = eid == first
        picked = jnp.where(hit, 1, picked)
        ids.append(first)
        vals.append(jnp.sum(jnp.where(hit, scores_t, 0.0), axis=0, keepdims=True))
        masked = jnp.where(hit, -jnp.inf, masked)
    w = jnp.concatenate(vals, axis=0)
    w = w / jnp.sum(w, axis=0, keepdims=True) * ROUTED_SCALE
    return picked, jnp.concatenate(ids, axis=0), w


def _route_kernel(logits_ref, rb_ref, sel_ref, idx_ref, w_ref):
    sel_ref[...], idx_ref[...], w_ref[...] = _route(jax.nn.sigmoid(logits_ref[...]), rb_ref[...])


def _route_picks(logits_t, rb_t, tn=2048):
    e, t = logits_t.shape
    tn = min(tn, t)
    k = TOPK_EXPERTS
    return pl.pallas_call(
        _route_kernel,
        grid=(t // tn,),
        in_specs=[pl.BlockSpec((e, tn), lambda i: (0, i)),
                  pl.BlockSpec((e, 1), lambda i: (0, 0))],
        out_specs=[pl.BlockSpec((e, tn), lambda i: (0, i)),
                   pl.BlockSpec((k, tn), lambda i: (0, i)),
                   pl.BlockSpec((k, tn), lambda i: (0, i))],
        out_shape=[jax.ShapeDtypeStruct((e, t), jnp.int32),
                   jax.ShapeDtypeStruct((k, t), jnp.int32),
                   jax.ShapeDtypeStruct((k, t), F32)],
        compiler_params=_cparams("arbitrary"),
        name="route",
    )(logits_t, rb_t)


def _mix_kernel(od_ref, om_ref, gd_ref, gm_ref, x_ref, mod_ref, wod_ref, wom_ref, wout_ref,
                gffn_ref, wrh_ref, wrl_ref, x1_ref, h2_ref, lg_ref, *, parts):
    n = x_ref.shape[0] // parts
    rows = [slice(p * n, (p + 1) * n) for p in range(parts)]
    y = [(_dot(od_ref[r, :], wod_ref[...]), _dot(om_ref[r, :], wom_ref[...])) for r in rows]
    mixed = []
    for r, (yd, ym) in zip(rows, y):
        z = (jax.nn.sigmoid(gd_ref[r, :].astype(F32)) * yd
             + jax.nn.sigmoid(gm_ref[r, :].astype(F32)) * ym)
        mixed.append(_dot(z.astype(BF16), wout_ref[...]))
    for r, mx in zip(rows, mixed):
        x1 = x_ref[r, :] + mod_ref[0, 2:3, :] * mx
        x1_ref[r, :] = x1
        h2 = _modulated_norm(x1, gffn_ref[...], mod_ref[0, 4:5, :], mod_ref[0, 3:4, :])
        h_hi = h2.astype(BF16)
        h2_ref[r, :] = h_hi
        h_lo = (h2 - h_hi.astype(F32)).astype(BF16)
        lg_ref[r, :] = (_dot(h_hi, wrh_ref[...]) + _dot(h_lo, wrh_ref[...])) + _dot(h_hi, wrl_ref[...])


def _mix(od, om, proj, x2, mod3, wod, wom, wout, g_ffn, w_router, seq, gd_col, gm_col, tm=256):
    t, d = x2.shape
    wr_hi = w_router.astype(BF16)
    wr_lo = (w_router - wr_hi.astype(F32)).astype(BF16)
    per_batch = seq // tm
    const = lambda i: (0, 0)
    once = pl.Buffered(1)
    return pl.pallas_call(
        functools.partial(_mix_kernel, parts=2),
        grid=(t // tm,),
        in_specs=[pl.BlockSpec((tm, od.shape[1]), lambda i: (i, 0)),
                  pl.BlockSpec((tm, om.shape[1]), lambda i: (i, 0)),
                  pl.BlockSpec((tm, d), lambda i: (i, gd_col)),
                  pl.BlockSpec((tm, d), lambda i: (i, gm_col)),
                  pl.BlockSpec((tm, d), lambda i: (i, 0)),
                  pl.BlockSpec((1, 6, d), lambda i: (i // per_batch, 0, 0)),
                  pl.BlockSpec(wod.shape, const, pipeline_mode=once),
                  pl.BlockSpec(wom.shape, const, pipeline_mode=once),
                  pl.BlockSpec(wout.shape, const, pipeline_mode=once),
                  pl.BlockSpec((1, d), const),
                  pl.BlockSpec(wr_hi.shape, const),
                  pl.BlockSpec(wr_lo.shape, const)],
        out_specs=[pl.BlockSpec((tm, d), lambda i: (i, 0)),
                   pl.BlockSpec((tm, d), lambda i: (i, 0)),
                   pl.BlockSpec((tm, N_EXPERTS), lambda i: (i, 0))],
        out_shape=[jax.ShapeDtypeStruct((t, d), F32),
                   jax.ShapeDtypeStruct((t, d), BF16),
                   jax.ShapeDtypeStruct((t, N_EXPERTS), F32)],
        compiler_params=_cparams("arbitrary"),
        name="mix",
    )(od, om, proj, proj, x2, mod3, wod, wom, wout, g_ffn, wr_hi, wr_lo)


def _expert_kernel(be_ref, na_ref, x_ref, rw_ref, wg_ref, wu_ref, wd_ref, *rest, base):
    o_ref, wg_s, wu_s, wd_s = rest[-4:]
    i = pl.program_id(0)
    blk = base + i

    @pl.when(blk < na_ref[0])
    def _():
        prev = be_ref[jnp.maximum(blk - 1, 0)]

        @pl.when((i == 0) | (be_ref[blk] != prev))
        def _():
            wg_s[...] = wg_ref[0].astype(BF16)
            wu_s[...] = wu_ref[0].astype(BF16)
            wd_s[...] = wd_ref[0].astype(BF16)

        x = x_ref[...]
        g = _dot(x, wg_s[...])
        u = _dot(x, wu_s[...])
        a = (g * jax.nn.sigmoid(g) * u).astype(BF16)
        y = _dot(a, wd_s[...])
        rows = x.shape[0]
        w_row = rw_ref[0]
        diag = (lax.broadcasted_iota(jnp.int32, (rows, rows), 0)
                == lax.broadcasted_iota(jnp.int32, (rows, rows), 1))
        w_col = jnp.sum(jnp.where(diag, w_row, 0.0), axis=1, keepdims=True)
        o_ref[...] = (y * w_col).astype(BF16)

    @pl.when(blk >= na_ref[0])
    def _():
        o_ref[...] = jnp.zeros(o_ref.shape, o_ref.dtype)


def _experts(block_expert, n_active, x_chunk, row_w, w_gate, w_up, w_down, y_pad, base, n_blocks, rows):
    pc, d = x_chunk.shape
    f = w_gate.shape[2]
    nc = pc // rows

    def local(i, na):
        return jnp.clip(jnp.minimum(base + i, na[0] - 1) - base, 0, nc - 1)

    def xmap(i, be, na):
        return (local(i, na), 0)

    def rmap(i, be, na):
        return (base + local(i, na), 0, 0)

    def wmap(i, be, na):
        return (be[base + local(i, na)], 0, 0)

    in_specs = [pl.BlockSpec((rows, d), xmap),
                pl.BlockSpec((1, 1, rows), rmap),
                pl.BlockSpec((1, d, f), wmap),
                pl.BlockSpec((1, d, f), wmap),
                pl.BlockSpec((1, f, d), wmap)]
    args = [block_expert, n_active, x_chunk, row_w.reshape(n_blocks, 1, rows), w_gate, w_up, w_down]
    aliases = {}
    if y_pad is not None:
        in_specs.append(pl.BlockSpec(memory_space=pl.ANY))
        args.append(y_pad)
        aliases = {len(args) - 1: 0}
    grid_spec = pltpu.PrefetchScalarGridSpec(
        num_scalar_prefetch=2,
        grid=(nc,),
        in_specs=in_specs,
        out_specs=pl.BlockSpec((rows, d), lambda i, be, na: (base + i, 0)),
        scratch_shapes=[pltpu.VMEM((d, f), BF16), pltpu.VMEM((d, f), BF16), pltpu.VMEM((f, d), BF16)],
    )
    return pl.pallas_call(
        functools.partial(_expert_kernel, base=base),
        grid_spec=grid_spec,
        out_shape=jax.ShapeDtypeStruct((n_blocks * rows, d), BF16),
        input_output_aliases=aliases,
        compiler_params=_cparams("arbitrary"),
        name="experts",
    )(*args)


def _shared_kernel(h2_ref, wsg_ref, wsu_ref, wsd_ref, o_ref):
    h2 = h2_ref[...]
    g = _dot(h2, wsg_ref[...])
    u = _dot(h2, wsu_ref[...])
    o_ref[...] = _dot((g * jax.nn.sigmoid(g) * u).astype(BF16), wsd_ref[...]).astype(BF16)


def _shared(h2, wsg, wsu, wsd, tm=512):
    t, d = h2.shape
    tm = min(tm, t)
    const = lambda i: (0, 0)
    row = pl.BlockSpec((tm, d), lambda i: (i, 0))
    return pl.pallas_call(
        _shared_kernel,
        grid=(t // tm,),
        in_specs=[row, pl.BlockSpec(wsg.shape, const), pl.BlockSpec(wsu.shape, const),
                  pl.BlockSpec(wsd.shape, const)],
        out_specs=row,
        out_shape=jax.ShapeDtypeStruct((t, d), BF16),
        compiler_params=_cparams("arbitrary"),
        name="shared",
    )(h2, wsg, wsu, wsd)


def _final_kernel(x1_ref, sh_ref, y_ref, mod_ref, gf_ref, *rest):
    o_ref = rest[-1]
    moe = sh_ref[...].astype(F32)
    for k in range(y_ref.shape[0]):
        moe = moe + y_ref[k].astype(F32)
    x2 = x1_ref[...] + mod_ref[0, 5:6, :] * moe
    ms = jnp.mean(x2 * x2, axis=-1, keepdims=True)
    o_ref[...] = x2 * lax.rsqrt(ms + RMS_EPS) * gf_ref[...]


def _final(x1, shared, y_tok, mod3, g_final, out, base_tok, seq, tm=FINAL_ROWS):
    t, d = x1.shape
    nk, tc, _ = y_tok.shape
    per_batch = seq // tm
    b0 = base_tok // tm
    row = pl.BlockSpec((tm, d), lambda i: (b0 + i, 0))
    in_specs = [row, row,
                pl.BlockSpec((nk, tm, d), lambda i: (0, i, 0)),
                pl.BlockSpec((1, 6, d), lambda i: ((b0 + i) // per_batch, 0, 0)),
                pl.BlockSpec((1, d), lambda i: (0, 0))]
    args = [x1, shared, y_tok, mod3, g_final]
    aliases = {}
    if out is not None:
        in_specs.append(pl.BlockSpec(memory_space=pl.ANY))
        args.append(out)
        aliases = {len(args) - 1: 0}
    return pl.pallas_call(
        _final_kernel,
        grid=(tc // tm,),
        in_specs=in_specs,
        out_specs=row,
        out_shape=jax.ShapeDtypeStruct((t, d), F32),
        input_output_aliases=aliases,
        compiler_params=_cparams("arbitrary"),
        name="final",
    )(*args)


def _chunk_bounds(n):
    total = sum(MOE_CHUNK_WEIGHTS)
    edges = [0]
    for i in range(len(MOE_CHUNK_WEIGHTS)):
        edges.append(round(n * sum(MOE_CHUNK_WEIGHTS[:i + 1]) / total))
    return [(a, b) for a, b in zip(edges[:-1], edges[1:]) if b > a]


def _bucket(rel):
    n = jnp.maximum(rel, 0)
    max_exact = REL_BUCKETS // 2
    nf = jnp.maximum(n, 1).astype(F32)
    large = max_exact + (jnp.log(nf / max_exact) / math.log(REL_MAX_DIST / max_exact)
                         * (REL_BUCKETS - max_exact)).astype(jnp.int32)
    large = jnp.minimum(large, REL_BUCKETS - 1)
    return jnp.where(n < max_exact, n, large)


def _bias_kernel(tab_ref, o_ref, *, t):
    h = pl.program_id(0)
    sub = pl.program_id(1)
    key = lax.broadcasted_iota(jnp.int32, (t, t), 0)
    qry = lax.broadcasted_iota(jnp.int32, (t, t), 1)
    rel = qry - key + sub * t
    bucket = _bucket(rel)
    far = tab_ref[h, REL_BUCKETS - 1]
    bias = jnp.zeros((t, t), F32)
    for b in range(REL_BUCKETS - 1):
        bias = jnp.where(bucket == b, tab_ref[h, b] - far, bias)
    o_ref[0, 0] = jnp.where(rel >= 0, bias * LOG2E, NEG_BIG)


def _near_bias(table, t):
    assert t >= REL_MAX_DIST
    heads = table.shape[0]
    return pl.pallas_call(
        functools.partial(_bias_kernel, t=t),
        grid=(heads, 2),
        in_specs=[pl.BlockSpec(memory_space=pltpu.SMEM)],
        out_specs=pl.BlockSpec((1, 1, t, t), lambda h, s: (h, s, 0, 0)),
        out_shape=jax.ShapeDtypeStruct((heads, 2, t, t), F32),
        compiler_params=_cparams("arbitrary", "arbitrary"),
        name="relbias",
    )(table)


def _dispatch(sel, eidx_t, w_t, rows):
    e, t = sel.shape
    k = eidx_t.shape[0]
    a = t * k
    eidx = eidx_t.T
    counts = jnp.sum(sel, axis=1)
    rank = jnp.cumsum(sel, axis=1) - sel
    padded = ((counts + rows - 1) // rows) * rows
    pad_ends = jnp.cumsum(padded)
    pad_starts = pad_ends - padded
    starts = jnp.cumsum(counts) - counts
    dest = rank + pad_starts[:, None]
    pos_t = jnp.sum(jnp.where(eidx_t[:, None, :] == jnp.arange(e)[None, :, None], dest[None], 0), axis=1)
    n_blocks = a // rows + e
    n_active = (pad_ends[-1] // rows).astype(jnp.int32)
    end_blocks = pad_ends // rows
    block_expert = jnp.sum(jnp.arange(n_blocks)[:, None] >= end_blocks[None, :], axis=1)
    block_expert = jnp.minimum(block_expert, e - 1).astype(jnp.int32)
    order = jnp.argsort(eidx.reshape(a))
    tok_sorted = (order // k).astype(jnp.int32)
    w_sorted = w_t.T.reshape(a)[order]
    j = (jnp.arange(n_blocks) * rows - pad_starts[block_expert])[:, None] + jnp.arange(rows)[None, :]
    valid = (j < counts[block_expert][:, None]).reshape(-1)
    src = jnp.clip(starts[block_expert][:, None] + j, 0, a - 1).reshape(-1)
    row_tok = jnp.where(valid, tok_sorted[src], jnp.arange(n_blocks * rows) % t)
    row_w = jnp.where(valid, w_sorted[src], 0.0)
    return pos_t, row_tok, row_w, block_expert, n_active.reshape(1)


def kernel(x, c, w_ada, b_ada, g_mix, g_ffn, w_in, diff_lambda, diff_subln_g, rel_bias, w_o_diff, w_o_moba,
           w_out, w_router, router_bias, w_exp_gate, w_exp_up, w_exp_down, w_sh_gate, w_sh_up, w_sh_down,
           g_final):
    batch, seq, d = x.shape
    t = batch * seq
    depth = w_ada.shape[0]
    assert seq % MOBA_BLOCK == 0 and seq % min(DIFF_TILE, seq) == 0
    hw = 2 * DIFF_HEAD_DIM
    qk_w = DIFF_HEADS * hw
    moba_w = MOBA_HEADS * MOBA_HEAD_DIM
    o_qm = 3 * qk_w
    o_km = o_qm + moba_w
    o_vm = o_km + moba_w
    o_gd = o_vm + moba_w
    o_gm = o_gd + d
    table_diff = rel_bias[:, :DIFF_HEADS].T
    table_moba = rel_bias[:, DIFF_HEADS:].T
    bias_diff = _near_bias(table_diff, min(DIFF_TILE, seq))
    bias_moba = _near_bias(table_moba, MOBA_BLOCK)
    colscale = jnp.ones((1, w_in.shape[2]), F32)
    colscale = colscale.at[:, :qk_w].set(DIFF_HEAD_DIM ** -0.5 * LOG2E)
    colscale = colscale.at[:, o_qm:o_km].set(MOBA_HEAD_DIM ** -0.5 * LOG2E)
    c_pad = jnp.zeros((8, d), F32).at[:batch].set(c)

    assert depth == 1, "single-layer block: the final norm is fused into the last kernel"
    l = 0
    xc = x.reshape(t, d)
    mod3 = _ada(c_pad, w_ada[l], b_ada[l][None, :])[:batch].reshape(batch, 6, d)
    proj = _inproj(xc, mod3, g_mix[l][None, :], w_in[l].astype(BF16), colscale, seq, tm=min(1024, seq))
    lam_init = 0.8 - 0.6 * math.exp(-0.3 * l)
    lv = diff_lambda[l].astype(F32)
    lam = (jnp.exp(jnp.sum(lv[0] * lv[1])) - jnp.exp(jnp.sum(lv[2] * lv[3])) + lam_init).reshape(1)
    td = min(DIFF_TILE, seq)
    qd_t = proj[:, :qk_w].T
    vd_t = proj[:, 2 * qk_w:o_qm].reshape(t // td, td, qk_w).transpose(0, 2, 1)
    od = _diff_attention(proj, qd_t, vd_t, lam, bias_diff, diff_subln_g[l][:, None], batch, seq,
                         1.0 - lam_init, td)
    qm_t = proj[:, o_qm:o_km].T
    vm_t = proj[:, o_vm:o_gd].reshape(t // MOBA_BLOCK, MOBA_BLOCK, moba_w).transpose(0, 2, 1)
    om = _moba_attention(proj, qm_t, vm_t, bias_moba, batch, seq, o_km)
    x1, h2, logits = _mix(od, om, proj, xc, mod3, w_o_diff[l].astype(BF16), w_o_moba[l].astype(BF16),
                          w_out[l].astype(BF16), g_ffn[l][None, :], w_router[l], seq,
                          o_gd // d, o_gm // d)
    sel, eidx_t, w_t = _route_picks(logits.T, router_bias[l][:, None])
    pos_t, row_tok, row_w, block_expert, n_active = _dispatch(sel, eidx_t, w_t, EXPERT_ROWS)
    shared = _shared(h2, w_sh_gate[l].astype(BF16), w_sh_up[l].astype(BF16), w_sh_down[l].astype(BF16))
    n_blocks = row_tok.shape[0] // EXPERT_ROWS
    y_pad = None
    for b0, b1 in _chunk_bounds(n_blocks):
        x_chunk = h2[row_tok[b0 * EXPERT_ROWS:b1 * EXPERT_ROWS]]
        y_pad = _experts(block_expert, n_active, x_chunk, row_w, w_exp_gate[l], w_exp_up[l], w_exp_down[l],
                         y_pad, b0, n_blocks, EXPERT_ROWS)
    out = None
    for u0, u1 in _chunk_bounds(t // FINAL_ROWS):
        t0, t1 = u0 * FINAL_ROWS, u1 * FINAL_ROWS
        idx = pos_t[:, t0:t1].reshape(-1)
        y_tok = y_pad[idx].reshape(TOPK_EXPERTS, t1 - t0, d)
        out = _final(x1, shared, y_tok, mod3, g_final[None, :], out, t0, seq)
    return out.reshape(batch, seq, d)
```

```python
import functools
import math

import jax
import jax.numpy as jnp
from jax import lax
from jax.experimental import pallas as pl
from jax.experimental.pallas import tpu as pltpu

F32 = jnp.float32
BF16 = jnp.bfloat16

DIFF_HEADS = 8
DIFF_HEAD_DIM = 64
MOBA_HEADS = 8
MOBA_HEAD_DIM = 128
MOBA_BLOCK = 256
MOBA_TOPK = 3
REL_BUCKETS = 32
REL_MAX_DIST = 128
N_EXPERTS = 64
N_GROUPS = 8
TOPK_GROUPS = 4
TOPK_EXPERTS = 8
ROUTED_SCALE = 2.5
RMS_EPS = 1e-6
NEG_BIG = -1e30
LOG2E = 1.4426950408889634
DIFF_TILE = 512
DIFF_HEADS_PER_STEP = 4
MOBA_HEADS_PER_STEP = 4
DENOM_ROWS = 16
EXPERT_ROWS = 512
MOE_CHUNK_WEIGHTS = (1, 2, 3, 4)
FINAL_ROWS = 256
VMEM_LIMIT = 56 * 1024 * 1024


def _cparams(*sem):
    return pltpu.CompilerParams(dimension_semantics=sem, vmem_limit_bytes=VMEM_LIMIT)


def _dot(a, b):
    return jnp.dot(a, b, preferred_element_type=F32)


def _ada_kernel(c_ref, w_ref, b_ref, o_ref):
    c = c_ref[...]
    ca = c * jax.nn.sigmoid(c)
    o_ref[...] = jnp.dot(ca, w_ref[...], preferred_element_type=F32,
                         precision=lax.Precision.HIGHEST) + b_ref[...]


def _ada(c_pad, w_ada, b_ada, tn=1024):
    rows, d = c_pad.shape
    n = w_ada.shape[1]
    return pl.pallas_call(
        _ada_kernel,
        grid=(n // tn,),
        in_specs=[pl.BlockSpec((rows, d), lambda j: (0, 0)),
                  pl.BlockSpec((d, tn), lambda j: (0, j)),
                  pl.BlockSpec((1, tn), lambda j: (0, j))],
        out_specs=pl.BlockSpec((rows, tn), lambda j: (0, j)),
        out_shape=jax.ShapeDtypeStruct((rows, n), F32),
        compiler_params=_cparams("arbitrary"),
        name="ada",
    )(c_pad, w_ada, b_ada)


def _modulated_norm(x, g, scale, shift):
    ms = jnp.mean(x * x, axis=-1, keepdims=True)
    return (x * lax.rsqrt(ms + RMS_EPS) * g) * (1.0 + scale) + shift


def _inproj_kernel(x_ref, mod_ref, g_ref, w_ref, cs_ref, o_ref, h_ref, *, chunk):
    @pl.when(pl.program_id(1) == 0)
    def _():
        shift = mod_ref[0, 0:1, :]
        scale = mod_ref[0, 1:2, :]
        g = g_ref[...]

        def body(r, carry):
            rows = pl.ds(pl.multiple_of(r * chunk, chunk), chunk)
            h_ref[rows, :] = _modulated_norm(x_ref[rows, :], g, scale, shift).astype(BF16)
            return carry

        lax.fori_loop(0, x_ref.shape[0] // chunk, body, 0)

    o_ref[...] = (_dot(h_ref[...], w_ref[...]) * cs_ref[...]).astype(BF16)


def _inproj(x2, mod3, g_mix, w_in_bf, colscale, seq, tm=1024, tn=1024):
    t, d = x2.shape
    n = w_in_bf.shape[1]
    per_batch = seq // tm
    return pl.pallas_call(
        functools.partial(_inproj_kernel, chunk=128),
        grid=(t // tm, n // tn),
        in_specs=[pl.BlockSpec((tm, d), lambda i, j: (i, 0)),
                  pl.BlockSpec((1, 6, d), lambda i, j: (i // per_batch, 0, 0)),
                  pl.BlockSpec((1, d), lambda i, j: (0, 0)),
                  pl.BlockSpec((d, tn), lambda i, j: (0, j)),
                  pl.BlockSpec((1, tn), lambda i, j: (0, j))],
        out_specs=pl.BlockSpec((tm, tn), lambda i, j: (i, j)),
        out_shape=jax.ShapeDtypeStruct((t, n), BF16),
        scratch_shapes=[pltpu.VMEM((tm, d), BF16)],
        compiler_params=_cparams("arbitrary", "arbitrary"),
        name="inproj",
    )(x2, mod3, g_mix, w_in_bf, colscale)


def _online_softmax(logits_fn, vt_fn, m, l, acc, heads, first):
    s_all = [logits_fn(g) for g in range(heads)]
    p_all, a_all = [], []
    for g, s in enumerate(s_all):
        smax = jnp.max(s, axis=0, keepdims=True)
        if first:
            mg = smax
            p = jnp.exp2(s - mg)
            if l is not None:
                l[g] = jnp.sum(p, axis=0, keepdims=True)
        else:
            m_old = m[g]
            mg = jnp.maximum(m_old, smax)
            a = jnp.exp2(m_old - mg)
            p = jnp.exp2(s - mg)
            if l is not None:
                l[g] = a * l[g] + jnp.sum(p, axis=0, keepdims=True)
            a_all.append(a)
        m[g] = mg
        p_all.append(p.astype(BF16))
    for g in range(heads):
        vt = vt_fn(g)
        if l is None:
            vt = jnp.concatenate([vt, jnp.ones((DENOM_ROWS, vt.shape[1]), vt.dtype)], axis=0)
        pv = _dot(vt, p_all[g])
        acc[g] = pv if first else a_all[g] * acc[g] + pv


def _pipelined_sweep(n, logits_fn, update_fn, buf0, buf1, heads):
    def fill(buf, c):
        for g in range(heads):
            buf[g] = logits_fn(g, c)

    @pl.when(n >= 1)
    def _():
        fill(buf0, 0)

    def body(i, carry):
        a = 2 * i
        fill(buf1, a + 1)
        update_fn(a, buf0)
        fill(buf0, jnp.minimum(a + 2, n - 1))
        update_fn(a + 1, buf1)
        return carry

    lax.fori_loop(0, n // 2, body, 0)

    @pl.when(n % 2 == 1)
    def _():
        update_fn(n - 1, buf0)


def _diff_kernel(lam_ref, qt_ref, k_ref, vt_ref, bias_ref, g_ref, o_ref, w12, m, acc, s0, s1,
                 *, t, heads, out_scale):
    qi = pl.program_id(2)
    hw = 2 * DIFF_HEAD_DIM
    zeros = jnp.zeros((DIFF_HEAD_DIM, t), BF16)
    for g in range(heads):
        w12[g, :, :t] = jnp.concatenate([qt_ref[g * hw:g * hw + DIFF_HEAD_DIM, :], zeros], axis=0)
        w12[g, :, t:] = jnp.concatenate([zeros, qt_ref[g * hw + DIFF_HEAD_DIM:(g + 1) * hw, :]], axis=0)

    def logits(g, kj):
        rows = pl.ds(pl.multiple_of(kj * t, t), t)
        return _dot(k_ref[rows, g * hw:(g + 1) * hw], w12[g])

    def vt(g, kj):
        return vt_ref[kj, g * hw:(g + 1) * hw, :]

    def both(b):
        return jnp.concatenate([b, b], axis=1)

    def chunk(kj, bias_idx, first):
        def biased(g):
            s = logits(g, kj)
            return s if bias_idx is None else s + both(bias_ref[g, bias_idx])
        _online_softmax(biased, lambda g: vt(g, kj), m, None, acc, heads, first)

    chunk(qi, 0, True)

    @pl.when(qi >= 1)
    def _():
        chunk(qi - 1, 1, False)

    def far_update(kj, buf):
        _online_softmax(lambda g: buf[g], lambda g: vt(g, kj), m, None, acc, heads, False)

    _pipelined_sweep(jnp.maximum(qi - 1, 0), logits, far_update, s0, s1, heads)

    for g in range(heads):
        o12 = acc[g, :hw, :] / acc[g, hw:hw + 1, :]
        o = o12[:, :t] - lam_ref[0] * o12[:, t:]
        ms = jnp.mean(o * o, axis=0, keepdims=True)
        o = (o * lax.rsqrt(ms + RMS_EPS) * g_ref[...]) * out_scale
        o_ref[:, g * hw:(g + 1) * hw] = o.T.astype(BF16)


def _diff_attention(proj, qt, vt, lam, bias, subln_g, batch, seq, out_scale, t, heads=DIFF_HEADS_PER_STEP):
    nq = seq // t
    hw = 2 * DIFF_HEAD_DIM
    gw = heads * hw
    ng = DIFF_HEADS // heads
    kcol = DIFF_HEADS // heads
    once = pl.Buffered(1)
    return pl.pallas_call(
        functools.partial(_diff_kernel, t=t, heads=heads, out_scale=out_scale),
        grid=(batch, ng, nq),
        in_specs=[pl.BlockSpec(memory_space=pltpu.SMEM),
                  pl.BlockSpec((gw, t), lambda b, h, i: (h, b * nq + i)),
                  pl.BlockSpec((seq, gw), lambda b, h, i: (b, kcol + h), pipeline_mode=once),
                  pl.BlockSpec((nq, gw, t), lambda b, h, i: (b, h, 0), pipeline_mode=once),
                  pl.BlockSpec((heads, 2, t, t), lambda b, h, i: (h, 0, 0, 0), pipeline_mode=once),
                  pl.BlockSpec((hw, 1), lambda b, h, i: (0, 0))],
        out_specs=pl.BlockSpec((t, gw), lambda b, h, i: (b * nq + i, h)),
        out_shape=jax.ShapeDtypeStruct((batch * seq, DIFF_HEADS * hw), BF16),
        scratch_shapes=[pltpu.VMEM((heads, hw, 2 * t), BF16),
                        pltpu.VMEM((heads, 1, 2 * t), F32),
                        pltpu.VMEM((heads, hw + DENOM_ROWS, 2 * t), F32),
                        pltpu.VMEM((heads, t, 2 * t), F32), pltpu.VMEM((heads, t, 2 * t), F32)],
        compiler_params=_cparams("arbitrary", "arbitrary", "arbitrary"),
        name="diffattn",
    )(lam, qt, proj, vt, bias, subln_g)


def _moba_kernel(qt_ref, k_ref, vt_ref, bias_ref, o_ref, m, acc, sel_ref, km, s0, s1, *, t, heads):
    cur = pl.program_id(2)
    dh = MOBA_HEAD_DIM
    nb = km.shape[0]

    def cols(g):
        return slice(g * dh, (g + 1) * dh)

    @pl.when(cur == 0)
    def _():
        def block_mean(j, carry):
            rows = pl.ds(pl.multiple_of(j * t, t), t)
            km[pl.ds(j, 1), :] = jnp.mean(k_ref[rows, :].astype(F32), axis=0, keepdims=True)
            return carry
        lax.fori_loop(0, nb, block_mean, 0)

    for g in range(heads):
        gate = jnp.dot(km[:, cols(g)], qt_ref[cols(g), :].astype(F32), preferred_element_type=F32,
                       precision=lax.Precision.HIGHEST)
        blk = lax.broadcasted_iota(jnp.int32, gate.shape, 0)
        gate = jnp.where(blk < cur, gate, -jnp.inf)
        sel = jnp.zeros(gate.shape, F32)
        for _ in range(MOBA_TOPK):
            gmax = jnp.max(gate, axis=0, keepdims=True)
            first = jnp.min(jnp.where(gate == gmax, blk, nb), axis=0, keepdims=True)
            pick = (blk == first) & (gmax > -jnp.inf)
            sel = jnp.where(pick, 1.0, sel)
            gate = jnp.where(blk == first, -jnp.inf, gate)
        sel_ref[g] = sel

    def logits(g, j0, nblk):
        rows = pl.ds(pl.multiple_of(j0 * t, t), nblk * t)
        return _dot(k_ref[rows, cols(g)], qt_ref[cols(g), :])

    def keep(g, j, valid):
        return jnp.where(valid, sel_ref[g, pl.ds(j, 1), :], 0.0) > 0.5

    def pair_vt(g, j0):
        return jnp.concatenate([vt_ref[j0, cols(g), :], vt_ref[j0 + 1, cols(g), :]], axis=1)

    @pl.when(cur == 0)
    def _():
        _online_softmax(lambda g: logits(g, 0, 1) + bias_ref[g, 0], lambda g: vt_ref[0, cols(g), :],
                        m, None, acc, heads, True)

    @pl.when(cur >= 1)
    def _():
        def near(g):
            s = logits(g, cur - 1, 2)
            prev = jnp.where(keep(g, cur - 1, True), s[:t] + bias_ref[g, 1], NEG_BIG)
            return jnp.concatenate([prev, s[t:] + bias_ref[g, 0]], axis=0)
        _online_softmax(near, lambda g: pair_vt(g, cur - 1), m, None, acc, heads, True)

    nfar = jnp.maximum(cur - 1, 0)

    def far_update(c, buf):
        j0 = 2 * c

        def masked(g):
            top = jnp.where(keep(g, j0, True), buf[g, :t, :], NEG_BIG)
            bot = jnp.where(keep(g, j0 + 1, j0 + 1 < nfar), buf[g, t:, :], NEG_BIG)
            return jnp.concatenate([top, bot], axis=0)
        _online_softmax(masked, lambda g: pair_vt(g, j0), m, None, acc, heads, False)

    _pipelined_sweep((nfar + 1) // 2, lambda g, c: logits(g, 2 * c, 2), far_update, s0, s1, heads)
    for g in range(heads):
        o_ref[:, cols(g)] = (acc[g, :dh, :] / acc[g, dh:dh + 1, :]).T.astype(BF16)


def _moba_attention(proj, qt, vt, bias, batch, seq, kcol_w, heads=MOBA_HEADS_PER_STEP):
    t = MOBA_BLOCK
    nq = seq // t
    dh = MOBA_HEAD_DIM
    gw = heads * dh
    ng = MOBA_HEADS // heads
    kcol = kcol_w // gw
    nb = nq
    once = pl.Buffered(1)
    return pl.pallas_call(
        functools.partial(_moba_kernel, t=t, heads=heads),
        grid=(batch, ng, nq),
        in_specs=[pl.BlockSpec((gw, t), lambda b, h, i: (h, b * nq + i)),
                  pl.BlockSpec((seq, gw), lambda b, h, i: (b, kcol + h), pipeline_mode=once),
                  pl.BlockSpec((nq, gw, t), lambda b, h, i: (b, h, 0), pipeline_mode=once),
                  pl.BlockSpec((heads, 2, t, t), lambda b, h, i: (h, 0, 0, 0), pipeline_mode=once)],
        out_specs=pl.BlockSpec((t, gw), lambda b, h, i: (b * nq + i, h)),
        out_shape=jax.ShapeDtypeStruct((batch * seq, MOBA_HEADS * dh), BF16),
        scratch_shapes=[pltpu.VMEM((heads, 1, t), F32),
                        pltpu.VMEM((heads, dh + DENOM_ROWS, t), F32), pltpu.VMEM((heads, nb, t), F32),
                        pltpu.VMEM((nb, gw), F32),
                        pltpu.VMEM((heads, 2 * t, t), F32), pltpu.VMEM((heads, 2 * t, t), F32)],
        compiler_params=_cparams("arbitrary", "arbitrary", "arbitrary"),
        name="moba",
    )(qt, proj, vt, bias)


def _first_index_of_max(vals, ids, sentinel):
    vmax = jnp.max(vals, axis=0, keepdims=True)
    first = jnp.min(jnp.where(vals == vmax, ids, sentinel), axis=0, keepdims=True)
    return vmax, first


def _route(scores_t, bias_t):
    e, n = scores_t.shape
    per = e // N_GROUPS
    selv = scores_t + bias_t
    sub = lax.broadcasted_iota(jnp.int32, (per, n), 0)
    gscore = []
    for g in range(N_GROUPS):
        blk = selv[g * per:(g + 1) * per, :]
        top1, first = _first_index_of_max(blk, sub, per)
        top2 = jnp.max(jnp.where(sub == first, -jnp.inf, blk), axis=0, keepdims=True)
        gscore.append(top1 + top2)
    gs = jnp.concatenate(gscore, axis=0)
    gid = lax.broadcasted_iota(jnp.int32, gs.shape, 0)
    gsel = jnp.zeros(gs.shape, F32)
    for _ in range(TOPK_GROUPS):
        _, first = _first_index_of_max(gs, gid, N_GROUPS)
        gsel = jnp.where(gid == first, 1.0, gsel)
        gs = jnp.where(gid == first, -jnp.inf, gs)
    masked = jnp.concatenate(
        [jnp.where(gsel[g:g + 1, :] > 0.5, selv[g * per:(g + 1) * per, :], -jnp.inf)
         for g in range(N_GROUPS)], axis=0)
    eid = lax.broadcasted_iota(jnp.int32, masked.shape, 0)
    picked = jnp.zeros(masked.shape, jnp.int32)
    ids, vals = [], []
    for _ in range(TOPK_EXPERTS):
        _, first = _first_index_of_max(masked, eid, e)
        hit = eid == first
        picked = jnp.where(hit, 1, picked)
        ids.append(first)
        vals.append(jnp.sum(jnp.where(hit, scores_t, 0.0), axis=0, keepdims=True))
        masked = jnp.where(hit, -jnp.inf, masked)
    w = jnp.concatenate(vals, axis=0)
    w = w / jnp.sum(w, axis=0, keepdims=True) * ROUTED_SCALE
    return picked, jnp.concatenate(ids, axis=0), w


def _route_kernel(logits_ref, rb_ref, sel_ref, idx_ref, w_ref):
    sel_ref[...], idx_ref[...], w_ref[...] = _route(jax.nn.sigmoid(logits_ref[...]), rb_ref[...])


def _route_picks(logits_t, rb_t, tn=2048):
    e, t = logits_t.shape
    tn = min(tn, t)
    k = TOPK_EXPERTS
    return pl.pallas_call(
        _route_kernel,
        grid=(t // tn,),
        in_specs=[pl.BlockSpec((e, tn), lambda i: (0, i)),
                  pl.BlockSpec((e, 1), lambda i: (0, 0))],
        out_specs=[pl.BlockSpec((e, tn), lambda i: (0, i)),
                   pl.BlockSpec((k, tn), lambda i: (0, i)),
                   pl.BlockSpec((k, tn), lambda i: (0, i))],
        out_shape=[jax.ShapeDtypeStruct((e, t), jnp.int32),
                   jax.ShapeDtypeStruct((k, t), jnp.int32),
                   jax.ShapeDtypeStruct((k, t), F32)],
        compiler_params=_cparams("arbitrary"),
        name="route",
    )(logits_t, rb_t)


def _mix_kernel(od_ref, om_ref, gd_ref, gm_ref, x_ref, mod_ref, wod_ref, wom_ref, wout_ref,
                gffn_ref, wrh_ref, wrl_ref, x1_ref, h2_ref, lg_ref, *, parts):
    n = x_ref.shape[0] // parts
    rows = [slice(p * n, (p + 1) * n) for p in range(parts)]
    y = [(_dot(od_ref[r, :], wod_ref[...]), _dot(om_ref[r, :], wom_ref[...])) for r in rows]
    mixed = []
    for r, (yd, ym) in zip(rows, y):
        z = (jax.nn.sigmoid(gd_ref[r, :].astype(F32)) * yd
             + jax.nn.sigmoid(gm_ref[r, :].astype(F32)) * ym)
        mixed.append(_dot(z.astype(BF16), wout_ref[...]))
    for r, mx in zip(rows, mixed):
        x1 = x_ref[r, :] + mod_ref[0, 2:3, :] * mx
        x1_ref[r, :] = x1
        h2 = _modulated_norm(x1, gffn_ref[...], mod_ref[0, 4:5, :], mod_ref[0, 3:4, :])
        h_hi = h2.astype(BF16)
        h2_ref[r, :] = h_hi
        h_lo = (h2 - h_hi.astype(F32)).astype(BF16)
        lg_ref[r, :] = (_dot(h_hi, wrh_ref[...]) + _dot(h_lo, wrh_ref[...])) + _dot(h_hi, wrl_ref[...])


def _mix(od, om, proj, x2, mod3, wod, wom, wout, g_ffn, w_router, seq, gd_col, gm_col, tm=256):
    t, d = x2.shape
    wr_hi = w_router.astype(BF16)
    wr_lo = (w_router - wr_hi.astype(F32)).astype(BF16)
    per_batch = seq // tm
    const = lambda i: (0, 0)
    once = pl.Buffered(1)
    return pl.pallas_call(
        functools.partial(_mix_kernel, parts=2),
        grid=(t // tm,),
        in_specs=[pl.BlockSpec((tm, od.shape[1]), lambda i: (i, 0)),
                  pl.BlockSpec((tm, om.shape[1]), lambda i: (i, 0)),
                  pl.BlockSpec((tm, d), lambda i: (i, gd_col)),
                  pl.BlockSpec((tm, d), lambda i: (i, gm_col)),
                  pl.BlockSpec((tm, d), lambda i: (i, 0)),
                  pl.BlockSpec((1, 6, d), lambda i: (i // per_batch, 0, 0)),
                  pl.BlockSpec(wod.shape, const, pipeline_mode=once),
                  pl.BlockSpec(wom.shape, const, pipeline_mode=once),
                  pl.BlockSpec(wout.shape, const, pipeline_mode=once),
                  pl.BlockSpec((1, d), const),
                  pl.BlockSpec(wr_hi.shape, const),
                  pl.BlockSpec(wr_lo.shape, const)],
        out_specs=[pl.BlockSpec((tm, d), lambda i: (i, 0)),
                   pl.BlockSpec((tm, d), lambda i: (i, 0)),
                   pl.BlockSpec((tm, N_EXPERTS), lambda i: (i, 0))],
        out_shape=[jax.ShapeDtypeStruct((t, d), F32),
                   jax.ShapeDtypeStruct((t, d), BF16),
                   jax.ShapeDtypeStruct((t, N_EXPERTS), F32)],
        compiler_params=_cparams("arbitrary"),
        name="mix",
    )(od, om, proj, proj, x2, mod3, wod, wom, wout, g_ffn, wr_hi, wr_lo)


def _expert_kernel(be_ref, na_ref, x_ref, wg_ref, wu_ref, wd_ref, *rest, base):
    o_ref, wg_s, wu_s, wd_s = rest[-4:]
    i = pl.program_id(0)
    blk = base + i

    @pl.when(blk < na_ref[0])
    def _():
        prev = be_ref[jnp.maximum(blk - 1, 0)]

        @pl.when((i == 0) | (be_ref[blk] != prev))
        def _():
            wg_s[...] = wg_ref[0].astype(BF16)
            wu_s[...] = wu_ref[0].astype(BF16)
            wd_s[...] = wd_ref[0].astype(BF16)

        x = x_ref[...]
        g = _dot(x, wg_s[...])
        u = _dot(x, wu_s[...])
        a = (g * jax.nn.sigmoid(g) * u).astype(BF16)
        o_ref[...] = _dot(a, wd_s[...]).astype(BF16)

    @pl.when(blk >= na_ref[0])
    def _():
        o_ref[...] = jnp.zeros(o_ref.shape, o_ref.dtype)


def _experts(block_expert, n_active, x_chunk, w_gate, w_up, w_down, y_pad, base, n_blocks, rows):
    pc, d = x_chunk.shape
    f = w_gate.shape[2]
    nc = pc // rows

    def local(i, na):
        return jnp.clip(jnp.minimum(base + i, na[0] - 1) - base, 0, nc - 1)

    def xmap(i, be, na):
        return (local(i, na), 0)

    def wmap(i, be, na):
        return (be[base + local(i, na)], 0, 0)

    in_specs = [pl.BlockSpec((rows, d), xmap),
                pl.BlockSpec((1, d, f), wmap),
                pl.BlockSpec((1, d, f), wmap),
                pl.BlockSpec((1, f, d), wmap)]
    args = [block_expert, n_active, x_chunk, w_gate, w_up, w_down]
    aliases = {}
    if y_pad is not None:
        in_specs.append(pl.BlockSpec(memory_space=pl.ANY))
        args.append(y_pad)
        aliases = {len(args) - 1: 0}
    grid_spec = pltpu.PrefetchScalarGridSpec(
        num_scalar_prefetch=2,
        grid=(nc,),
        in_specs=in_specs,
        out_specs=pl.BlockSpec((rows, d), lambda i, be, na: (base + i, 0)),
        scratch_shapes=[pltpu.VMEM((d, f), BF16), pltpu.VMEM((d, f), BF16), pltpu.VMEM((f, d), BF16)],
    )
    return pl.pallas_call(
        functools.partial(_expert_kernel, base=base),
        grid_spec=grid_spec,
        out_shape=jax.ShapeDtypeStruct((n_blocks * rows, d), BF16),
        input_output_aliases=aliases,
        compiler_params=_cparams("arbitrary"),
        name="experts",
    )(*args)


def _shared_kernel(h2_ref, wsg_ref, wsu_ref, wsd_ref, o_ref):
    h2 = h2_ref[...]
    g = _dot(h2, wsg_ref[...])
    u = _dot(h2, wsu_ref[...])
    o_ref[...] = _dot((g * jax.nn.sigmoid(g) * u).astype(BF16), wsd_ref[...]).astype(BF16)


def _shared(h2, wsg, wsu, wsd, tm=512):
    t, d = h2.shape
    tm = min(tm, t)
    const = lambda i: (0, 0)
    row = pl.BlockSpec((tm, d), lambda i: (i, 0))
    return pl.pallas_call(
        _shared_kernel,
        grid=(t // tm,),
        in_specs=[row, pl.BlockSpec(wsg.shape, const), pl.BlockSpec(wsu.shape, const),
                  pl.BlockSpec(wsd.shape, const)],
        out_specs=row,
        out_shape=jax.ShapeDtypeStruct((t, d), BF16),
        compiler_params=_cparams("arbitrary"),
        name="shared",
    )(h2, wsg, wsu, wsd)


def _final_kernel(x1_ref, sh_ref, y_ref, w_ref, mod_ref, gf_ref, *rest):
    o_ref = rest[-1]
    moe = sh_ref[...].astype(F32)
    for k in range(y_ref.shape[0]):
        moe = moe + y_ref[k].astype(F32) * w_ref[:, k:k + 1]
    x2 = x1_ref[...] + mod_ref[0, 5:6, :] * moe
    ms = jnp.mean(x2 * x2, axis=-1, keepdims=True)
    o_ref[...] = x2 * lax.rsqrt(ms + RMS_EPS) * gf_ref[...]


def _final(x1, shared, y_tok, w_tok, mod3, g_final, out, base_tok, seq, tm=FINAL_ROWS):
    t, d = x1.shape
    nk, tc, _ = y_tok.shape
    per_batch = seq // tm
    b0 = base_tok // tm
    row = pl.BlockSpec((tm, d), lambda i: (b0 + i, 0))
    in_specs = [row, row,
                pl.BlockSpec((nk, tm, d), lambda i: (0, i, 0)),
                pl.BlockSpec((tm, nk), lambda i: (b0 + i, 0)),
                pl.BlockSpec((1, 6, d), lambda i: ((b0 + i) // per_batch, 0, 0)),
                pl.BlockSpec((1, d), lambda i: (0, 0))]
    args = [x1, shared, y_tok, w_tok, mod3, g_final]
    aliases = {}
    if out is not None:
        in_specs.append(pl.BlockSpec(memory_space=pl.ANY))
        args.append(out)
        aliases = {len(args) - 1: 0}
    return pl.pallas_call(
        _final_kernel,
        grid=(tc // tm,),
        in_specs=in_specs,
        out_specs=row,
        out_shape=jax.ShapeDtypeStruct((t, d), F32),
        input_output_aliases=aliases,
        compiler_params=_cparams("arbitrary"),
        name="final",
    )(*args)


def _chunk_bounds(n):
    total = sum(MOE_CHUNK_WEIGHTS)
    edges = [0]
    for i in range(len(MOE_CHUNK_WEIGHTS)):
        edges.append(round(n * sum(MOE_CHUNK_WEIGHTS[:i + 1]) / total))
    return [(a, b) for a, b in zip(edges[:-1], edges[1:]) if b > a]


def _bucket(rel):
    n = jnp.maximum(rel, 0)
    max_exact = REL_BUCKETS // 2
    nf = jnp.maximum(n, 1).astype(F32)
    large = max_exact + (jnp.log(nf / max_exact) / math.log(REL_MAX_DIST / max_exact)
                         * (REL_BUCKETS - max_exact)).astype(jnp.int32)
    large = jnp.minimum(large, REL_BUCKETS - 1)
    return jnp.where(n < max_exact, n, large)


def _bias_kernel(tab_ref, o_ref, *, t):
    h = pl.program_id(0)
    sub = pl.program_id(1)
    key = lax.broadcasted_iota(jnp.int32, (t, t), 0)
    qry = lax.broadcasted_iota(jnp.int32, (t, t), 1)
    rel = qry - key + sub * t
    bucket = _bucket(rel)
    far = tab_ref[h, REL_BUCKETS - 1]
    bias = jnp.zeros((t, t), F32)
    for b in range(REL_BUCKETS - 1):
        bias = jnp.where(bucket == b, tab_ref[h, b] - far, bias)
    o_ref[0, 0] = jnp.where(rel >= 0, bias * LOG2E, NEG_BIG)


def _near_bias(table, t):
    assert t >= REL_MAX_DIST
    heads = table.shape[0]
    return pl.pallas_call(
        functools.partial(_bias_kernel, t=t),
        grid=(heads, 2),
        in_specs=[pl.BlockSpec(memory_space=pltpu.SMEM)],
        out_specs=pl.BlockSpec((1, 1, t, t), lambda h, s: (h, s, 0, 0)),
        out_shape=jax.ShapeDtypeStruct((heads, 2, t, t), F32),
        compiler_params=_cparams("arbitrary", "arbitrary"),
        name="relbias",
    )(table)


def _dispatch(sel, eidx_t, rows):
    e, t = sel.shape
    k = eidx_t.shape[0]
    a = t * k
    eidx = eidx_t.T
    counts = jnp.sum(sel, axis=1)
    rank = jnp.cumsum(sel, axis=1) - sel
    padded = ((counts + rows - 1) // rows) * rows
    pad_ends = jnp.cumsum(padded)
    pad_starts = pad_ends - padded
    starts = jnp.cumsum(counts) - counts
    dest = rank + pad_starts[:, None]
    pos_t = jnp.sum(jnp.where(eidx_t[:, None, :] == jnp.arange(e)[None, :, None], dest[None], 0), axis=1)
    n_blocks = a // rows + e
    n_active = (pad_ends[-1] // rows).astype(jnp.int32)
    end_blocks = pad_ends // rows
    block_expert = jnp.sum(jnp.arange(n_blocks)[:, None] >= end_blocks[None, :], axis=1)
    block_expert = jnp.minimum(block_expert, e - 1).astype(jnp.int32)
    order = jnp.argsort(eidx.reshape(a))
    tok_sorted = (order // k).astype(jnp.int32)
    j = (jnp.arange(n_blocks) * rows - pad_starts[block_expert])[:, None] + jnp.arange(rows)[None, :]
    valid = (j < counts[block_expert][:, None]).reshape(-1)
    src = jnp.clip(starts[block_expert][:, None] + j, 0, a - 1).reshape(-1)
    row_tok = jnp.where(valid, tok_sorted[src], jnp.arange(n_blocks * rows) % t)
    return pos_t, row_tok, block_expert, n_active.reshape(1)


def kernel(x, c, w_ada, b_ada, g_mix, g_ffn, w_in, diff_lambda, diff_subln_g, rel_bias, w_o_diff, w_o_moba,
           w_out, w_router, router_bias, w_exp_gate, w_exp_up, w_exp_down, w_sh_gate, w_sh_up, w_sh_down,
           g_final):
    batch, seq, d = x.shape
    t = batch * seq
    depth = w_ada.shape[0]
    assert seq % MOBA_BLOCK == 0 and seq % min(DIFF_TILE, seq) == 0
    hw = 2 * DIFF_HEAD_DIM
    qk_w = DIFF_HEADS * hw
    moba_w = MOBA_HEADS * MOBA_HEAD_DIM
    o_qm = 3 * qk_w
    o_km = o_qm + moba_w
    o_vm = o_km + moba_w
    o_gd = o_vm + moba_w
    o_gm = o_gd + d
    table_diff = rel_bias[:, :DIFF_HEADS].T
    table_moba = rel_bias[:, DIFF_HEADS:].T
    bias_diff = _near_bias(table_diff, min(DIFF_TILE, seq))
    bias_moba = _near_bias(table_moba, MOBA_BLOCK)
    colscale = jnp.ones((1, w_in.shape[2]), F32)
    colscale = colscale.at[:, :qk_w].set(DIFF_HEAD_DIM ** -0.5 * LOG2E)
    colscale = colscale.at[:, o_qm:o_km].set(MOBA_HEAD_DIM ** -0.5 * LOG2E)
    c_pad = jnp.zeros((8, d), F32).at[:batch].set(c)

    assert depth == 1, "single-layer block: the final norm is fused into the last kernel"
    l = 0
    xc = x.reshape(t, d)
    mod3 = _ada(c_pad, w_ada[l], b_ada[l][None, :])[:batch].reshape(batch, 6, d)
    proj = _inproj(xc, mod3, g_mix[l][None, :], w_in[l].astype(BF16), colscale, seq, tm=min(1024, seq))
    lam_init = 0.8 - 0.6 * math.exp(-0.3 * l)
    lv = diff_lambda[l].astype(F32)
    lam = (jnp.exp(jnp.sum(lv[0] * lv[1])) - jnp.exp(jnp.sum(lv[2] * lv[3])) + lam_init).reshape(1)
    td = min(DIFF_TILE, seq)
    qd_t = proj[:, :qk_w].T
    vd_t = proj[:, 2 * qk_w:o_qm].reshape(t // td, td, qk_w).transpose(0, 2, 1)
    od = _diff_attention(proj, qd_t, vd_t, lam, bias_diff, diff_subln_g[l][:, None], batch, seq,
                         1.0 - lam_init, td)
    qm_t = proj[:, o_qm:o_km].T
    vm_t = proj[:, o_vm:o_gd].reshape(t // MOBA_BLOCK, MOBA_BLOCK, moba_w).transpose(0, 2, 1)
    om = _moba_attention(proj, qm_t, vm_t, bias_moba, batch, seq, o_km)
    x1, h2, logits = _mix(od, om, proj, xc, mod3, w_o_diff[l].astype(BF16), w_o_moba[l].astype(BF16),
                          w_out[l].astype(BF16), g_ffn[l][None, :], w_router[l], seq,
                          o_gd // d, o_gm // d)
    sel, eidx_t, w_t = _route_picks(logits.T, router_bias[l][:, None])
    pos_t, row_tok, block_expert, n_active = _dispatch(sel, eidx_t, EXPERT_ROWS)
    w_tok = w_t.T
    shared = _shared(h2, w_sh_gate[l].astype(BF16), w_sh_up[l].astype(BF16), w_sh_down[l].astype(BF16))
    n_blocks = row_tok.shape[0] // EXPERT_ROWS
    y_pad = None
    for b0, b1 in _chunk_bounds(n_blocks):
        x_chunk = h2[row_tok[b0 * EXPERT_ROWS:b1 * EXPERT_ROWS]]
        y_pad = _experts(block_expert, n_active, x_chunk, w_exp_gate[l], w_exp_up[l], w_exp_down[l],
                         y_pad, b0, n_blocks, EXPERT_ROWS)
    out = None
    for u0, u1 in _chunk_bounds(t // FINAL_ROWS):
        t0, t1 = u0 * FINAL_ROWS, u1 * FINAL_ROWS
        idx = pos_t[:, t0:t1].reshape(-1)
        y_tok = y_pad[idx].reshape(TOPK_EXPERTS, t1 - t0, d)
        out = _final(x1, shared, y_tok, w_tok, mod3, g_final[None, :], out, t0, seq)
    return out.reshape(batch, seq, d)
```
